```python
import math
import jax, jax.numpy as jnp
from jax import lax
import numpy as np

D_MODEL = 2048
BATCH = 4
SEQ = 2048
DEPTH = 2
DEC_BATCH = 128
DEC_SEQ = 8
PAST_LEN = 2048
PAGE_SIZE = 128

HEAD_DIM = 128
N_MIXERS = 4
GROUP_WIDTH = D_MODEL // N_MIXERS
GROUP_HEADS = GROUP_WIDTH // HEAD_DIM
NSA_HEADS = GROUP_HEADS
NSA_BLOCK = 64
NSA_TOP_N = 16
NSA_WINDOW = 512
NSA_FULL_ROWS = 4
NSA_KV_ROWS = 6
CONV_DIM = GROUP_WIDTH
CONV_WIDTH = 31
GM_DIM = GROUP_WIDTH
GM_GROUPS = GROUP_HEADS
GM_GROUP_WIDTH = GM_DIM // GM_GROUPS
GM_CHUNK = 128
SB_HEADS = GROUP_HEADS
D_FF = (11 * D_MODEL) // 4
FFN_CONV_WIDTH = 3
ROPE_THETA = 10000.0
Q_BLOCK = 128
LN_EPS = 1e-5
ALPHA = (2 * DEPTH) ** 0.25
BETA = (8 * DEPTH) ** -0.25
IN_SIZES = (NSA_HEADS * HEAD_DIM, NSA_KV_ROWS * HEAD_DIM, 3 * NSA_HEADS,
            2 * CONV_DIM, 2 * GM_DIM, 3 * SB_HEADS * HEAD_DIM)
IN_COLS = sum(IN_SIZES)

kernel_name = 'hybrid_nsa_conv_gmlp_stickbreak_step'


def layer_norm(x, g, b):
    xf = x.astype(jnp.float32)
    mu = jnp.mean(xf, -1, keepdims=True)
    var = jnp.mean(jnp.square(xf - mu), -1, keepdims=True)
    y = (xf - mu) * lax.rsqrt(var + LN_EPS) * g.astype(jnp.float32) + b.astype(jnp.float32)
    return y.astype(x.dtype)


def rope(x, pos):
    half = x.shape[-1] // 2
    inv = ROPE_THETA ** (-jnp.arange(half, dtype=jnp.float32) / half)
    ang = pos.astype(jnp.float32)[:, None] * inv[None, :]
    ang = ang.reshape((ang.shape[0],) + (1,) * (x.ndim - 3) + (half,))
    cos, sin = jnp.cos(ang), jnp.sin(ang)
    xf = x.astype(jnp.float32)
    x1, x2 = xf[..., :half], xf[..., half:]
    return jnp.concatenate([x1 * cos - x2 * sin, x2 * cos + x1 * sin], -1).astype(x.dtype)


def masked_softmax(s, mask):
    s = jnp.where(mask, s, -jnp.inf)
    m = jnp.max(s, -1, keepdims=True)
    m = jnp.where(jnp.isfinite(m), m, 0.0)
    e = jnp.where(mask, jnp.exp(s - m), 0.0)
    return e / jnp.maximum(jnp.sum(e, -1, keepdims=True), jnp.finfo(jnp.float32).tiny)


def causal_dwconv(x, state, w, b):
    K, C = w.shape
    xp = jnp.concatenate([state, x], 1)
    y = lax.conv_general_dilated(xp, w[:, None, :], (1,), 'VALID',
                                 dimension_numbers=('NWC', 'WIO', 'NWC'), feature_group_count=C)
    return y + b, xp[:, xp.shape[1] - (K - 1):]


def split_blocks(a, nb, qb):
    return a.reshape((a.shape[0], nb, qb) + a.shape[2:]).swapaxes(0, 1)


def merge_blocks(a):
    a = a.swapaxes(0, 1)
    return a.reshape((a.shape[0], a.shape[1] * a.shape[2]) + a.shape[3:])


def nsa_compress(xb, pe, w1, w2):
    h = (xb + pe).reshape(xb.shape[0], xb.shape[1], -1)
    return jax.nn.gelu(h @ w1) @ w2


def nsa_mixer(q, gate_logits, kv_new, past_kv, win_buf, pe, cw1, cw2):
    B, T, H, d = q.shape
    P = past_kv.shape[1]
    L = P + T
    q_pos = P + jnp.arange(T)
    scale = d ** -0.5
    full_new = kv_new[:, :, :NSA_FULL_ROWS]
    win_new = kv_new[:, :, NSA_FULL_ROWS:]
    full = jnp.concatenate([past_kv, full_new], 1)
    n_blk = -(-L // NSA_BLOCK)
    full = jnp.pad(full, ((0, 0), (0, n_blk * NSA_BLOCK - L), (0, 0), (0, 0)))
    blocks = full.reshape(B, n_blk, NSA_BLOCK, NSA_FULL_ROWS, d)
    k_cmp = nsa_compress(blocks[:, :, :, 0], pe[0], cw1[0], cw2[0])
    v_cmp = nsa_compress(blocks[:, :, :, 1], pe[1], cw1[1], cw2[1])
    k_sel, v_sel = blocks[:, :, :, 2], blocks[:, :, :, 3]

    blk = jnp.arange(n_blk)
    cmask = ((blk + 1) * NSA_BLOCK - 1)[None, :] <= q_pos[:, None]
    s_c = jnp.einsum('bthd,bnd->bthn', q, k_cmp).astype(jnp.float32) * scale
    p_c = masked_softmax(s_c, cmask[None, :, None, :])
    o_cmp = jnp.einsum('bthn,bnd->bthd', p_c.astype(v_cmp.dtype), v_cmp)

    cur = q_pos // NSA_BLOCK
    forced = (blk[None] == 0) | (blk[None] == cur[:, None]) | (blk[None] == cur[:, None] - 1)
    started = blk[None] * NSA_BLOCK <= q_pos[:, None]
    imp = jnp.sum(p_c, 2)
    score = jnp.where(forced[None], jnp.inf, jnp.where(started[None], imp, -jnp.inf))
    k_eff = min(NSA_TOP_N, n_blk)
    _, sel_idx = lax.top_k(score, k_eff)

    qb = math.gcd(T, Q_BLOCK)
    nb = T // qb
    q_blk = split_blocks(q, nb, qb)
    idx_blk = split_blocks(sel_idx, nb, qb)
    pos_blk = q_pos.reshape(nb, qb)
    offs = jnp.arange(NSA_BLOCK)
    n_sel = k_eff * NSA_BLOCK

    def sel_step(args):
        qq, ii, pp = args
        kg = jax.vmap(lambda kb, ib: kb[ib])(k_sel, ii).reshape(B, qb, n_sel, d)
        vg = jax.vmap(lambda vb, ib: vb[ib])(v_sel, ii).reshape(B, qb, n_sel, d)
        kpos = (ii[..., None] * NSA_BLOCK + offs).reshape(B, qb, n_sel)
        s = jnp.einsum('bqhd,bqkd->bqhk', qq, kg).astype(jnp.float32) * scale
        p = masked_softmax(s, (kpos <= pp[None, :, None])[:, :, None, :])
        return jnp.einsum('bqhk,bqkd->bqhd', p.astype(vg.dtype), vg)

    o_sel = merge_blocks(lax.map(sel_step, (q_blk, idx_blk, pos_blk)))

    Pw = win_buf.shape[1]
    wcat = jnp.concatenate([win_buf, win_new], 1)
    wpad = jnp.pad(wcat, ((0, 0), (NSA_WINDOW - Pw, 0), (0, 0), (0, 0)))
    band = NSA_WINDOW + qb
    band_off = jnp.arange(band)

    def win_step(args):
        qq, pp, i0 = args
        kv = lax.dynamic_slice_in_dim(wpad, i0, band, axis=1)
        kpos = P - NSA_WINDOW + i0 + band_off
        mask = ((kpos[None] <= pp[:, None]) & (kpos[None] > pp[:, None] - NSA_WINDOW)
                & (kpos[None] >= 0))
        s = jnp.einsum('bqhd,bkd->bqhk', qq, kv[:, :, 0]).astype(jnp.float32) * scale
        p = masked_softmax(s, mask[None, :, None, :])
        return jnp.einsum('bqhk,bkd->bqhd', p.astype(kv.dtype), kv[:, :, 1])

    o_win = merge_blocks(lax.map(win_step, (q_blk, pos_blk, jnp.arange(nb) * qb)))

    g = jax.nn.sigmoid(gate_logits.astype(jnp.float32)).reshape(B, T, H, 3).astype(q.dtype)
    o = g[..., 0:1] * o_cmp + g[..., 1:2] * o_sel + g[..., 2:3] * o_win
    new_win = wcat[:, wcat.shape[1] - min(NSA_WINDOW, L):]
    return o.reshape(B, T, H * d), full_new, new_win


def stick_breaking(q, k, v, past_kv):
    B, T, H, d = q.shape
    P = past_kv.shape[1]
    new_rows = jnp.stack([k, v], 2)
    kf = jnp.concatenate([past_kv[:, :, 0], k], 1)
    vf = jnp.concatenate([past_kv[:, :, 1], v], 1)
    kpos = jnp.arange(P + T)
    q_pos = P + jnp.arange(T)
    scale = d ** -0.5
    qb = math.gcd(T, Q_BLOCK)
    nb = T // qb

    def sb_step(args):
        qq, pp = args
        z = jnp.einsum('bqhd,bkhd->bhqk', qq, kf).astype(jnp.float32) * scale
        mask = (kpos[None, :] < pp[:, None])[None, None]
        log_keep = jnp.where(mask, jax.nn.log_sigmoid(-z), 0.0)
        after = lax.cumsum(log_keep, axis=3, reverse=True) - log_keep
        a = jnp.where(mask, jnp.exp(jax.nn.log_sigmoid(z) + after), 0.0)
        return jnp.einsum('bhqk,bkhd->bqhd', a.astype(vf.dtype), vf)

    o = merge_blocks(lax.map(sb_step, (split_blocks(q, nb, qb), q_pos.reshape(nb, qb))))
    return o.reshape(B, T, H * d), new_rows


def conformer_conv(a, b, state, w, bias, ln_g, ln_b):
    glu = a * jax.nn.sigmoid(b)
    y, new_state = causal_dwconv(glu, state, w, bias)
    return jax.nn.silu(layer_norm(y, ln_g, ln_b)), new_state


def chunk_gmlp(u, v, ln_g, ln_b, ws, bias):
    B, T, C = v.shape
    vn = layer_norm(v, ln_g, ln_b)
    cs = min(GM_CHUNK, T)
    n_ch = -(-T // cs)
    Tp = n_ch * cs
    vp = jnp.pad(vn, ((0, 0), (0, Tp - T), (0, 0))).reshape(B, n_ch, cs, GM_GROUPS, GM_GROUP_WIDTH)
    w = jnp.tril(ws[:, :cs, :cs])
    s = jnp.einsum('gij,bcjgd->bcigd', w, vp) + bias[:, :cs].T[None, None, :, :, None]
    s = s.reshape(B, Tp, C)[:, :T]
    return u * s, vn


def conv_ffn(h, state, w_up, cw, cb, w_down):
    up = h @ w_up
    up, new_state = causal_dwconv(up, state, cw, cb)
    a, b = jnp.split(up, 2, -1)
    return (jax.nn.silu(a) * b) @ w_down, new_state


def decoder_layer(x, past_nsa, win_buf, past_sb, conv_state, ffn_state,
                  w_in, nsa_pe, nsa_cw1, nsa_cw2, conv_w, conv_b, conv_ln_g, conv_ln_b,
                  gm_ln_g, gm_ln_b, gm_ws, gm_bias, w_out, ln1_g, ln1_b,
                  ffn_up, ffn_conv_w, ffn_conv_b, ffn_down, ln2_g, ln2_b):
    B, T, _ = x.shape
    P = past_nsa.shape[1]
    pos = P + jnp.arange(T)
    proj = x @ w_in
    offs = np.cumsum(IN_SIZES)[:-1].tolist()
    q_nsa, kv_nsa, g_nsa, conv_in, gm_in, sb_in = jnp.split(proj, offs, axis=-1)
    q_nsa = rope(q_nsa.reshape(B, T, NSA_HEADS, HEAD_DIM), pos)
    kv = kv_nsa.reshape(B, T, NSA_KV_ROWS // 2, 2, HEAD_DIM)
    keys = rope(kv[:, :, :, 0], pos)
    kv_nsa = jnp.stack([keys, kv[:, :, :, 1]], 3).reshape(B, T, NSA_KV_ROWS, HEAD_DIM)
    o_nsa, nsa_rows, win_new = nsa_mixer(q_nsa, g_nsa, kv_nsa, past_nsa, win_buf, nsa_pe, nsa_cw1, nsa_cw2)
    ca, cb = jnp.split(conv_in, 2, -1)
    o_conv, conv_new = conformer_conv(ca, cb, conv_state, conv_w, conv_b, conv_ln_g, conv_ln_b)
    gu, gv = jnp.split(gm_in, 2, -1)
    o_gm, gm_v = chunk_gmlp(gu, gv, gm_ln_g, gm_ln_b, gm_ws, gm_bias)
    sq, sk, sv = [t.reshape(B, T, SB_HEADS, HEAD_DIM) for t in jnp.split(sb_in, 3, -1)]
    o_sb, sb_rows = stick_breaking(sq, sk, sv, past_sb)
    mix = jnp.concatenate([o_nsa, o_conv, o_gm, o_sb], -1) @ w_out
    h = layer_norm(ALPHA * x + mix, ln1_g, ln1_b)
    f, ffn_new = conv_ffn(h, ffn_state, ffn_up, ffn_conv_w, ffn_conv_b, ffn_down)
    y = layer_norm(ALPHA * h + f, ln2_g, ln2_b)
    return y, nsa_rows, sb_rows, win_new, conv_new, ffn_new, gm_v


def setup_inputs(seed: int = 0) -> dict:
    key = jax.random.key(seed)
    ks = jax.random.split(key, 32)
    f32 = jnp.float32
    n_pages = PAST_LEN // PAGE_SIZE
    n_used = DEC_BATCH * n_pages
    n_pool = n_used + n_used // 4
    w_buf = min(NSA_WINDOW, PAST_LEN)

    def nrm(k, shape, s=1.0):
        return jax.random.normal(k, shape, f32) * s

    page_table = jax.random.permutation(ks[7], n_pool)[:n_used].reshape(DEC_BATCH, n_pages).astype(jnp.int32)
    return {
        'x_prompt': nrm(ks[0], (BATCH, SEQ, D_MODEL)),
        'x_sample': nrm(ks[1], (DEC_BATCH, DEC_SEQ, D_MODEL)),
        'cache_nsa_kv': nrm(ks[2], (DEPTH, n_pool, PAGE_SIZE, NSA_FULL_ROWS, HEAD_DIM)),
        'cache_sb_kv': nrm(ks[3], (DEPTH, n_pool, PAGE_SIZE, 2, SB_HEADS, HEAD_DIM)),
        'state_nsa_win': nrm(ks[4], (DEPTH, DEC_BATCH, w_buf, 2, HEAD_DIM)),
        'state_conv': nrm(ks[5], (DEPTH, DEC_BATCH, CONV_WIDTH - 1, CONV_DIM)),
        'state_ffn': nrm(ks[6], (DEPTH, DEC_BATCH, FFN_CONV_WIDTH - 1, 2 * D_FF)),
        'page_table': page_table,
        'w_in': nrm(ks[8], (DEPTH, D_MODEL, IN_COLS), D_MODEL ** -0.5),
        'nsa_pe': nrm(ks[9], (DEPTH, 2, NSA_BLOCK, HEAD_DIM), 0.1),
        'nsa_cw1': nrm(ks[10], (DEPTH, 2, NSA_BLOCK * HEAD_DIM, HEAD_DIM), (NSA_BLOCK * HEAD_DIM) ** -0.5),
        'nsa_cw2': nrm(ks[11], (DEPTH, 2, HEAD_DIM, HEAD_DIM), HEAD_DIM ** -0.5),
        'conv_w': nrm(ks[12], (DEPTH, CONV_WIDTH, CONV_DIM), CONV_WIDTH ** -0.5),
        'conv_b': nrm(ks[13], (DEPTH, CONV_DIM), 0.02),
        'conv_ln_g': 1.0 + nrm(ks[14], (DEPTH, CONV_DIM), 0.02),
        'conv_ln_b': nrm(ks[15], (DEPTH, CONV_DIM), 0.02),
        'gm_ln_g': 1.0 + nrm(ks[16], (DEPTH, GM_DIM), 0.02),
        'gm_ln_b': nrm(ks[17], (DEPTH, GM_DIM), 0.02),
        'gm_ws': nrm(ks[18], (DEPTH, GM_GROUPS, GM_CHUNK, GM_CHUNK), GM_CHUNK ** -0.5),
        'gm_bias': 1.0 + nrm(ks[19], (DEPTH, GM_GROUPS, GM_CHUNK), 0.02),
        'w_out': nrm(ks[20], (DEPTH, D_MODEL, D_MODEL), BETA * D_MODEL ** -0.5),
        'ln1_g': 1.0 + nrm(ks[21], (DEPTH, D_MODEL), 0.02),
        'ln1_b': nrm(ks[22], (DEPTH, D_MODEL), 0.02),
        'ffn_up': nrm(ks[23], (DEPTH, D_MODEL, 2 * D_FF), D_MODEL ** -0.5),
        'ffn_conv_w': nrm(ks[24], (DEPTH, FFN_CONV_WIDTH, 2 * D_FF), FFN_CONV_WIDTH ** -0.5),
        'ffn_conv_b': nrm(ks[25], (DEPTH, 2 * D_FF), 0.02),
        'ffn_down': nrm(ks[26], (DEPTH, D_FF, D_MODEL), BETA * D_FF ** -0.5),
        'ln2_g': 1.0 + nrm(ks[27], (DEPTH, D_MODEL), 0.02),
        'ln2_b': nrm(ks[28], (DEPTH, D_MODEL), 0.02),
    }


def reference(x_prompt, x_sample, cache_nsa_kv, cache_sb_kv, state_nsa_win, state_conv, state_ffn,
              page_table, w_in, nsa_pe, nsa_cw1, nsa_cw2, conv_w, conv_b, conv_ln_g, conv_ln_b,
              gm_ln_g, gm_ln_b, gm_ws, gm_bias, w_out, ln1_g, ln1_b,
              ffn_up, ffn_conv_w, ffn_conv_b, ffn_down, ln2_g, ln2_b):
    n_pages = page_table.shape[1]

    def gather_pages(pool):
        g = pool[page_table]
        return g.reshape((g.shape[0], n_pages * g.shape[2]) + g.shape[3:])

    xp, xs = x_prompt, x_sample
    bp, dt = xp.shape[0], xp.dtype
    outs_p, outs_s = [], []
    for l in range(DEPTH):
        lw = (w_in[l], nsa_pe[l], nsa_cw1[l], nsa_cw2[l], conv_w[l], conv_b[l], conv_ln_g[l], conv_ln_b[l],
              gm_ln_g[l], gm_ln_b[l], gm_ws[l], gm_bias[l], w_out[l], ln1_g[l], ln1_b[l],
              ffn_up[l], ffn_conv_w[l], ffn_conv_b[l], ffn_down[l], ln2_g[l], ln2_b[l])
        xp, *sp = decoder_layer(
            xp,
            jnp.zeros((bp, 0, NSA_FULL_ROWS, HEAD_DIM), dt),
            jnp.zeros((bp, 0, 2, HEAD_DIM), dt),
            jnp.zeros((bp, 0, 2, SB_HEADS, HEAD_DIM), dt),
            jnp.zeros((bp, CONV_WIDTH - 1, CONV_DIM), dt),
            jnp.zeros((bp, FFN_CONV_WIDTH - 1, 2 * D_FF), dt),
            *lw)
        xs, *ss = decoder_layer(
            xs, gather_pages(cache_nsa_kv[l]), state_nsa_win[l], gather_pages(cache_sb_kv[l]),
            state_conv[l], state_ffn[l], *lw)
        outs_p.append(sp)
        outs_s.append(ss)
    nsa_kv_p = jnp.stack([o[0] for o in outs_p])
    nsa_kv_s = jnp.stack([o[0] for o in outs_s])
    sb_kv_p = jnp.stack([o[1] for o in outs_p])
    sb_kv_s = jnp.stack([o[1] for o in outs_s])
    win_p = jnp.stack([o[2] for o in outs_p])
    win_s = jnp.stack([o[2] for o in outs_s])
    conv_p = jnp.stack([o[3] for o in outs_p])
    conv_s = jnp.stack([o[3] for o in outs_s])
    ffn_p = jnp.stack([o[4] for o in outs_p])
    ffn_s = jnp.stack([o[4] for o in outs_s])
    gmv_s = jnp.stack([o[5] for o in outs_s])
    return (xp, xs, nsa_kv_p, nsa_kv_s, sb_kv_p, sb_kv_s, win_p, win_s, conv_p, conv_s, ffn_p, ffn_s, gmv_s)
```

```python
import functools
import math

import jax
import jax.numpy as jnp
import numpy as np
from jax import lax
from jax.experimental import pallas as pl
from jax.experimental.pallas import tpu as pltpu

F32 = jnp.float32
BF16 = jnp.bfloat16

D_MODEL = 2048
HEAD_DIM = 128
HEADS = 4
GROUP_WIDTH = HEADS * HEAD_DIM
NSA_BLOCK = 64
NSA_TOP_N = 16
NSA_WINDOW = 512
CONV_WIDTH = 31
GM_CHUNK = 128
D_FF = (11 * D_MODEL) // 4
FFN_CONV_WIDTH = 3
ROPE_THETA = 10000.0
LN_EPS = 1e-5
DEPTH = 2
ALPHA = (2 * DEPTH) ** 0.25
PAGE_SIZE = 128
SCALE = HEAD_DIM ** -0.5

VMEM_LIMIT_V7X = 56 * 1024 * 1024
SUBLANES = 8
LANES = 128

C_CONV = 0
C_GM = C_CONV + 2 * GROUP_WIDTH
C_SQ = C_GM + 2 * GROUP_WIDTH
C_SK = C_SQ + GROUP_WIDTH
C_SV = C_SK + GROUP_WIDTH
C_Q = C_SV + GROUP_WIDTH
C_KV = C_Q + GROUP_WIDTH
C_G = C_KV + 6 * HEAD_DIM
N_PROJ = 5120
PROJ_TN = 512


def _cparams(sem):
    return pltpu.CompilerParams(dimension_semantics=sem, vmem_limit_bytes=VMEM_LIMIT_V7X)


def _layer_norm(x, g, b):
    mu = jnp.mean(x, -1, keepdims=True)
    xc = x - mu
    var = jnp.mean(xc * xc, -1, keepdims=True)
    return xc * lax.rsqrt(var + LN_EPS) * g + b


def _masked_softmax(s, mask):
    sm = jnp.where(mask, s, -1e30)
    m = jnp.max(sm, -1, keepdims=True)
    e = jnp.where(mask, jnp.exp(sm - m), 0.0)
    return e / jnp.maximum(jnp.sum(e, -1, keepdims=True), jnp.finfo(F32).tiny)


def _dot_t(a, b):
    return lax.dot_general(a, b, (((1,), (1,)), ((), ())), preferred_element_type=F32)


def _dot(a, b):
    return jnp.dot(a, b, preferred_element_type=F32)


def _split_dot(x, u16):
    hi = x.astype(BF16)
    lo = (x - hi.astype(F32)).astype(BF16)
    return _dot(hi, u16) + _dot(lo, u16)


def _proj_kernel(x_ref, w_ref, o32_ref, o16_ref):
    acc = _dot(x_ref[...], w_ref[...])
    o32_ref[...] = acc
    o16_ref[...] = acc.astype(BF16)


def _proj(x16, w16, tm):
    M, K = x16.shape
    N = w16.shape[1]
    return pl.pallas_call(
        _proj_kernel,
        grid=(M // tm, N // PROJ_TN),
        in_specs=[pl.BlockSpec((tm, K), lambda i, j: (i, 0)),
                  pl.BlockSpec((K, PROJ_TN), lambda i, j: (0, j))],
        out_specs=[pl.BlockSpec((tm, PROJ_TN), lambda i, j: (i, j)),
                   pl.BlockSpec((tm, PROJ_TN), lambda i, j: (i, j))],
        out_shape=[jax.ShapeDtypeStruct((M, N), F32), jax.ShapeDtypeStruct((M, N), BF16)],
        compiler_params=_cparams(("parallel", "parallel")),
        name="proj",
    )(x16, w16)


def _rope_kernel(q_ref, kva_ref, kvb_ref, kvc_ref, cos_ref, sin_ref, pe_ref,
                 q16_ref, kv32_ref, kv16_ref, xkv_ref):
    cos = cos_ref[...]
    sin = sin_ref[...]

    def rot(x):
        return x * cos + pltpu.roll(x, HEAD_DIM // 2, axis=1) * sin

    for h in range(HEADS):
        sl = slice(h * HEAD_DIM, (h + 1) * HEAD_DIM)
        q16_ref[:, sl] = rot(q_ref[:, sl]).astype(BF16)
    for p, ref in enumerate((kva_ref, kvb_ref, kvc_ref)):
        k = rot(ref[:, :HEAD_DIM])
        v = ref[:, HEAD_DIM:]
        ks = slice(2 * p * HEAD_DIM, (2 * p + 1) * HEAD_DIM)
        vs = slice((2 * p + 1) * HEAD_DIM, (2 * p + 2) * HEAD_DIM)
        kv32_ref[:, ks] = k
        kv32_ref[:, vs] = v
        kv16_ref[:, ks] = k.astype(BF16)
        kv16_ref[:, vs] = v.astype(BF16)
        if p == 0:
            xkv_ref[0] = (k + pe_ref[0]).astype(BF16)
            xkv_ref[1] = (v + pe_ref[1]).astype(BF16)


def _rope(proj32, cos2, sin2, pe_t, tm):
    M = proj32.shape[0]
    qb = C_Q // GROUP_WIDTH
    kb = C_KV // (2 * HEAD_DIM)
    row = lambda i: (i, 0)
    return pl.pallas_call(
        _rope_kernel,
        grid=(M // tm,),
        in_specs=[pl.BlockSpec((tm, GROUP_WIDTH), lambda i: (i, qb)),
                  pl.BlockSpec((tm, 2 * HEAD_DIM), lambda i: (i, kb)),
                  pl.BlockSpec((tm, 2 * HEAD_DIM), lambda i: (i, kb + 1)),
                  pl.BlockSpec((tm, 2 * HEAD_DIM), lambda i: (i, kb + 2)),
                  pl.BlockSpec((tm, HEAD_DIM), row),
                  pl.BlockSpec((tm, HEAD_DIM), row),
                  pl.BlockSpec((2, tm, HEAD_DIM), lambda i: (0, 0, 0))],
        out_specs=[pl.BlockSpec((tm, GROUP_WIDTH), row),
                   pl.BlockSpec((tm, 6 * HEAD_DIM), row),
                   pl.BlockSpec((tm, 6 * HEAD_DIM), row),
                   pl.BlockSpec((2, tm, HEAD_DIM), lambda i: (0, i, 0))],
        out_shape=[jax.ShapeDtypeStruct((M, GROUP_WIDTH), BF16),
                   jax.ShapeDtypeStruct((M, 6 * HEAD_DIM), F32),
                   jax.ShapeDtypeStruct((M, 6 * HEAD_DIM), BF16),
                   jax.ShapeDtypeStruct((2, M, HEAD_DIM), BF16)],
        compiler_params=_cparams(("parallel",)),
        name="rope",
    )(proj32, proj32, proj32, proj32, cos2, sin2, pe_t)


def _gelu_tanh(x):
    return x * (0.5 * (1.0 + jnp.tanh(math.sqrt(2.0 / math.pi) * (x + 0.044715 * (x * x * x)))))


def _cmp_kernel(x_ref, w1_ref, w2_ref, o_ref):
    h = _gelu_tanh(_dot(x_ref[0], w1_ref[0]))
    o_ref[0] = _dot(h.astype(BF16), w2_ref[0])


def _compress(x16, w1, w2, tm):
    _, R, K = x16.shape
    d = w1.shape[2]
    return pl.pallas_call(
        _cmp_kernel,
        grid=(2, R // tm),
        in_specs=[pl.BlockSpec((1, tm, K), lambda s, i: (s, i, 0)),
                  pl.BlockSpec((1, K, d), lambda s, i: (s, 0, 0)),
                  pl.BlockSpec((1, d, d), lambda s, i: (s, 0, 0))],
        out_specs=pl.BlockSpec((1, tm, d), lambda s, i: (s, i, 0)),
        out_shape=jax.ShapeDtypeStruct((2, R, d), F32),
        compiler_params=_cparams(("parallel", "parallel")),
        name="nsa_compress",
    )(x16, w1, w2)


def _select_blocks(imp, tpos, n_blk, n_real):
    nidx = lax.broadcasted_iota(jnp.int32, imp.shape, 1)
    cur = tpos // NSA_BLOCK
    forced = (nidx == 0) | (nidx == cur) | (nidx == cur - 1)
    started = nidx * NSA_BLOCK <= tpos
    score = jnp.where(forced, jnp.inf, jnp.where(started, imp, -jnp.inf))
    score = jnp.where(nidx < n_real, score, -jnp.inf)
    rank = jnp.zeros(imp.shape, jnp.int32)
    for n2 in range(n_real):
        c = score[:, n2:n2 + 1]
        ahead = (c > score) | ((c == score) & (nidx > n2))
        rank = rank + ahead.astype(jnp.int32)
    return (rank < min(NSA_TOP_N, n_real)) & (nidx < n_real)


def _gate_cols(gs, h):
    return [gs[:, 3 * h + c:3 * h + c + 1] for c in range(3)]


def _nsa_prompt_kernel(q_ref, ks_ref, vs_ref, kw_ref, vw_ref, kc_ref, vc_ref, g_ref, o_ref, *, T, TQ):
    i = pl.program_id(1)
    n_blk = T // NSA_BLOCK
    t0 = i * TQ
    tpos = t0 + lax.broadcasted_iota(jnp.int32, (TQ, 1), 0)
    kc = kc_ref[0].astype(BF16)
    vc = vc_ref[0].astype(BF16)
    nidx = lax.broadcasted_iota(jnp.int32, (TQ, n_blk), 1)
    cmask = (nidx + 1) * NSA_BLOCK - 1 <= tpos
    gs = jax.nn.sigmoid(g_ref[...])

    qs = [q_ref[:, h * HEAD_DIM:(h + 1) * HEAD_DIM] for h in range(HEADS)]
    imp = jnp.zeros((TQ, n_blk), F32)
    o_cmp = []
    for h in range(HEADS):
        p = _masked_softmax(_dot_t(qs[h], kc) * SCALE, cmask)
        imp = imp + p
        o_cmp.append(_dot(p.astype(BF16), vc))

    sel = _select_blocks(imp, tpos, n_blk, n_blk)
    expand = (lax.broadcasted_iota(jnp.int32, (n_blk, T), 1) // NSA_BLOCK
              == lax.broadcasted_iota(jnp.int32, (n_blk, T), 0))
    selk = _dot(jnp.where(sel, 1.0, 0.0).astype(BF16), jnp.where(expand, 1.0, 0.0).astype(BF16))
    kpos = lax.broadcasted_iota(jnp.int32, (TQ, T), 1)
    smask = (selk > 0.5) & (kpos <= tpos)

    band = min(NSA_WINDOW + TQ, T)
    w0 = pl.multiple_of(jnp.clip(t0 - NSA_WINDOW, 0, T - band), LANES)
    kw = kw_ref[pl.ds(w0, band), :]
    vw = vw_ref[pl.ds(w0, band), :]
    wpos = w0 + lax.broadcasted_iota(jnp.int32, (TQ, band), 1)
    wmask = (wpos <= tpos) & (wpos > tpos - NSA_WINDOW)

    ks = ks_ref[...]
    vs = vs_ref[...]
    for h in range(HEADS):
        p = _masked_softmax(_dot_t(qs[h], ks) * SCALE, smask)
        o_sel = _dot(p.astype(BF16), vs)
        pw = _masked_softmax(_dot_t(qs[h], kw) * SCALE, wmask)
        o_win = _dot(pw.astype(BF16), vw)
        g0, g1, g2 = _gate_cols(gs, h)
        o = g0 * o_cmp[h] + g1 * o_sel + g2 * o_win
        o_ref[:, h * HEAD_DIM:(h + 1) * HEAD_DIM] = o.astype(BF16)


def _nsa_prompt(q16, kv16, kcmp, vcmp, proj32, B, T):
    TQ = min(128, T)
    nq = T // TQ
    n_blk = T // NSA_BLOCK
    kvcol = lambda c: pl.BlockSpec((T, HEAD_DIM), lambda b, i: (b, c))
    return pl.pallas_call(
        functools.partial(_nsa_prompt_kernel, T=T, TQ=TQ),
        grid=(B, nq),
        in_specs=[pl.BlockSpec((TQ, GROUP_WIDTH), lambda b, i: (b * nq + i, 0)),
                  kvcol(2), kvcol(3), kvcol(4), kvcol(5),
                  pl.BlockSpec((1, n_blk, HEAD_DIM), lambda b, i: (b, 0, 0)),
                  pl.BlockSpec((1, n_blk, HEAD_DIM), lambda b, i: (b, 0, 0)),
                  pl.BlockSpec((TQ, LANES), lambda b, i: (b * nq + i, C_G // LANES))],
        out_specs=pl.BlockSpec((TQ, GROUP_WIDTH), lambda b, i: (b * nq + i, 0)),
        out_shape=jax.ShapeDtypeStruct((B * T, GROUP_WIDTH), BF16),
        compiler_params=_cparams(("parallel", "arbitrary")),
        name="nsa_prompt",
    )(q16, kv16, kv16, kv16, kv16, kcmp, vcmp, proj32)


def _gather_cmp_kernel(pt_ref, pe_ref, *refs, n_pages):
    pages = refs[:n_pages]
    xkv_ref, = refs[n_pages:]
    pe_k = pe_ref[0]
    pe_v = pe_ref[1]
    for j in range(n_pages):
        sl = slice(j * PAGE_SIZE, (j + 1) * PAGE_SIZE)
        xkv_ref[0, 0, sl, :] = (pages[j][0, 0, :, :HEAD_DIM] + pe_k).astype(BF16)
        xkv_ref[1, 0, sl, :] = (pages[j][0, 0, :, HEAD_DIM:] + pe_v).astype(BF16)


def _gather_cmp(cache4, layer, page_table, pe_page):
    B, n_pages = page_table.shape
    P = n_pages * PAGE_SIZE

    def page_spec(j):
        return pl.BlockSpec((1, 1, PAGE_SIZE, 2 * HEAD_DIM), lambda b, pt: (layer, pt[b, j], 0, 0))

    out = pl.BlockSpec((2, 1, P, HEAD_DIM), lambda b, pt: (0, b, 0, 0))
    return pl.pallas_call(
        functools.partial(_gather_cmp_kernel, n_pages=n_pages),
        grid_spec=pltpu.PrefetchScalarGridSpec(
            num_scalar_prefetch=1, grid=(B,),
            in_specs=[pl.BlockSpec((2, PAGE_SIZE, HEAD_DIM), lambda b, pt: (0, 0, 0))]
            + [page_spec(j) for j in range(n_pages)],
            out_specs=out),
        out_shape=jax.ShapeDtypeStruct((2, B, P, HEAD_DIM), BF16),
        compiler_params=_cparams(("arbitrary",)),
        name="nsa_gather_cmp",
    )(page_table, pe_page, *([cache4] * n_pages))


def _nsa_sample_kernel(pt_ref, q_ref, kvn_ref, win_ref, kc_ref, vc_ref, g_ref, *refs, T, P, n_pages, n_blk, n_pad):
    pages = refs[:n_pages]
    o_ref = refs[n_pages]
    R = HEADS * T
    row = lax.broadcasted_iota(jnp.int32, (R, 1), 0)
    tq = row % T
    tpos = P + tq
    q_bt = q_ref[0]
    q = jnp.concatenate([q_bt[:, h * HEAD_DIM:(h + 1) * HEAD_DIM] for h in range(HEADS)], axis=0).astype(BF16)
    gs = jax.nn.sigmoid(g_ref[0])

    kc = kc_ref[0].astype(BF16)
    vc = vc_ref[0].astype(BF16)
    nidx = lax.broadcasted_iota(jnp.int32, (R, n_pad), 1)
    cmask = ((nidx + 1) * NSA_BLOCK - 1 <= tpos) & (nidx < n_blk)
    p_c = _masked_softmax(_dot_t(q, kc) * SCALE, cmask)
    o_cmp = _dot(p_c.astype(BF16), vc)
    imp = p_c[0:T]
    for h in range(1, HEADS):
        imp = imp + p_c[h * T:(h + 1) * T]
    sel_t = _select_blocks(imp, P + lax.broadcasted_iota(jnp.int32, (T, 1), 0), n_pad, n_blk)
    sel = jnp.concatenate([jnp.where(sel_t, 1.0, 0.0)] * HEADS, axis=0)

    lane = lax.broadcasted_iota(jnp.int32, (R, PAGE_SIZE), 1)
    kvn = kvn_ref[0]
    pad_rows = jnp.zeros((PAGE_SIZE - T, HEAD_DIM), F32)
    new_chunk = lambda c: jnp.concatenate([kvn[:, c * HEAD_DIM:(c + 1) * HEAD_DIM], pad_rows], axis=0).astype(BF16)
    ks_new, vs_new, kw_new, vw_new = new_chunk(2), new_chunk(3), new_chunk(4), new_chunk(5)
    blocks_per_page = PAGE_SIZE // NSA_BLOCK
    s_list, m_list, v_list = [], [], []
    for j in range(n_pages + 1):
        if j < n_pages:
            kj = pages[j][0, 0, :, :HEAD_DIM].astype(BF16)
            vj = pages[j][0, 0, :, HEAD_DIM:].astype(BF16)
        else:
            kj, vj = ks_new, vs_new
        kpos = j * PAGE_SIZE + lane
        chosen = jnp.zeros((R, PAGE_SIZE), F32)
        for u in range(blocks_per_page):
            n = j * blocks_per_page + u
            if n < n_blk:
                in_blk = (lane >= u * NSA_BLOCK) & (lane < (u + 1) * NSA_BLOCK)
                chosen = jnp.where(in_blk, sel[:, n:n + 1], chosen)
        m_list.append((chosen > 0.5) & (kpos <= tpos))
        s_list.append(_dot_t(q, kj) * SCALE)
        v_list.append(vj)
    o_sel = _multi_chunk_attention(s_list, m_list, v_list)

    W = win_ref.shape[1]
    n_wc = W // PAGE_SIZE
    s_list, m_list, v_list = [], [], []
    for j in range(n_wc + 1):
        if j < n_wc:
            kj = win_ref[0, j * PAGE_SIZE:(j + 1) * PAGE_SIZE, :HEAD_DIM].astype(BF16)
            vj = win_ref[0, j * PAGE_SIZE:(j + 1) * PAGE_SIZE, HEAD_DIM:].astype(BF16)
            kpos = P - W + j * PAGE_SIZE + lane
        else:
            kj, vj = kw_new, vw_new
            kpos = P + lane
        m_list.append((kpos <= tpos) & (kpos > tpos - NSA_WINDOW) & (kpos >= 0))
        s_list.append(_dot_t(q, kj) * SCALE)
        v_list.append(vj)
    o_win = _multi_chunk_attention(s_list, m_list, v_list)

    for h in range(HEADS):
        g0, g1, g2 = _gate_cols(gs, h)
        rs = slice(h * T, (h + 1) * T)
        o_ref[0, :, h * HEAD_DIM:(h + 1) * HEAD_DIM] = g0 * o_cmp[rs] + g1 * o_sel[rs] + g2 * o_win[rs]


def _multi_chunk_attention(s_list, m_list, v_list):
    sm = [jnp.where(m, s, -1e30) for s, m in zip(s_list, m_list)]
    mx = sm[0]
    for s in sm[1:]:
        mx = jnp.maximum(mx, s)
    mx = jnp.max(mx, -1, keepdims=True)
    es = [jnp.where(m, jnp.exp(s - mx), 0.0) for s, m in zip(sm, m_list)]
    tot = es[0]
    for e in es[1:]:
        tot = tot + e
    den = jnp.maximum(jnp.sum(tot, -1, keepdims=True), jnp.finfo(F32).tiny)
    acc = _dot(es[0].astype(BF16), v_list[0])
    for e, v in zip(es[1:], v_list[1:]):
        acc = acc + _dot(e.astype(BF16), v)
    return acc / den


def _nsa_sample(q_b, kv_b, win_state, kcmp, vcmp, g32_b, cache4, layer, page_table):
    B, T, _ = q_b.shape
    n_pages = page_table.shape[1]
    P = n_pages * PAGE_SIZE
    n_blk = -(-(P + T) // NSA_BLOCK)
    n_pad = kcmp.shape[1]
    W = win_state.shape[1]
    per_b = lambda *tail: pl.BlockSpec((1,) + tail, lambda b, pt: (b,) + (0,) * len(tail))

    def page_spec(j):
        return pl.BlockSpec((1, 1, PAGE_SIZE, 2 * HEAD_DIM), lambda b, pt: (layer, pt[b, j], 0, 1))

    return pl.pallas_call(
        functools.partial(_nsa_sample_kernel, T=T, P=P, n_pages=n_pages, n_blk=n_blk, n_pad=n_pad),
        grid_spec=pltpu.PrefetchScalarGridSpec(
            num_scalar_prefetch=1, grid=(B,),
            in_specs=[per_b(T, GROUP_WIDTH), per_b(T, 6 * HEAD_DIM), per_b(W, 2 * HEAD_DIM),
                      per_b(n_pad, HEAD_DIM), per_b(n_pad, HEAD_DIM), per_b(T, LANES)]
            + [page_spec(j) for j in range(n_pages)],
            out_specs=per_b(T, GROUP_WIDTH)),
        out_shape=jax.ShapeDtypeStruct((B, T, GROUP_WIDTH), F32),
        compiler_params=_cparams(("arbitrary",)),
        name="nsa_sample",
    )(page_table, q_b, kv_b, win_state, kcmp, vcmp, g32_b, *([cache4] * n_pages))


def _log_sigmoid_pair(z):
    ls_pos = jnp.minimum(z, 0.0) - jnp.log1p(jnp.exp(-jnp.abs(z)))
    return ls_pos, ls_pos - z


def _strict_upper_ones(n):
    r = lax.broadcasted_iota(jnp.int32, (n, n), 0)
    c = lax.broadcasted_iota(jnp.int32, (n, n), 1)
    return jnp.where(r > c, 1.0, 0.0).astype(BF16)


def _sb_prompt_kernel(q_ref, k_ref, v_ref, o_ref, *, TQ, KC):
    qi = pl.program_id(2)
    q = q_ref[...]
    t0 = qi * TQ
    n_chunks = (t0 + TQ) // KC
    later = _strict_upper_ones(KC)
    tpos = t0 + lax.broadcasted_iota(jnp.int32, (TQ, 1), 0)
    lane = lax.broadcasted_iota(jnp.int32, (TQ, KC), 1)

    def body(c, carry):
        acc, after_c = carry
        k0 = pl.multiple_of((n_chunks - 1 - c) * KC, KC)
        z = _dot_t(q, k_ref[pl.ds(k0, KC), :]) * SCALE
        mask = k0 + lane < tpos
        ls_pos, ls_neg = _log_sigmoid_pair(z)
        log_keep = jnp.where(mask, ls_neg, 0.0)
        after = _split_dot(log_keep, later) + after_c
        a = jnp.where(mask, jnp.exp(ls_pos + after), 0.0)
        acc = acc + _dot(a.astype(BF16), v_ref[pl.ds(k0, KC), :])
        return acc, after_c + jnp.sum(log_keep, -1, keepdims=True)

    acc, _ = lax.fori_loop(0, n_chunks, body, (jnp.zeros((TQ, HEAD_DIM), F32), jnp.zeros((TQ, 1), F32)))
    o_ref[...] = acc.astype(BF16)


def _sb_prompt(proj16, B, T):
    TQ = min(512, T)
    KC = min(256, T)
    nq = T // TQ
    col = lambda base: (lambda b, h, i: (b, base // HEAD_DIM + h))
    return pl.pallas_call(
        functools.partial(_sb_prompt_kernel, TQ=TQ, KC=KC),
        grid=(B, HEADS, nq),
        in_specs=[pl.BlockSpec((TQ, HEAD_DIM), lambda b, h, i: (b * nq + i, C_SQ // HEAD_DIM + h)),
                  pl.BlockSpec((T, HEAD_DIM), col(C_SK)),
                  pl.BlockSpec((T, HEAD_DIM), col(C_SV))],
        out_specs=pl.BlockSpec((TQ, HEAD_DIM), lambda b, h, i: (b * nq + i, h)),
        out_shape=jax.ShapeDtypeStruct((B * T, GROUP_WIDTH), BF16),
        compiler_params=_cparams(("parallel", "parallel", "arbitrary")),
        name="sb_prompt",
    )(proj16, proj16, proj16)


def _sb_sample_kernel(pt_ref, qkv_ref, *refs, T, P, n_pages):
    pages = refs[:n_pages]
    o_ref = refs[n_pages]
    R = HEADS * T
    C = PAGE_SIZE
    head = lax.broadcasted_iota(jnp.int32, (R, 1), 0) // T
    qkv = qkv_ref[0]
    q_t = qkv[:, :GROUP_WIDTH]
    col_head = lax.broadcasted_iota(jnp.int32, (R, GROUP_WIDTH), 1) // HEAD_DIM
    q_bd = jnp.where(col_head == head, jnp.concatenate([q_t] * HEADS, axis=0), 0.0).astype(BF16)
    pad_rows = jnp.zeros((C - T, GROUP_WIDTH), F32)
    k_new = jnp.concatenate([qkv[:, GROUP_WIDTH:2 * GROUP_WIDTH], pad_rows], axis=0).astype(BF16)
    v_new = jnp.concatenate([qkv[:, 2 * GROUP_WIDTH:], pad_rows], axis=0).astype(BF16)

    n_ch = n_pages + 1
    zs, vs = [], []
    for j in range(n_ch):
        if j < n_pages:
            kj = pages[j][0, 0, :, :GROUP_WIDTH].astype(BF16)
            vj = pages[j][0, 0, :, GROUP_WIDTH:].astype(BF16)
        else:
            kj, vj = k_new, v_new
        zs.append(_dot_t(q_bd, kj) * SCALE)
        vs.append(vj)
    z = jnp.concatenate(zs, axis=0)
    crow = lax.broadcasted_iota(jnp.int32, (n_ch * R, 1), 0)
    kpos = (crow // R) * C + lax.broadcasted_iota(jnp.int32, (n_ch * R, C), 1)
    mask = kpos < P + (crow % R) % T
    ls_pos, ls_neg = _log_sigmoid_pair(z)
    log_keep = jnp.where(mask, ls_neg, 0.0)
    after_local = _split_dot(log_keep, _strict_upper_ones(C))
    tot = jnp.sum(log_keep, -1, keepdims=True)
    carry = jnp.zeros((R, 1), F32)
    carries = [None] * n_ch
    for j in range(n_ch - 1, -1, -1):
        carries[j] = carry
        carry = carry + tot[j * R:(j + 1) * R]
    after = after_local + jnp.concatenate(carries, axis=0)
    a = jnp.where(mask, jnp.exp(ls_pos + after), 0.0).astype(BF16)
    acc = jnp.zeros((R, GROUP_WIDTH), F32)
    for j in range(n_ch):
        acc = acc + _dot(a[j * R:(j + 1) * R], vs[j])
    for h in range(HEADS):
        o_ref[0, :, h * HEAD_DIM:(h + 1) * HEAD_DIM] = acc[h * T:(h + 1) * T, h * HEAD_DIM:(h + 1) * HEAD_DIM]


def _sb_sample(qkv_b, cache_sb4, layer, page_table):
    B, T, _ = qkv_b.shape
    n_pages = page_table.shape[1]
    P = n_pages * PAGE_SIZE

    def page_spec(j):
        return pl.BlockSpec((1, 1, PAGE_SIZE, 2 * GROUP_WIDTH), lambda b, pt: (layer, pt[b, j], 0, 0))

    return pl.pallas_call(
        functools.partial(_sb_sample_kernel, T=T, P=P, n_pages=n_pages),
        grid_spec=pltpu.PrefetchScalarGridSpec(
            num_scalar_prefetch=1, grid=(B,),
            in_specs=[pl.BlockSpec((1, T, 3 * GROUP_WIDTH), lambda b, pt: (b, 0, 0))]
            + [page_spec(j) for j in range(n_pages)],
            out_specs=pl.BlockSpec((1, T, GROUP_WIDTH), lambda b, pt: (b, 0, 0))),
        out_shape=jax.ShapeDtypeStruct((B, T, GROUP_WIDTH), F32),
        compiler_params=_cparams(("arbitrary",)),
        name="sb_sample",
    )(page_table, qkv_b, *([cache_sb4] * n_pages))


CONV_HIST = 32
CONV_ROWS = 32


def _conv_prompt_kernel(ab_ref, w_ref, b_ref, g_ref, be_ref, o_ref, tail_ref, buf_ref, *, tm):
    i = pl.program_id(1)

    @pl.when(i == 0)
    def _():
        buf_ref[0:CONV_HIST, :] = jnp.zeros((CONV_HIST, GROUP_WIDTH), F32)

    @pl.when(i > 0)
    def _():
        buf_ref[0:CONV_HIST, :] = buf_ref[tm:tm + CONV_HIST, :]

    a = ab_ref[:, :GROUP_WIDTH]
    glu = a * jax.nn.sigmoid(ab_ref[:, GROUP_WIDTH:])
    buf_ref[CONV_HIST:, :] = glu
    tail_ref[0] = glu[tm - CONV_HIST:, :]
    w = w_ref[...]
    for r in range(tm // CONV_ROWS):
        acc = jnp.broadcast_to(b_ref[...], (CONV_ROWS, GROUP_WIDTH))
        for k in range(CONV_WIDTH):
            start = CONV_HIST + r * CONV_ROWS - (CONV_WIDTH - 1) + k
            acc = acc + w[k:k + 1, :] * buf_ref[start:start + CONV_ROWS, :]
        y = _layer_norm(acc, g_ref[...], be_ref[...])
        o_ref[r * CONV_ROWS:(r + 1) * CONV_ROWS, :] = (y * jax.nn.sigmoid(y)).astype(BF16)


def _conv_prompt(proj32, w, b, g, be, B, T):
    tm = min(256, T)
    nt = T // tm
    vec = pl.BlockSpec((1, GROUP_WIDTH), lambda bb, i: (0, 0))
    return pl.pallas_call(
        functools.partial(_conv_prompt_kernel, tm=tm),
        grid=(B, nt),
        in_specs=[pl.BlockSpec((tm, 2 * GROUP_WIDTH), lambda bb, i: (bb * nt + i, C_CONV // (2 * GROUP_WIDTH))),
                  pl.BlockSpec((CONV_WIDTH, GROUP_WIDTH), lambda bb, i: (0, 0)), vec, vec, vec],
        out_specs=[pl.BlockSpec((tm, GROUP_WIDTH), lambda bb, i: (bb * nt + i, 0)),
                   pl.BlockSpec((1, CONV_HIST, GROUP_WIDTH), lambda bb, i: (bb, 0, 0))],
        out_shape=[jax.ShapeDtypeStruct((B * T, GROUP_WIDTH), BF16),
                   jax.ShapeDtypeStruct((B, CONV_HIST, GROUP_WIDTH), F32)],
        scratch_shapes=[pltpu.VMEM((CONV_HIST + tm, GROUP_WIDTH), F32)],
        compiler_params=_cparams(("parallel", "arbitrary")),
        name="conv_prompt",
    )(proj32, w, b, g, be)


def _conv_sample_kernel(st_ref, ab_ref, w_ref, b_ref, g_ref, be_ref, o_ref, glu_ref, *, T):
    S = CONV_WIDTH - 1
    w = w_ref[...]
    glu = []
    for t in range(T):
        ab = ab_ref[t]
        gt = ab[:, :GROUP_WIDTH] * jax.nn.sigmoid(ab[:, GROUP_WIDTH:])
        glu_ref[t] = gt
        glu.append(gt)
    for t in range(T):
        acc = jnp.broadcast_to(b_ref[...], glu[0].shape)
        for k in range(CONV_WIDTH):
            j = t + k
            acc = acc + w[k:k + 1, :] * (st_ref[j] if j < S else glu[j - S])
        y = _layer_norm(acc, g_ref[...], be_ref[...])
        o_ref[t] = (y * jax.nn.sigmoid(y)).astype(BF16)


def _conv_sample(state_tm, proj32_3d, w, b, g, be):
    S, B, _ = state_tm.shape
    T = proj32_3d.shape[0]
    bb = min(32, B)
    vec = pl.BlockSpec((1, GROUP_WIDTH), lambda i: (0, 0))
    out = pl.BlockSpec((T, bb, GROUP_WIDTH), lambda i: (0, i, 0))
    return pl.pallas_call(
        functools.partial(_conv_sample_kernel, T=T),
        grid=(B // bb,),
        in_specs=[pl.BlockSpec((S, bb, GROUP_WIDTH), lambda i: (0, i, 0)),
                  pl.BlockSpec((T, bb, 2 * GROUP_WIDTH), lambda i: (0, i, C_CONV // (2 * GROUP_WIDTH))),
                  pl.BlockSpec((CONV_WIDTH, GROUP_WIDTH), lambda i: (0, 0)), vec, vec, vec],
        out_specs=[out, out],
        out_shape=[jax.ShapeDtypeStruct((T, B, GROUP_WIDTH), BF16),
                   jax.ShapeDtypeStruct((T, B, GROUP_WIDTH), F32)],
        compiler_params=_cparams(("parallel",)),
        name="conv_sample",
    )(state_tm, proj32_3d, w, b, g, be)


def _gm_prompt_kernel(uv_ref, ws_ref, bias_ref, g_ref, be_ref, o_ref, *, cs):
    vn = _layer_norm(uv_ref[:, GROUP_WIDTH:], g_ref[...], be_ref[...])
    r = lax.broadcasted_iota(jnp.int32, (cs, cs), 0)
    c = lax.broadcasted_iota(jnp.int32, (cs, cs), 1)
    for gi in range(HEADS):
        sl = slice(gi * HEAD_DIM, (gi + 1) * HEAD_DIM)
        w = jnp.where(r >= c, ws_ref[gi], 0.0).astype(BF16)
        s = _dot(w, vn[:, sl].astype(BF16)) + bias_ref[:, gi:gi + 1]
        o_ref[:, sl] = (uv_ref[:, sl] * s).astype(BF16)


def _gm_prompt(proj32, ws, bias_t, g, be, B, T):
    cs = min(GM_CHUNK, T)
    n = B * T // cs
    vec = pl.BlockSpec((1, GROUP_WIDTH), lambda i: (0, 0))
    return pl.pallas_call(
        functools.partial(_gm_prompt_kernel, cs=cs),
        grid=(n,),
        in_specs=[pl.BlockSpec((cs, 2 * GROUP_WIDTH), lambda i: (i, C_GM // (2 * GROUP_WIDTH))),
                  pl.BlockSpec((HEADS, cs, cs), lambda i: (0, 0, 0)),
                  pl.BlockSpec((cs, HEADS), lambda i: (0, 0)), vec, vec],
        out_specs=pl.BlockSpec((cs, GROUP_WIDTH), lambda i: (i, 0)),
        out_shape=jax.ShapeDtypeStruct((B * T, GROUP_WIDTH), BF16),
        compiler_params=_cparams(("parallel",)),
        name="gm_prompt",
    )(proj32, ws, bias_t, g, be)


def _gm_sample_kernel(uv_ref, wexp_ref, bexp_ref, g_ref, be_ref, o_ref, vn_ref, *, T):
    vn = []
    for t in range(T):
        v = _layer_norm(uv_ref[t][:, GROUP_WIDTH:], g_ref[...], be_ref[...])
        vn_ref[t] = v
        vn.append(v)
    for i in range(T):
        s = jnp.broadcast_to(bexp_ref[i:i + 1, :], vn[0].shape)
        for j in range(i + 1):
            s = s + wexp_ref[i, j:j + 1, :] * vn[j]
        o_ref[i] = (uv_ref[i][:, :GROUP_WIDTH] * s).astype(BF16)


def _gm_sample(proj32_3d, wexp, bexp, g, be):
    T, B, _ = proj32_3d.shape
    bb = min(32, B)
    vec = pl.BlockSpec((1, GROUP_WIDTH), lambda i: (0, 0))
    out = pl.BlockSpec((T, bb, GROUP_WIDTH), lambda i: (0, i, 0))
    return pl.pallas_call(
        functools.partial(_gm_sample_kernel, T=T),
        grid=(B // bb,),
        in_specs=[pl.BlockSpec((T, bb, 2 * GROUP_WIDTH), lambda i: (0, i, C_GM // (2 * GROUP_WIDTH))),
                  pl.BlockSpec((T, T, GROUP_WIDTH), lambda i: (0, 0, 0)),
                  pl.BlockSpec((T, GROUP_WIDTH), lambda i: (0, 0)), vec, vec],
        out_specs=[out, out],
        out_shape=[jax.ShapeDtypeStruct((T, B, GROUP_WIDTH), BF16),
                   jax.ShapeDtypeStruct((T, B, GROUP_WIDTH), F32)],
        compiler_params=_cparams(("parallel",)),
        name="gm_sample",
    )(proj32_3d, wexp, bexp, g, be)


def _outproj_kernel(a0_ref, a1_ref, a2_ref, a3_ref, w_ref, x_ref, g_ref, b_ref, h32_ref, h16_ref):
    acc = _dot(a0_ref[...], w_ref[0:GROUP_WIDTH, :])
    for n, a_ref in enumerate((a1_ref, a2_ref, a3_ref), start=1):
        acc = acc + _dot(a_ref[...], w_ref[n * GROUP_WIDTH:(n + 1) * GROUP_WIDTH, :])
    h = _layer_norm(ALPHA * x_ref[...] + acc, g_ref[...], b_ref[...])
    h32_ref[...] = h
    h16_ref[...] = h.astype(BF16)


def _outproj(mix, w16, x32, g, b, tm):
    M = x32.shape[0]
    row = lambda i: (i, 0)
    vec = pl.BlockSpec((1, D_MODEL), lambda i: (0, 0))
    return pl.pallas_call(
        _outproj_kernel,
        grid=(M // tm,),
        in_specs=[pl.BlockSpec((tm, GROUP_WIDTH), row)] * 4
        + [pl.BlockSpec((D_MODEL, D_MODEL), lambda i: (0, 0)), pl.BlockSpec((tm, D_MODEL), row), vec, vec],
        out_specs=[pl.BlockSpec((tm, D_MODEL), row)] * 2,
        out_shape=[jax.ShapeDtypeStruct((M, D_MODEL), F32), jax.ShapeDtypeStruct((M, D_MODEL), BF16)],
        compiler_params=_cparams(("parallel",)),
        name="outproj_ln1",
    )(*mix, w16, x32, g, b)


FFN_TF = 512
FFN_CARRY = SUBLANES


def _ffn_up_kernel(h_ref, wa_ref, wb_ref, cwa_ref, cwb_ref, cba_ref, cbb_ref, *refs, shift, tiles_per_seq, tm):
    if shift == 1:
        g_ref, ta_ref, tb_ref, ca_ref, cb_ref = refs
    else:
        sa_ref, sb_ref, g_ref, ta_ref, tb_ref = refs
    m = pl.program_id(0)
    f = pl.program_id(1)
    h = h_ref[...]
    ups = (_dot(h, wa_ref[...]), _dot(h, wb_ref[...]))
    outs = []
    for part, (up, cw_ref, cb_ref_) in enumerate(zip(ups, (cwa_ref, cwb_ref), (cba_ref, cbb_ref))):
        cw = cw_ref[...]
        if shift == 1:
            carry_ref = (ca_ref, cb_ref)[part]

            @pl.when(m % tiles_per_seq == 0)
            def _():
                carry_ref[f] = jnp.zeros((FFN_CARRY, FFN_TF), F32)

            prev = carry_ref[f]
            row = lax.broadcasted_iota(jnp.int32, (tm, 1), 0)
            p1 = pltpu.roll(up, 1, axis=0)
            p2 = pltpu.roll(up, 2, axis=0)
            last = prev[FFN_CARRY - 1:FFN_CARRY, :]
            last2 = prev[FFN_CARRY - 2:FFN_CARRY - 1, :]
            p1 = jnp.where(row == 0, last, p1)
            p2 = jnp.where(row == 0, last2, jnp.where(row == 1, last, p2))
            carry_ref[f] = up[tm - FFN_CARRY:, :]
            (ta_ref, tb_ref)[part][0] = up[tm - FFN_CARRY:, :]
        else:
            st = (sa_ref, sb_ref)[part][...]
            p1 = jnp.concatenate([st[shift:], up[:tm - shift]], axis=0)
            p2 = jnp.concatenate([st, up[:tm - 2 * shift]], axis=0)
            (ta_ref, tb_ref)[part][...] = up[tm - 2 * shift:, :]
        outs.append(cw[0:1, :] * p2 + cw[1:2, :] * p1 + cw[2:3, :] * up + cb_ref_[...])
    a, b = outs
    g_ref[...] = (a * jax.nn.sigmoid(a) * b).astype(BF16)


def _ffn_up(h16, wup16, cw, cb, state_tm, *, shift, seq_len):
    M = h16.shape[0]
    nf = D_FF // FFN_TF
    wa = pl.BlockSpec((D_MODEL, FFN_TF), lambda m, f: (0, f))
    wb = pl.BlockSpec((D_MODEL, FFN_TF), lambda m, f: (0, f + nf))
    cwa = pl.BlockSpec((FFN_CONV_WIDTH, FFN_TF), lambda m, f: (0, f))
    cwb = pl.BlockSpec((FFN_CONV_WIDTH, FFN_TF), lambda m, f: (0, f + nf))
    cba = pl.BlockSpec((1, FFN_TF), lambda m, f: (0, f))
    cbb = pl.BlockSpec((1, FFN_TF), lambda m, f: (0, f + nf))
    if shift == 1:
        tm = min(512, seq_len)
        nm = M // tm
        kern = functools.partial(_ffn_up_kernel, shift=1, tiles_per_seq=seq_len // tm, tm=tm)
        tail = pl.BlockSpec((1, FFN_CARRY, FFN_TF), lambda m, f: (m, 0, f))
        return pl.pallas_call(
            kern, grid=(nm, nf),
            in_specs=[pl.BlockSpec((tm, D_MODEL), lambda m, f: (m, 0)), wa, wb, cwa, cwb, cba, cbb],
            out_specs=[pl.BlockSpec((tm, FFN_TF), lambda m, f: (m, f)), tail, tail],
            out_shape=[jax.ShapeDtypeStruct((M, D_FF), BF16),
                       jax.ShapeDtypeStruct((nm, FFN_CARRY, D_FF), F32),
                       jax.ShapeDtypeStruct((nm, FFN_CARRY, D_FF), F32)],
            scratch_shapes=[pltpu.VMEM((nf, FFN_CARRY, FFN_TF), F32), pltpu.VMEM((nf, FFN_CARRY, FFN_TF), F32)],
            compiler_params=_cparams(("arbitrary", "arbitrary")),
            name="ffn_up_prompt",
        )(h16, wup16, wup16, cw, cw, cb, cb)
    tm = M
    kern = functools.partial(_ffn_up_kernel, shift=shift, tiles_per_seq=1, tm=tm)
    sa = pl.BlockSpec((2 * shift, FFN_TF), lambda m, f: (0, f))
    sb = pl.BlockSpec((2 * shift, FFN_TF), lambda m, f: (0, f + nf))
    tail = pl.BlockSpec((2 * shift, FFN_TF), lambda m, f: (0, f))
    return pl.pallas_call(
        kern, grid=(1, nf),
        in_specs=[pl.BlockSpec((tm, D_MODEL), lambda m, f: (0, 0)), wa, wb, cwa, cwb, cba, cbb, sa, sb],
        out_specs=[pl.BlockSpec((tm, FFN_TF), lambda m, f: (0, f)), tail, tail],
        out_shape=[jax.ShapeDtypeStruct((M, D_FF), BF16),
                   jax.ShapeDtypeStruct((2 * shift, D_FF), F32),
                   jax.ShapeDtypeStruct((2 * shift, D_FF), F32)],
        compiler_params=_cparams(("arbitrary", "arbitrary")),
        name="ffn_up_sample",
    )(h16, wup16, wup16, cw, cw, cb, cb, state_tm, state_tm)


def _ffn_down_kernel(g_ref, w_ref, h_ref, ln_g_ref, ln_b_ref, y32_ref, y16_ref, acc_ref):
    k = pl.program_id(1)

    @pl.when(k == 0)
    def _():
        acc_ref[...] = jnp.zeros(acc_ref.shape, F32)

    acc_ref[...] += _dot(g_ref[...], w_ref[...])

    @pl.when(k == pl.num_programs(1) - 1)
    def _():
        y = _layer_norm(ALPHA * h_ref[...] + acc_ref[...], ln_g_ref[...], ln_b_ref[...])
        y32_ref[...] = y
        y16_ref[...] = y.astype(BF16)


def _ffn_down(g16, wdown16, h32, ln_g, ln_b, tm):
    M = h32.shape[0]
    tk = FFN_TF
    vec = pl.BlockSpec((1, D_MODEL), lambda m, k: (0, 0))
    row = pl.BlockSpec((tm, D_MODEL), lambda m, k: (m, 0))
    return pl.pallas_call(
        _ffn_down_kernel,
        grid=(M // tm, D_FF // tk),
        in_specs=[pl.BlockSpec((tm, tk), lambda m, k: (m, k)),
                  pl.BlockSpec((tk, D_MODEL), lambda m, k: (k, 0)), row, vec, vec],
        out_specs=[row, row],
        out_shape=[jax.ShapeDtypeStruct((M, D_MODEL), F32), jax.ShapeDtypeStruct((M, D_MODEL), BF16)],
        scratch_shapes=[pltpu.VMEM((tm, D_MODEL), F32)],
        compiler_params=_cparams(("parallel", "arbitrary")),
        name="ffn_down_ln2",
    )(g16, wdown16, h32, ln_g, ln_b)


def _rope_tables(pos):
    half = HEAD_DIM // 2
    inv = ROPE_THETA ** (-jnp.arange(half, dtype=F32) / half)
    ang = pos.astype(F32)[:, None] * inv[None, :]
    cos, sin = jnp.cos(ang), jnp.sin(ang)
    return jnp.concatenate([cos, cos], -1), jnp.concatenate([-sin, sin], -1)


def _permute_w_in(w):
    q, kv, g, conv, gm, sb = jnp.split(w, np.cumsum([GROUP_WIDTH, 6 * HEAD_DIM, 3 * HEADS, 2 * GROUP_WIDTH,
                                                     2 * GROUP_WIDTH]).tolist(), axis=1)
    pad = jnp.zeros((w.shape[0], N_PROJ - C_G - 3 * HEADS), w.dtype)
    return jnp.concatenate([conv, gm, sb, q, kv, g, pad], axis=1).astype(BF16)


def _row2(v):
    return v.reshape(1, -1)


def _layer_prompt(x32, x16, lw, B, T):
    M = B * T
    proj32, proj16 = _proj(x16, lw["w_in"], min(1024, M))
    pos = jnp.tile(jnp.arange(T), B)
    cos2, sin2 = _rope_tables(pos)
    tm_r = min(256, T)
    pe_t = jnp.tile(lw["nsa_pe"], (1, tm_r // NSA_BLOCK, 1))
    q16, kv32, kv16, xkv = _rope(proj32, cos2, sin2, pe_t, tm_r)

    n_blk = T // NSA_BLOCK
    cmp = _compress(xkv.reshape(2, B * n_blk, NSA_BLOCK * HEAD_DIM), lw["nsa_cw1"], lw["nsa_cw2"],
                    min(128, B * n_blk))
    kcmp = cmp[0].reshape(B, n_blk, HEAD_DIM)
    vcmp = cmp[1].reshape(B, n_blk, HEAD_DIM)
    o_nsa = _nsa_prompt(q16, kv16, kcmp, vcmp, proj32, B, T)

    o_conv, conv_tail = _conv_prompt(proj32, lw["conv_w"], lw["conv_b"], lw["conv_ln_g"], lw["conv_ln_b"], B, T)
    o_gm = _gm_prompt(proj32, lw["gm_ws"], lw["gm_bias_t"], lw["gm_ln_g"], lw["gm_ln_b"], B, T)
    o_sb = _sb_prompt(proj16, B, T)

    h32, h16 = _outproj((o_nsa, o_conv, o_gm, o_sb), lw["w_out"], x32, lw["ln1_g"], lw["ln1_b"], min(512, M))
    g16, tail_a, tail_b = _ffn_up(h16, lw["ffn_up"], lw["ffn_conv_w"], lw["ffn_conv_b"], None, shift=1, seq_len=T)
    y32, y16 = _ffn_down(g16, lw["ffn_down"], h32, lw["ln2_g"], lw["ln2_b"], min(512, M))

    nsa_rows = kv32[:, :4 * HEAD_DIM].reshape(B, T, 4, HEAD_DIM)
    wlen = min(NSA_WINDOW, T)
    win_rows = kv32[:, 4 * HEAD_DIM:].reshape(B, T, 2, HEAD_DIM)[:, T - wlen:]
    sb_rows = proj32[:, C_SK:C_SK + 2 * GROUP_WIDTH].reshape(B, T, 2, HEADS, HEAD_DIM)
    conv_rows = conv_tail[:, CONV_HIST - (CONV_WIDTH - 1):]
    tiles_per_seq = tail_a.shape[0] // B
    ffn_tail = jnp.concatenate([tail_a, tail_b], -1)[tiles_per_seq - 1::tiles_per_seq]
    ffn_rows = ffn_tail[:, FFN_CARRY - (FFN_CONV_WIDTH - 1):]
    return y32, y16, (nsa_rows, sb_rows, win_rows, conv_rows, ffn_rows)


def _layer_sample(x32, x16, lw, layer, B, T, cache_nsa4, cache_sb4, win_state, conv_state, ffn_state, page_table):
    M = B * T
    n_pages = page_table.shape[1]
    P = n_pages * PAGE_SIZE
    proj32, proj16 = _proj(x16, lw["w_in"], M)
    pos = jnp.repeat(P + jnp.arange(T), B)
    cos2, sin2 = _rope_tables(pos)
    tm_r = min(256, M)
    pe_t = jnp.tile(lw["nsa_pe"], (1, tm_r // NSA_BLOCK, 1))
    q16, kv32, kv16, _ = _rope(proj32, cos2, sin2, pe_t, tm_r)

    to_b = lambda a: a.reshape(T, B, -1).swapaxes(0, 1)
    kv32_b = to_b(kv32)

    pe_page = jnp.tile(lw["nsa_pe"], (1, PAGE_SIZE // NSA_BLOCK, 1))
    xkv = _gather_cmp(cache_nsa4, layer, page_table, pe_page)
    n_past = P // NSA_BLOCK
    cmp_past = _compress(xkv.reshape(2, B * n_past, NSA_BLOCK * HEAD_DIM), lw["nsa_cw1"], lw["nsa_cw2"],
                         min(512, B * n_past)).reshape(2, B, n_past, HEAD_DIM)
    n_blk = -(-(P + T) // NSA_BLOCK)
    n_new = n_blk - n_past
    new_rows = jnp.pad(kv32_b[:, :, :2 * HEAD_DIM], ((0, 0), (0, n_new * NSA_BLOCK - T), (0, 0)))
    new_rows = new_rows.reshape(B, n_new, NSA_BLOCK, 2, HEAD_DIM)
    x_new = jnp.stack([new_rows[:, :, :, 0] + lw["nsa_pe"][0], new_rows[:, :, :, 1] + lw["nsa_pe"][1]])
    x_new = x_new.reshape(2, B * n_new, NSA_BLOCK * HEAD_DIM).astype(BF16)
    cmp_new = _compress(x_new, lw["nsa_cw1"], lw["nsa_cw2"], B * n_new).reshape(2, B, n_new, HEAD_DIM)
    n_pad = -(-n_blk // SUBLANES) * SUBLANES
    cmp = jnp.concatenate([cmp_past, cmp_new, jnp.zeros((2, B, n_pad - n_blk, HEAD_DIM), F32)], axis=2)

    g32_b = to_b(proj32[:, C_G:C_G + LANES])
    win2 = win_state.reshape(B, win_state.shape[1], 2 * HEAD_DIM)
    o_nsa = _nsa_sample(to_b(q16).astype(F32), to_b(kv16).astype(F32), win2, cmp[0], cmp[1], g32_b,
                        cache_nsa4, layer, page_table)
    o_sb = _sb_sample(to_b(proj16[:, C_SQ:C_SQ + 3 * GROUP_WIDTH]).astype(F32), cache_sb4, layer, page_table)
    to_t = lambda a: a.swapaxes(0, 1).reshape(M, -1).astype(BF16)
    o_nsa, o_sb = to_t(o_nsa), to_t(o_sb)

    proj32_3d = proj32.reshape(T, B, N_PROJ)
    o_conv, glu = _conv_sample(conv_state.swapaxes(0, 1), proj32_3d, lw["conv_w"], lw["conv_b"],
                               lw["conv_ln_g"], lw["conv_ln_b"])
    o_gm, vn = _gm_sample(proj32_3d, lw["gm_wexp"], lw["gm_bexp"], lw["gm_ln_g"], lw["gm_ln_b"])

    mix = (o_nsa, o_conv.reshape(M, GROUP_WIDTH), o_gm.reshape(M, GROUP_WIDTH), o_sb)
    h32, h16 = _outproj(mix, lw["w_out"], x32, lw["ln1_g"], lw["ln1_b"], min(512, M))
    ffn_state_tm = ffn_state.swapaxes(0, 1).reshape(2 * B, 2 * D_FF)
    g16, tail_a, tail_b = _ffn_up(h16, lw["ffn_up"], lw["ffn_conv_w"], lw["ffn_conv_b"], ffn_state_tm,
                                  shift=B, seq_len=T)
    y32, y16 = _ffn_down(g16, lw["ffn_down"], h32, lw["ln2_g"], lw["ln2_b"], min(512, M))

    nsa_rows = kv32_b[:, :, :4 * HEAD_DIM].reshape(B, T, 4, HEAD_DIM)
    win_new = kv32_b[:, :, 4 * HEAD_DIM:].reshape(B, T, 2, HEAD_DIM)
    wcat = jnp.concatenate([win_state, win_new], 1)
    win_rows = wcat[:, wcat.shape[1] - min(NSA_WINDOW, P + T):]
    sb_rows = to_b(proj32[:, C_SK:C_SK + 2 * GROUP_WIDTH]).reshape(B, T, 2, HEADS, HEAD_DIM)
    ccat = jnp.concatenate([conv_state, glu.swapaxes(0, 1)], 1)
    conv_rows = ccat[:, ccat.shape[1] - (CONV_WIDTH - 1):]
    ffn_rows = jnp.concatenate([tail_a, tail_b], -1).reshape(2, B, 2 * D_FF).swapaxes(0, 1)
    gm_v = vn.swapaxes(0, 1)
    return y32, y16, (nsa_rows, sb_rows, win_rows, conv_rows, ffn_rows, gm_v)


def _layer_weights(l, T_s, w_in, nsa_pe, nsa_cw1, nsa_cw2, conv_w, conv_b, conv_ln_g, conv_ln_b,
                   gm_ln_g, gm_ln_b, gm_ws, gm_bias, w_out, ln1_g, ln1_b,
                   ffn_up, ffn_conv_w, ffn_conv_b, ffn_down, ln2_g, ln2_b):
    cs = min(GM_CHUNK, T_s)
    ws_s = jnp.tril(gm_ws[l][:, :cs, :cs])
    wexp = jnp.repeat(ws_s.transpose(1, 2, 0), HEAD_DIM, axis=2)
    bexp = jnp.repeat(gm_bias[l][:, :cs].T, HEAD_DIM, axis=1)
    return {
        "w_in": _permute_w_in(w_in[l]),
        "nsa_pe": nsa_pe[l],
        "nsa_cw1": nsa_cw1[l].astype(BF16), "nsa_cw2": nsa_cw2[l].astype(BF16),
        "conv_w": conv_w[l], "conv_b": _row2(conv_b[l]),
        "conv_ln_g": _row2(conv_ln_g[l]), "conv_ln_b": _row2(conv_ln_b[l]),
        "gm_ln_g": _row2(gm_ln_g[l]), "gm_ln_b": _row2(gm_ln_b[l]),
        "gm_ws": gm_ws[l], "gm_bias_t": gm_bias[l].T, "gm_wexp": wexp, "gm_bexp": bexp,
        "w_out": w_out[l].astype(BF16), "ln1_g": _row2(ln1_g[l]), "ln1_b": _row2(ln1_b[l]),
        "ffn_up": ffn_up[l].astype(BF16), "ffn_conv_w": ffn_conv_w[l], "ffn_conv_b": _row2(ffn_conv_b[l]),
        "ffn_down": ffn_down[l].astype(BF16), "ln2_g": _row2(ln2_g[l]), "ln2_b": _row2(ln2_b[l]),
    }


def kernel(x_prompt, x_sample, cache_nsa_kv, cache_sb_kv, state_nsa_win, state_conv, state_ffn, page_table,
           w_in, nsa_pe, nsa_cw1, nsa_cw2, conv_w, conv_b, conv_ln_g, conv_ln_b, gm_ln_g, gm_ln_b, gm_ws, gm_bias,
           w_out, ln1_g, ln1_b, ffn_up, ffn_conv_w, ffn_conv_b, ffn_down, ln2_g, ln2_b):
    Bp, Tp, _ = x_prompt.shape
    Bs, Ts, _ = x_sample.shape
    depth = w_in.shape[0]
    n_pool = cache_nsa_kv.shape[1]
    cache_nsa4 = cache_nsa_kv.reshape(depth, n_pool, PAGE_SIZE, 4 * HEAD_DIM)
    cache_sb4 = cache_sb_kv.reshape(depth, n_pool, PAGE_SIZE, 2 * GROUP_WIDTH)

    xp32 = x_prompt.reshape(Bp * Tp, D_MODEL)
    xs32 = x_sample.swapaxes(0, 1).reshape(Ts * Bs, D_MODEL)
    xp16, xs16 = xp32.astype(BF16), xs32.astype(BF16)
    outs_p, outs_s = [], []
    for l in range(depth):
        lw = _layer_weights(l, Ts, w_in, nsa_pe, nsa_cw1, nsa_cw2, conv_w, conv_b, conv_ln_g, conv_ln_b,
                            gm_ln_g, gm_ln_b, gm_ws, gm_bias, w_out, ln1_g, ln1_b,
                            ffn_up, ffn_conv_w, ffn_conv_b, ffn_down, ln2_g, ln2_b)
        xp32, xp16, sp = _layer_prompt(xp32, xp16, lw, Bp, Tp)
        xs32, xs16, ss = _layer_sample(xs32, xs16, lw, l, Bs, Ts, cache_nsa4, cache_sb4, state_nsa_win[l],
                                       state_conv[l], state_ffn[l], page_table)
        outs_p.append(sp)
        outs_s.append(ss)
    y_p = xp32.reshape(Bp, Tp, D_MODEL)
    y_s = xs32.reshape(Ts, Bs, D_MODEL).swapaxes(0, 1)
    st = lambda outs, i: jnp.stack([o[i] for o in outs])
    return (y_p, y_s, st(outs_p, 0), st(outs_s, 0), st(outs_p, 1), st(outs_s, 1), st(outs_p, 2), st(outs_s, 2),
            st(outs_p, 3), st(outs_s, 3), st(outs_p, 4), st(outs_s, 4), st(outs_s, 5))
```

```python
import functools
import math

import jax
import jax.numpy as jnp
import numpy as np
from jax import lax
from jax.experimental import pallas as pl
from jax.experimental.pallas import tpu as pltpu

F32 = jnp.float32
BF16 = jnp.bfloat16

D_MODEL = 2048
HEAD_DIM = 128
HEADS = 4
GROUP_WIDTH = HEADS * HEAD_DIM
NSA_BLOCK = 64
NSA_TOP_N = 16
NSA_WINDOW = 512
CONV_WIDTH = 31
GM_CHUNK = 128
D_FF = (11 * D_MODEL) // 4
FFN_CONV_WIDTH = 3
ROPE_THETA = 10000.0
LN_EPS = 1e-5
DEPTH = 2
ALPHA = (2 * DEPTH) ** 0.25
PAGE_SIZE = 128
SCALE = HEAD_DIM ** -0.5

VMEM_LIMIT_V7X = 56 * 1024 * 1024
SUBLANES = 8
LANES = 128

C_CONV = 0
C_GM = C_CONV + 2 * GROUP_WIDTH
C_SQ = C_GM + 2 * GROUP_WIDTH
C_SK = C_SQ + GROUP_WIDTH
C_SV = C_SK + GROUP_WIDTH
C_Q = C_SV + GROUP_WIDTH
C_KV = C_Q + GROUP_WIDTH
C_G = C_KV + 6 * HEAD_DIM
N_PROJ = 5120
PROJ_TN = 512


def _cparams(sem):
    return pltpu.CompilerParams(dimension_semantics=sem, vmem_limit_bytes=VMEM_LIMIT_V7X)


def _layer_norm(x, g, b):
    mu = jnp.mean(x, -1, keepdims=True)
    xc = x - mu
    var = jnp.mean(xc * xc, -1, keepdims=True)
    return xc * lax.rsqrt(var + LN_EPS) * g + b


def _masked_softmax(s, mask):
    sm = jnp.where(mask, s, -1e30)
    m = jnp.max(sm, -1, keepdims=True)
    e = jnp.where(mask, jnp.exp(sm - m), 0.0)
    return e / jnp.maximum(jnp.sum(e, -1, keepdims=True), jnp.finfo(F32).tiny)


def _dot_t(a, b):
    return lax.dot_general(a, b, (((1,), (1,)), ((), ())), preferred_element_type=F32)


def _dot(a, b):
    return jnp.dot(a, b, preferred_element_type=F32)


def _split_dot(x, u16):
    hi = x.astype(BF16)
    lo = (x - hi.astype(F32)).astype(BF16)
    return _dot(hi, u16) + _dot(lo, u16)


def _proj_kernel(x_ref, w_ref, o32_ref, o16_ref):
    acc = _dot(x_ref[...], w_ref[...])
    o32_ref[...] = acc
    o16_ref[...] = acc.astype(BF16)


def _proj(x16, w16, tm):
    M, K = x16.shape
    N = w16.shape[1]
    return pl.pallas_call(
        _proj_kernel,
        grid=(M // tm, N // PROJ_TN),
        in_specs=[pl.BlockSpec((tm, K), lambda i, j: (i, 0)),
                  pl.BlockSpec((K, PROJ_TN), lambda i, j: (0, j))],
        out_specs=[pl.BlockSpec((tm, PROJ_TN), lambda i, j: (i, j)),
                   pl.BlockSpec((tm, PROJ_TN), lambda i, j: (i, j))],
        out_shape=[jax.ShapeDtypeStruct((M, N), F32), jax.ShapeDtypeStruct((M, N), BF16)],
        compiler_params=_cparams(("parallel", "parallel")),
        name="proj",
    )(x16, w16)


def _rope_kernel(q_ref, kva_ref, kvb_ref, kvc_ref, cos_ref, sin_ref, pe_ref,
                 q16_ref, kv32_ref, kv16_ref, xkv_ref):
    cos = cos_ref[...]
    sin = sin_ref[...]

    def rot(x):
        return x * cos + pltpu.roll(x, HEAD_DIM // 2, axis=1) * sin

    for h in range(HEADS):
        sl = slice(h * HEAD_DIM, (h + 1) * HEAD_DIM)
        q16_ref[:, sl] = rot(q_ref[:, sl]).astype(BF16)
    for p, ref in enumerate((kva_ref, kvb_ref, kvc_ref)):
        k = rot(ref[:, :HEAD_DIM])
        v = ref[:, HEAD_DIM:]
        ks = slice(2 * p * HEAD_DIM, (2 * p + 1) * HEAD_DIM)
        vs = slice((2 * p + 1) * HEAD_DIM, (2 * p + 2) * HEAD_DIM)
        kv32_ref[:, ks] = k
        kv32_ref[:, vs] = v
        kv16_ref[:, ks] = k.astype(BF16)
        kv16_ref[:, vs] = v.astype(BF16)
        if p == 0:
            xkv_ref[0] = (k + pe_ref[0]).astype(BF16)
            xkv_ref[1] = (v + pe_ref[1]).astype(BF16)


def _rope(proj32, cos2, sin2, pe_t, tm):
    M = proj32.shape[0]
    qb = C_Q // GROUP_WIDTH
    kb = C_KV // (2 * HEAD_DIM)
    row = lambda i: (i, 0)
    return pl.pallas_call(
        _rope_kernel,
        grid=(M // tm,),
        in_specs=[pl.BlockSpec((tm, GROUP_WIDTH), lambda i: (i, qb)),
                  pl.BlockSpec((tm, 2 * HEAD_DIM), lambda i: (i, kb)),
                  pl.BlockSpec((tm, 2 * HEAD_DIM), lambda i: (i, kb + 1)),
                  pl.BlockSpec((tm, 2 * HEAD_DIM), lambda i: (i, kb + 2)),
                  pl.BlockSpec((tm, HEAD_DIM), row),
                  pl.BlockSpec((tm, HEAD_DIM), row),
                  pl.BlockSpec((2, tm, HEAD_DIM), lambda i: (0, 0, 0))],
        out_specs=[pl.BlockSpec((tm, GROUP_WIDTH), row),
                   pl.BlockSpec((tm, 6 * HEAD_DIM), row),
                   pl.BlockSpec((tm, 6 * HEAD_DIM), row),
                   pl.BlockSpec((2, tm, HEAD_DIM), lambda i: (0, i, 0))],
        out_shape=[jax.ShapeDtypeStruct((M, GROUP_WIDTH), BF16),
                   jax.ShapeDtypeStruct((M, 6 * HEAD_DIM), F32),
                   jax.ShapeDtypeStruct((M, 6 * HEAD_DIM), BF16),
                   jax.ShapeDtypeStruct((2, M, HEAD_DIM), BF16)],
        compiler_params=_cparams(("parallel",)),
        name="rope",
    )(proj32, proj32, proj32, proj32, cos2, sin2, pe_t)


def _gelu_tanh(x):
    return x * (0.5 * (1.0 + jnp.tanh(math.sqrt(2.0 / math.pi) * (x + 0.044715 * (x * x * x)))))


def _cmp_kernel(x_ref, w1_ref, w2_ref, o_ref):
    h = _gelu_tanh(_dot(x_ref[0], w1_ref[0]))
    o_ref[0] = _dot(h.astype(BF16), w2_ref[0])


def _compress(x16, w1, w2, tm):
    _, R, K = x16.shape
    d = w1.shape[2]
    return pl.pallas_call(
        _cmp_kernel,
        grid=(2, R // tm),
        in_specs=[pl.BlockSpec((1, tm, K), lambda s, i: (s, i, 0)),
                  pl.BlockSpec((1, K, d), lambda s, i: (s, 0, 0)),
                  pl.BlockSpec((1, d, d), lambda s, i: (s, 0, 0))],
        out_specs=pl.BlockSpec((1, tm, d), lambda s, i: (s, i, 0)),
        out_shape=jax.ShapeDtypeStruct((2, R, d), F32),
        compiler_params=_cparams(("parallel", "parallel")),
        name="nsa_compress",
    )(x16, w1, w2)


def _select_blocks(imp, tpos, n_blk, n_real):
    nidx = lax.broadcasted_iota(jnp.int32, imp.shape, 1)
    cur = tpos // NSA_BLOCK
    forced = (nidx == 0) | (nidx == cur) | (nidx == cur - 1)
    started = nidx * NSA_BLOCK <= tpos
    score = jnp.where(forced, jnp.inf, jnp.where(started, imp, -jnp.inf))
    score = jnp.where(nidx < n_real, score, -jnp.inf)
    rank = jnp.zeros(imp.shape, jnp.int32)
    for n2 in range(n_real):
        c = score[:, n2:n2 + 1]
        ahead = (c > score) | ((c == score) & (nidx > n2))
        rank = rank + ahead.astype(jnp.int32)
    return (rank < min(NSA_TOP_N, n_real)) & (nidx < n_real)


def _gate_cols(gs, h):
    return [gs[:, 3 * h + c:3 * h + c + 1] for c in range(3)]


def _nsa_prompt_kernel(q_ref, ks_ref, vs_ref, kw_ref, vw_ref, kc_ref, vc_ref, g_ref, o_ref, *, T, TQ):
    i = pl.program_id(1)
    n_blk = T // NSA_BLOCK
    t0 = i * TQ
    tpos = t0 + lax.broadcasted_iota(jnp.int32, (TQ, 1), 0)
    kc = kc_ref[0].astype(BF16)
    vc = vc_ref[0].astype(BF16)
    nidx = lax.broadcasted_iota(jnp.int32, (TQ, n_blk), 1)
    cmask = (nidx + 1) * NSA_BLOCK - 1 <= tpos
    gs = jax.nn.sigmoid(g_ref[...])

    qs = [q_ref[:, h * HEAD_DIM:(h + 1) * HEAD_DIM] for h in range(HEADS)]
    imp = jnp.zeros((TQ, n_blk), F32)
    o_cmp = []
    for h in range(HEADS):
        p = _masked_softmax(_dot_t(qs[h], kc) * SCALE, cmask)
        imp = imp + p
        o_cmp.append(_dot(p.astype(BF16), vc))

    sel = _select_blocks(imp, tpos, n_blk, n_blk)
    expand = (lax.broadcasted_iota(jnp.int32, (n_blk, T), 1) // NSA_BLOCK
              == lax.broadcasted_iota(jnp.int32, (n_blk, T), 0))
    selk = _dot(jnp.where(sel, 1.0, 0.0).astype(BF16), jnp.where(expand, 1.0, 0.0).astype(BF16))
    kpos = lax.broadcasted_iota(jnp.int32, (TQ, T), 1)
    smask = (selk > 0.5) & (kpos <= tpos)

    band = min(NSA_WINDOW + TQ, T)
    w0 = pl.multiple_of(jnp.clip(t0 - NSA_WINDOW, 0, T - band), LANES)
    kw = kw_ref[pl.ds(w0, band), :]
    vw = vw_ref[pl.ds(w0, band), :]
    wpos = w0 + lax.broadcasted_iota(jnp.int32, (TQ, band), 1)
    wmask = (wpos <= tpos) & (wpos > tpos - NSA_WINDOW)

    ks = ks_ref[...]
    vs = vs_ref[...]
    for h in range(HEADS):
        p = _masked_softmax(_dot_t(qs[h], ks) * SCALE, smask)
        o_sel = _dot(p.astype(BF16), vs)
        pw = _masked_softmax(_dot_t(qs[h], kw) * SCALE, wmask)
        o_win = _dot(pw.astype(BF16), vw)
        g0, g1, g2 = _gate_cols(gs, h)
        o = g0 * o_cmp[h] + g1 * o_sel + g2 * o_win
        o_ref[:, h * HEAD_DIM:(h + 1) * HEAD_DIM] = o.astype(BF16)


def _nsa_prompt(q16, kv16, kcmp, vcmp, proj32, B, T):
    TQ = min(128, T)
    nq = T // TQ
    n_blk = T // NSA_BLOCK
    kvcol = lambda c: pl.BlockSpec((T, HEAD_DIM), lambda b, i: (b, c))
    return pl.pallas_call(
        functools.partial(_nsa_prompt_kernel, T=T, TQ=TQ),
        grid=(B, nq),
        in_specs=[pl.BlockSpec((TQ, GROUP_WIDTH), lambda b, i: (b * nq + i, 0)),
                  kvcol(2), kvcol(3), kvcol(4), kvcol(5),
                  pl.BlockSpec((1, n_blk, HEAD_DIM), lambda b, i: (b, 0, 0)),
                  pl.BlockSpec((1, n_blk, HEAD_DIM), lambda b, i: (b, 0, 0)),
                  pl.BlockSpec((TQ, LANES), lambda b, i: (b * nq + i, C_G // LANES))],
        out_specs=pl.BlockSpec((TQ, GROUP_WIDTH), lambda b, i: (b * nq + i, 0)),
        out_shape=jax.ShapeDtypeStruct((B * T, GROUP_WIDTH), BF16),
        compiler_params=_cparams(("parallel", "arbitrary")),
        name="nsa_prompt",
    )(q16, kv16, kv16, kv16, kv16, kcmp, vcmp, proj32)


def _rows(ref, r, n, stride):
    return ref[0, 0, pl.ds(r, n, stride=stride), :]


def _gather_nsa_kernel(pt_ref, pe_ref, *refs, n_pages):
    pages = refs[:n_pages]
    xkv_ref, ksv_ref = refs[n_pages:]
    pe_k = pe_ref[0]
    pe_v = pe_ref[1]
    for j in range(n_pages):
        sl = slice(j * PAGE_SIZE, (j + 1) * PAGE_SIZE)
        xkv_ref[0, 0, sl, :] = (_rows(pages[j], 0, PAGE_SIZE, 4) + pe_k).astype(BF16)
        xkv_ref[1, 0, sl, :] = (_rows(pages[j], 1, PAGE_SIZE, 4) + pe_v).astype(BF16)
        ksv_ref[0, 0, sl, :] = _rows(pages[j], 2, PAGE_SIZE, 4).astype(BF16)
        ksv_ref[1, 0, sl, :] = _rows(pages[j], 3, PAGE_SIZE, 4).astype(BF16)


def _gather_nsa(cache, layer, page_table, pe_page):
    B, n_pages = page_table.shape
    P = n_pages * PAGE_SIZE

    def page_spec(j):
        return pl.BlockSpec((1, 1, PAGE_SIZE * 4, HEAD_DIM), lambda b, pt: (layer, pt[b, j], 0, 0))

    out = pl.BlockSpec((2, 1, P, HEAD_DIM), lambda b, pt: (0, b, 0, 0))
    return pl.pallas_call(
        functools.partial(_gather_nsa_kernel, n_pages=n_pages),
        grid_spec=pltpu.PrefetchScalarGridSpec(
            num_scalar_prefetch=1, grid=(B,),
            in_specs=[pl.BlockSpec((2, PAGE_SIZE, HEAD_DIM), lambda b, pt: (0, 0, 0))]
            + [page_spec(j) for j in range(n_pages)],
            out_specs=[out, out]),
        out_shape=[jax.ShapeDtypeStruct((2, B, P, HEAD_DIM), BF16)] * 2,
        compiler_params=_cparams(("arbitrary",)),
        name="nsa_gather",
    )(page_table, pe_page, *([cache] * n_pages))


def _nsa_sample_kernel(q_ref, kvn_ref, win_ref, kc_ref, vc_ref, g_ref, ks_ref, vs_ref, o_ref, *, T, P, n_blk, n_pad):
    R = HEADS * T
    row = lax.broadcasted_iota(jnp.int32, (R, 1), 0)
    tq = row % T
    tpos = P + tq
    q_bt = q_ref[0]
    q = jnp.concatenate([q_bt[:, h * HEAD_DIM:(h + 1) * HEAD_DIM] for h in range(HEADS)], axis=0).astype(BF16)
    gs = jax.nn.sigmoid(g_ref[0])

    kc = kc_ref[0].astype(BF16)
    vc = vc_ref[0].astype(BF16)
    nidx = lax.broadcasted_iota(jnp.int32, (R, n_pad), 1)
    cmask = ((nidx + 1) * NSA_BLOCK - 1 <= tpos) & (nidx < n_blk)
    p_c = _masked_softmax(_dot_t(q, kc) * SCALE, cmask)
    o_cmp = _dot(p_c.astype(BF16), vc)
    imp = p_c[0:T]
    for h in range(1, HEADS):
        imp = imp + p_c[h * T:(h + 1) * T]
    sel_t = _select_blocks(imp, P + lax.broadcasted_iota(jnp.int32, (T, 1), 0), n_pad, n_blk)
    sel = jnp.concatenate([jnp.where(sel_t, 1.0, 0.0)] * HEADS, axis=0)

    lane = lax.broadcasted_iota(jnp.int32, (R, PAGE_SIZE), 1)
    kvn = kvn_ref[0]
    pad_rows = jnp.zeros((PAGE_SIZE - T, HEAD_DIM), F32)
    new_chunk = lambda c: jnp.concatenate([kvn[:, c * HEAD_DIM:(c + 1) * HEAD_DIM], pad_rows], axis=0).astype(BF16)
    ks_new, vs_new, kw_new, vw_new = new_chunk(2), new_chunk(3), new_chunk(4), new_chunk(5)
    expand = (lax.broadcasted_iota(jnp.int32, (n_pad, P), 1) // NSA_BLOCK
              == lax.broadcasted_iota(jnp.int32, (n_pad, P), 0))
    selk = _dot(sel.astype(BF16), jnp.where(expand, 1.0, 0.0).astype(BF16))
    m_past = (selk > 0.5) & (lax.broadcasted_iota(jnp.int32, (R, P), 1) <= tpos)
    s_past = _dot_t(q, ks_ref[0, 0]) * SCALE
    n_past = P // NSA_BLOCK
    chosen = jnp.zeros((R, PAGE_SIZE), F32)
    for u in range(PAGE_SIZE // NSA_BLOCK):
        if n_past + u < n_blk:
            in_blk = (lane >= u * NSA_BLOCK) & (lane < (u + 1) * NSA_BLOCK)
            chosen = jnp.where(in_blk, sel[:, n_past + u:n_past + u + 1], chosen)
    m_new = (chosen > 0.5) & (P + lane <= tpos)
    s_new = _dot_t(q, ks_new) * SCALE
    o_sel = _multi_chunk_attention([s_past, s_new], [m_past, m_new], [vs_ref[0, 0], vs_new])

    W = win_ref.shape[2] // 2
    kw = _rows(win_ref, 0, W, 2).astype(BF16)
    vw = _rows(win_ref, 1, W, 2).astype(BF16)
    wpos = P - W + lax.broadcasted_iota(jnp.int32, (R, W), 1)
    m_w = (wpos <= tpos) & (wpos > tpos - NSA_WINDOW) & (wpos >= 0)
    m_wn = (P + lane <= tpos) & (P + lane > tpos - NSA_WINDOW)
    o_win = _multi_chunk_attention([_dot_t(q, kw) * SCALE, _dot_t(q, kw_new) * SCALE], [m_w, m_wn], [vw, vw_new])

    for h in range(HEADS):
        g0, g1, g2 = _gate_cols(gs, h)
        rs = slice(h * T, (h + 1) * T)
        o_ref[0, :, h * HEAD_DIM:(h + 1) * HEAD_DIM] = g0 * o_cmp[rs] + g1 * o_sel[rs] + g2 * o_win[rs]


def _multi_chunk_attention(s_list, m_list, v_list):
    sm = [jnp.where(m, s, -1e30) for s, m in zip(s_list, m_list)]
    mx = jnp.max(sm[0], -1, keepdims=True)
    for s in sm[1:]:
        mx = jnp.maximum(mx, jnp.max(s, -1, keepdims=True))
    es = [jnp.where(m, jnp.exp(s - mx), 0.0) for s, m in zip(sm, m_list)]
    tot = jnp.sum(es[0], -1, keepdims=True)
    for e in es[1:]:
        tot = tot + jnp.sum(e, -1, keepdims=True)
    den = jnp.maximum(tot, jnp.finfo(F32).tiny)
    acc = _dot(es[0].astype(BF16), v_list[0])
    for e, v in zip(es[1:], v_list[1:]):
        acc = acc + _dot(e.astype(BF16), v)
    return acc / den


def _nsa_sample(q_b, kv_b, win_all, layer, kcmp, vcmp, g32_b, ksv):
    B, T, _ = q_b.shape
    P = ksv.shape[2]
    n_blk = -(-(P + T) // NSA_BLOCK)
    n_pad = kcmp.shape[1]
    W2 = win_all.shape[2]
    per_b = lambda *tail: pl.BlockSpec((1,) + tail, lambda b: (b,) + (0,) * len(tail))
    return pl.pallas_call(
        functools.partial(_nsa_sample_kernel, T=T, P=P, n_blk=n_blk, n_pad=n_pad),
        grid=(B,),
        in_specs=[per_b(T, GROUP_WIDTH), per_b(T, 6 * HEAD_DIM),
                  pl.BlockSpec((1, 1, W2, HEAD_DIM), lambda b: (layer, b, 0, 0)),
                  per_b(n_pad, HEAD_DIM), per_b(n_pad, HEAD_DIM), per_b(T, LANES),
                  pl.BlockSpec((1, 1, P, HEAD_DIM), lambda b: (0, b, 0, 0)),
                  pl.BlockSpec((1, 1, P, HEAD_DIM), lambda b: (1, b, 0, 0))],
        out_specs=per_b(T, GROUP_WIDTH),
        out_shape=jax.ShapeDtypeStruct((B, T, GROUP_WIDTH), F32),
        compiler_params=_cparams(("parallel",)),
        name="nsa_sample",
    )(q_b, kv_b, win_all, kcmp, vcmp, g32_b, ksv, ksv)


def _log_sigmoid_pair(z):
    ls_pos = jnp.minimum(z, 0.0) - jnp.log1p(jnp.exp(-jnp.abs(z)))
    return ls_pos, ls_pos - z


def _strict_upper_ones(n):
    r = lax.broadcasted_iota(jnp.int32, (n, n), 0)
    c = lax.broadcasted_iota(jnp.int32, (n, n), 1)
    return jnp.where(r > c, 1.0, 0.0).astype(BF16)


def _sb_prompt_kernel(q_ref, k_ref, v_ref, o_ref, *, TQ, KC):
    qi = pl.program_id(2)
    q = q_ref[...]
    t0 = qi * TQ
    n_chunks = (t0 + TQ) // KC
    later = _strict_upper_ones(KC)
    tpos = t0 + lax.broadcasted_iota(jnp.int32, (TQ, 1), 0)
    lane = lax.broadcasted_iota(jnp.int32, (TQ, KC), 1)

    def body(c, carry):
        acc, after_c = carry
        k0 = pl.multiple_of((n_chunks - 1 - c) * KC, KC)
        z = _dot_t(q, k_ref[pl.ds(k0, KC), :]) * SCALE
        mask = k0 + lane < tpos
        ls_pos, ls_neg = _log_sigmoid_pair(z)
        log_keep = jnp.where(mask, ls_neg, 0.0)
        after = _split_dot(log_keep, later) + after_c
        a = jnp.where(mask, jnp.exp(ls_pos + after), 0.0)
        acc = acc + _dot(a.astype(BF16), v_ref[pl.ds(k0, KC), :])
        return acc, after_c + jnp.sum(log_keep, -1, keepdims=True)

    acc, _ = lax.fori_loop(0, n_chunks, body, (jnp.zeros((TQ, HEAD_DIM), F32), jnp.zeros((TQ, 1), F32)))
    o_ref[...] = acc.astype(BF16)


def _sb_prompt(proj16, B, T):
    TQ = min(512, T)
    KC = min(256, T)
    nq = T // TQ
    col = lambda base: (lambda b, h, i: (b, base // HEAD_DIM + h))
    return pl.pallas_call(
        functools.partial(_sb_prompt_kernel, TQ=TQ, KC=KC),
        grid=(B, HEADS, nq),
        in_specs=[pl.BlockSpec((TQ, HEAD_DIM), lambda b, h, i: (b * nq + i, C_SQ // HEAD_DIM + h)),
                  pl.BlockSpec((T, HEAD_DIM), col(C_SK)),
                  pl.BlockSpec((T, HEAD_DIM), col(C_SV))],
        out_specs=pl.BlockSpec((TQ, HEAD_DIM), lambda b, h, i: (b * nq + i, h)),
        out_shape=jax.ShapeDtypeStruct((B * T, GROUP_WIDTH), BF16),
        compiler_params=_cparams(("parallel", "parallel", "arbitrary")),
        name="sb_prompt",
    )(proj16, proj16, proj16)


def _sb_sample_kernel(pt_ref, qkv_ref, *refs, T, P, n_pages):
    pages = refs[:n_pages]
    o_ref = refs[n_pages]
    R = HEADS * T
    C = PAGE_SIZE
    head = lax.broadcasted_iota(jnp.int32, (R, 1), 0) // T
    qkv = qkv_ref[0]
    q_t = qkv[:, :GROUP_WIDTH]
    col_head = lax.broadcasted_iota(jnp.int32, (R, GROUP_WIDTH), 1) // HEAD_DIM
    q_bd = jnp.where(col_head == head, jnp.concatenate([q_t] * HEADS, axis=0), 0.0).astype(BF16)
    pad_rows = jnp.zeros((C - T, GROUP_WIDTH), F32)
    k_new = jnp.concatenate([qkv[:, GROUP_WIDTH:2 * GROUP_WIDTH], pad_rows], axis=0).astype(BF16)
    v_new = jnp.concatenate([qkv[:, 2 * GROUP_WIDTH:], pad_rows], axis=0).astype(BF16)

    n_ch = n_pages + 1
    zs, vs = [], []
    for j in range(n_ch):
        if j < n_pages:
            kj = jnp.concatenate([_rows(pages[j], h, C, 2 * HEADS) for h in range(HEADS)], axis=1).astype(BF16)
            vj = jnp.concatenate([_rows(pages[j], HEADS + h, C, 2 * HEADS) for h in range(HEADS)], axis=1).astype(BF16)
        else:
            kj, vj = k_new, v_new
        zs.append(_dot_t(q_bd, kj) * SCALE)
        vs.append(vj)
    z = jnp.concatenate(zs, axis=0)
    crow = lax.broadcasted_iota(jnp.int32, (n_ch * R, 1), 0)
    kpos = (crow // R) * C + lax.broadcasted_iota(jnp.int32, (n_ch * R, C), 1)
    mask = kpos < P + (crow % R) % T
    ls_pos, ls_neg = _log_sigmoid_pair(z)
    log_keep = jnp.where(mask, ls_neg, 0.0)
    after_local = _split_dot(log_keep, _strict_upper_ones(C))
    tot = jnp.sum(log_keep, -1, keepdims=True)
    carry = jnp.zeros((R, 1), F32)
    carries = [None] * n_ch
    for j in range(n_ch - 1, -1, -1):
        carries[j] = carry
        carry = carry + tot[j * R:(j + 1) * R]
    after = after_local + jnp.concatenate(carries, axis=0)
    a = jnp.where(mask, jnp.exp(ls_pos + after), 0.0).astype(BF16)
    acc = jnp.zeros((R, GROUP_WIDTH), F32)
    for j in range(n_ch):
        acc = acc + _dot(a[j * R:(j + 1) * R], vs[j])
    for h in range(HEADS):
        o_ref[0, :, h * HEAD_DIM:(h + 1) * HEAD_DIM] = acc[h * T:(h + 1) * T, h * HEAD_DIM:(h + 1) * HEAD_DIM]


def _sb_sample(qkv_b, cache_sb, layer, page_table):
    B, T, _ = qkv_b.shape
    n_pages = page_table.shape[1]
    P = n_pages * PAGE_SIZE

    def page_spec(j):
        return pl.BlockSpec((1, 1, PAGE_SIZE * 2 * HEADS, HEAD_DIM), lambda b, pt: (layer, pt[b, j], 0, 0))

    return pl.pallas_call(
        functools.partial(_sb_sample_kernel, T=T, P=P, n_pages=n_pages),
        grid_spec=pltpu.PrefetchScalarGridSpec(
            num_scalar_prefetch=1, grid=(B,),
            in_specs=[pl.BlockSpec((1, T, 3 * GROUP_WIDTH), lambda b, pt: (b, 0, 0))]
            + [page_spec(j) for j in range(n_pages)],
            out_specs=pl.BlockSpec((1, T, GROUP_WIDTH), lambda b, pt: (b, 0, 0))),
        out_shape=jax.ShapeDtypeStruct((B, T, GROUP_WIDTH), F32),
        compiler_params=_cparams(("arbitrary",)),
        name="sb_sample",
    )(page_table, qkv_b, *([cache_sb] * n_pages))


CONV_HIST = 32
CONV_ROWS = 32


def _conv_prompt_kernel(ab_ref, w_ref, b_ref, g_ref, be_ref, o_ref, tail_ref, buf_ref, *, tm):
    i = pl.program_id(1)

    @pl.when(i == 0)
    def _():
        buf_ref[0:CONV_HIST, :] = jnp.zeros((CONV_HIST, GROUP_WIDTH), F32)

    @pl.when(i > 0)
    def _():
        buf_ref[0:CONV_HIST, :] = buf_ref[tm:tm + CONV_HIST, :]

    a = ab_ref[:, :GROUP_WIDTH]
    glu = a * jax.nn.sigmoid(ab_ref[:, GROUP_WIDTH:])
    buf_ref[CONV_HIST:, :] = glu
    tail_ref[0] = glu[tm - CONV_HIST:, :]
    w = w_ref[...]
    for r in range(tm // CONV_ROWS):
        acc = jnp.broadcast_to(b_ref[...], (CONV_ROWS, GROUP_WIDTH))
        for k in range(CONV_WIDTH):
            start = CONV_HIST + r * CONV_ROWS - (CONV_WIDTH - 1) + k
            acc = acc + w[k:k + 1, :] * buf_ref[start:start + CONV_ROWS, :]
        y = _layer_norm(acc, g_ref[...], be_ref[...])
        o_ref[r * CONV_ROWS:(r + 1) * CONV_ROWS, :] = (y * jax.nn.sigmoid(y)).astype(BF16)


def _conv_prompt(proj32, w, b, g, be, B, T):
    tm = min(256, T)
    nt = T // tm
    vec = pl.BlockSpec((1, GROUP_WIDTH), lambda bb, i: (0, 0))
    return pl.pallas_call(
        functools.partial(_conv_prompt_kernel, tm=tm),
        grid=(B, nt),
        in_specs=[pl.BlockSpec((tm, 2 * GROUP_WIDTH), lambda bb, i: (bb * nt + i, C_CONV // (2 * GROUP_WIDTH))),
                  pl.BlockSpec((CONV_WIDTH, GROUP_WIDTH), lambda bb, i: (0, 0)), vec, vec, vec],
        out_specs=[pl.BlockSpec((tm, GROUP_WIDTH), lambda bb, i: (bb * nt + i, 0)),
                   pl.BlockSpec((1, CONV_HIST, GROUP_WIDTH), lambda bb, i: (bb, 0, 0))],
        out_shape=[jax.ShapeDtypeStruct((B * T, GROUP_WIDTH), BF16),
                   jax.ShapeDtypeStruct((B, CONV_HIST, GROUP_WIDTH), F32)],
        scratch_shapes=[pltpu.VMEM((CONV_HIST + tm, GROUP_WIDTH), F32)],
        compiler_params=_cparams(("parallel", "arbitrary")),
        name="conv_prompt",
    )(proj32, w, b, g, be)


def _conv_sample_kernel(st_ref, ab_ref, w_ref, b_ref, g_ref, be_ref, o_ref, glu_ref, *, T):
    S = CONV_WIDTH - 1
    w = w_ref[...]
    glu = []
    for t in range(T):
        ab = ab_ref[t]
        gt = ab[:, :GROUP_WIDTH] * jax.nn.sigmoid(ab[:, GROUP_WIDTH:])
        glu_ref[t] = gt
        glu.append(gt)
    for t in range(T):
        acc = jnp.broadcast_to(b_ref[...], glu[0].shape)
        for k in range(CONV_WIDTH):
            j = t + k
            acc = acc + w[k:k + 1, :] * (st_ref[j] if j < S else glu[j - S])
        y = _layer_norm(acc, g_ref[...], be_ref[...])
        o_ref[t] = (y * jax.nn.sigmoid(y)).astype(BF16)


def _conv_sample(state_tm, proj32_3d, w, b, g, be):
    S, B, _ = state_tm.shape
    T = proj32_3d.shape[0]
    bb = min(32, B)
    vec = pl.BlockSpec((1, GROUP_WIDTH), lambda i: (0, 0))
    out = pl.BlockSpec((T, bb, GROUP_WIDTH), lambda i: (0, i, 0))
    return pl.pallas_call(
        functools.partial(_conv_sample_kernel, T=T),
        grid=(B // bb,),
        in_specs=[pl.BlockSpec((S, bb, GROUP_WIDTH), lambda i: (0, i, 0)),
                  pl.BlockSpec((T, bb, 2 * GROUP_WIDTH), lambda i: (0, i, C_CONV // (2 * GROUP_WIDTH))),
                  pl.BlockSpec((CONV_WIDTH, GROUP_WIDTH), lambda i: (0, 0)), vec, vec, vec],
        out_specs=[out, out],
        out_shape=[jax.ShapeDtypeStruct((T, B, GROUP_WIDTH), BF16),
                   jax.ShapeDtypeStruct((T, B, GROUP_WIDTH), F32)],
        compiler_params=_cparams(("parallel",)),
        name="conv_sample",
    )(state_tm, proj32_3d, w, b, g, be)


def _gm_prompt_kernel(uv_ref, ws_ref, bias_ref, g_ref, be_ref, o_ref, *, cs):
    vn = _layer_norm(uv_ref[:, GROUP_WIDTH:], g_ref[...], be_ref[...])
    r = lax.broadcasted_iota(jnp.int32, (cs, cs), 0)
    c = lax.broadcasted_iota(jnp.int32, (cs, cs), 1)
    for gi in range(HEADS):
        sl = slice(gi * HEAD_DIM, (gi + 1) * HEAD_DIM)
        w = jnp.where(r >= c, ws_ref[gi], 0.0).astype(BF16)
        s = _dot(w, vn[:, sl].astype(BF16)) + bias_ref[:, gi:gi + 1]
        o_ref[:, sl] = (uv_ref[:, sl] * s).astype(BF16)


def _gm_prompt(proj32, ws, bias_t, g, be, B, T):
    cs = min(GM_CHUNK, T)
    n = B * T // cs
    vec = pl.BlockSpec((1, GROUP_WIDTH), lambda i: (0, 0))
    return pl.pallas_call(
        functools.partial(_gm_prompt_kernel, cs=cs),
        grid=(n,),
        in_specs=[pl.BlockSpec((cs, 2 * GROUP_WIDTH), lambda i: (i, C_GM // (2 * GROUP_WIDTH))),
                  pl.BlockSpec((HEADS, cs, cs), lambda i: (0, 0, 0)),
                  pl.BlockSpec((cs, HEADS), lambda i: (0, 0)), vec, vec],
        out_specs=pl.BlockSpec((cs, GROUP_WIDTH), lambda i: (i, 0)),
        out_shape=jax.ShapeDtypeStruct((B * T, GROUP_WIDTH), BF16),
        compiler_params=_cparams(("parallel",)),
        name="gm_prompt",
    )(proj32, ws, bias_t, g, be)


def _gm_sample_kernel(uv_ref, wexp_ref, bexp_ref, g_ref, be_ref, o_ref, vn_ref, *, T):
    vn = []
    for t in range(T):
        v = _layer_norm(uv_ref[t][:, GROUP_WIDTH:], g_ref[...], be_ref[...])
        vn_ref[t] = v
        vn.append(v)
    for i in range(T):
        s = jnp.broadcast_to(bexp_ref[i:i + 1, :], vn[0].shape)
        for j in range(i + 1):
            s = s + wexp_ref[i, j:j + 1, :] * vn[j]
        o_ref[i] = (uv_ref[i][:, :GROUP_WIDTH] * s).astype(BF16)


def _gm_sample(proj32_3d, wexp, bexp, g, be):
    T, B, _ = proj32_3d.shape
    bb = min(32, B)
    vec = pl.BlockSpec((1, GROUP_WIDTH), lambda i: (0, 0))
    out = pl.BlockSpec((T, bb, GROUP_WIDTH), lambda i: (0, i, 0))
    return pl.pallas_call(
        functools.partial(_gm_sample_kernel, T=T),
        grid=(B // bb,),
        in_specs=[pl.BlockSpec((T, bb, 2 * GROUP_WIDTH), lambda i: (0, i, C_GM // (2 * GROUP_WIDTH))),
                  pl.BlockSpec((T, T, GROUP_WIDTH), lambda i: (0, 0, 0)),
                  pl.BlockSpec((T, GROUP_WIDTH), lambda i: (0, 0)), vec, vec],
        out_specs=[out, out],
        out_shape=[jax.ShapeDtypeStruct((T, B, GROUP_WIDTH), BF16),
                   jax.ShapeDtypeStruct((T, B, GROUP_WIDTH), F32)],
        compiler_params=_cparams(("parallel",)),
        name="gm_sample",
    )(proj32_3d, wexp, bexp, g, be)


def _outproj_kernel(a0_ref, a1_ref, a2_ref, a3_ref, w_ref, x_ref, g_ref, b_ref, h32_ref, h16_ref):
    acc = _dot(a0_ref[...], w_ref[0:GROUP_WIDTH, :])
    for n, a_ref in enumerate((a1_ref, a2_ref, a3_ref), start=1):
        acc = acc + _dot(a_ref[...], w_ref[n * GROUP_WIDTH:(n + 1) * GROUP_WIDTH, :])
    h = _layer_norm(ALPHA * x_ref[...] + acc, g_ref[...], b_ref[...])
    h32_ref[...] = h
    h16_ref[...] = h.astype(BF16)


def _outproj(mix, w16, x32, g, b, tm):
    M = x32.shape[0]
    row = lambda i: (i, 0)
    vec = pl.BlockSpec((1, D_MODEL), lambda i: (0, 0))
    return pl.pallas_call(
        _outproj_kernel,
        grid=(M // tm,),
        in_specs=[pl.BlockSpec((tm, GROUP_WIDTH), row)] * 4
        + [pl.BlockSpec((D_MODEL, D_MODEL), lambda i: (0, 0)), pl.BlockSpec((tm, D_MODEL), row), vec, vec],
        out_specs=[pl.BlockSpec((tm, D_MODEL), row)] * 2,
        out_shape=[jax.ShapeDtypeStruct((M, D_MODEL), F32), jax.ShapeDtypeStruct((M, D_MODEL), BF16)],
        compiler_params=_cparams(("parallel",)),
        name="outproj_ln1",
    )(*mix, w16, x32, g, b)


FFN_TF = 512
FFN_CARRY = SUBLANES


def _ffn_up_kernel(h_ref, wa_ref, wb_ref, cwa_ref, cwb_ref, cba_ref, cbb_ref, *refs, shift, tiles_per_seq, tm):
    if shift == 1:
        g_ref, ta_ref, tb_ref, ca_ref, cb_ref = refs
    else:
        sa_ref, sb_ref, g_ref, ta_ref, tb_ref = refs
    m = pl.program_id(0)
    f = pl.program_id(1)
    h = h_ref[...]
    ups = (_dot(h, wa_ref[...]), _dot(h, wb_ref[...]))
    outs = []
    for part, (up, cw_ref, cb_ref_) in enumerate(zip(ups, (cwa_ref, cwb_ref), (cba_ref, cbb_ref))):
        cw = cw_ref[...]
        if shift == 1:
            carry_ref = (ca_ref, cb_ref)[part]

            @pl.when(m % tiles_per_seq == 0)
            def _():
                carry_ref[f] = jnp.zeros((FFN_CARRY, FFN_TF), F32)

            prev = carry_ref[f]
            row = lax.broadcasted_iota(jnp.int32, (tm, 1), 0)
            p1 = pltpu.roll(up, 1, axis=0)
            p2 = pltpu.roll(up, 2, axis=0)
            last = prev[FFN_CARRY - 1:FFN_CARRY, :]
            last2 = prev[FFN_CARRY - 2:FFN_CARRY - 1, :]
            p1 = jnp.where(row == 0, last, p1)
            p2 = jnp.where(row == 0, last2, jnp.where(row == 1, last, p2))
            carry_ref[f] = up[tm - FFN_CARRY:, :]
            (ta_ref, tb_ref)[part][0] = up[tm - FFN_CARRY:, :]
        else:
            st = (sa_ref, sb_ref)[part][...]
            p1 = jnp.concatenate([st[shift:], up[:tm - shift]], axis=0)
            p2 = jnp.concatenate([st, up[:tm - 2 * shift]], axis=0)
            (ta_ref, tb_ref)[part][...] = up[tm - 2 * shift:, :]
        outs.append(cw[0:1, :] * p2 + cw[1:2, :] * p1 + cw[2:3, :] * up + cb_ref_[...])
    a, b = outs
    g_ref[...] = (a * jax.nn.sigmoid(a) * b).astype(BF16)


def _ffn_up(h16, wup16, cw, cb, state_tm, *, shift, seq_len):
    M = h16.shape[0]
    nf = D_FF // FFN_TF
    wa = pl.BlockSpec((D_MODEL, FFN_TF), lambda m, f: (0, f))
    wb = pl.BlockSpec((D_MODEL, FFN_TF), lambda m, f: (0, f + nf))
    cwa = pl.BlockSpec((FFN_CONV_WIDTH, FFN_TF), lambda m, f: (0, f))
    cwb = pl.BlockSpec((FFN_CONV_WIDTH, FFN_TF), lambda m, f: (0, f + nf))
    cba = pl.BlockSpec((1, FFN_TF), lambda m, f: (0, f))
    cbb = pl.BlockSpec((1, FFN_TF), lambda m, f: (0, f + nf))
    if shift == 1:
        tm = min(512, seq_len)
        nm = M // tm
        kern = functools.partial(_ffn_up_kernel, shift=1, tiles_per_seq=seq_len // tm, tm=tm)
        tail = pl.BlockSpec((1, FFN_CARRY, FFN_TF), lambda m, f: (m, 0, f))
        return pl.pallas_call(
            kern, grid=(nm, nf),
            in_specs=[pl.BlockSpec((tm, D_MODEL), lambda m, f: (m, 0)), wa, wb, cwa, cwb, cba, cbb],
            out_specs=[pl.BlockSpec((tm, FFN_TF), lambda m, f: (m, f)), tail, tail],
            out_shape=[jax.ShapeDtypeStruct((M, D_FF), BF16),
                       jax.ShapeDtypeStruct((nm, FFN_CARRY, D_FF), F32),
                       jax.ShapeDtypeStruct((nm, FFN_CARRY, D_FF), F32)],
            scratch_shapes=[pltpu.VMEM((nf, FFN_CARRY, FFN_TF), F32), pltpu.VMEM((nf, FFN_CARRY, FFN_TF), F32)],
            compiler_params=_cparams(("arbitrary", "arbitrary")),
            name="ffn_up_prompt",
        )(h16, wup16, wup16, cw, cw, cb, cb)
    tm = M
    kern = functools.partial(_ffn_up_kernel, shift=shift, tiles_per_seq=1, tm=tm)
    sa = pl.BlockSpec((2 * shift, FFN_TF), lambda m, f: (0, f))
    sb = pl.BlockSpec((2 * shift, FFN_TF), lambda m, f: (0, f + nf))
    tail = pl.BlockSpec((2 * shift, FFN_TF), lambda m, f: (0, f))
    return pl.pallas_call(
        kern, grid=(1, nf),
        in_specs=[pl.BlockSpec((tm, D_MODEL), lambda m, f: (0, 0)), wa, wb, cwa, cwb, cba, cbb, sa, sb],
        out_specs=[pl.BlockSpec((tm, FFN_TF), lambda m, f: (0, f)), tail, tail],
        out_shape=[jax.ShapeDtypeStruct((M, D_FF), BF16),
                   jax.ShapeDtypeStruct((2 * shift, D_FF), F32),
                   jax.ShapeDtypeStruct((2 * shift, D_FF), F32)],
        compiler_params=_cparams(("arbitrary", "arbitrary")),
        name="ffn_up_sample",
    )(h16, wup16, wup16, cw, cw, cb, cb, state_tm, state_tm)


def _ffn_down_kernel(g_ref, w_ref, h_ref, ln_g_ref, ln_b_ref, y32_ref, y16_ref, acc_ref):
    k = pl.program_id(1)

    @pl.when(k == 0)
    def _():
        acc_ref[...] = jnp.zeros(acc_ref.shape, F32)

    acc_ref[...] += _dot(g_ref[...], w_ref[...])

    @pl.when(k == pl.num_programs(1) - 1)
    def _():
        y = _layer_norm(ALPHA * h_ref[...] + acc_ref[...], ln_g_ref[...], ln_b_ref[...])
        y32_ref[...] = y
        y16_ref[...] = y.astype(BF16)


def _ffn_down(g16, wdown16, h32, ln_g, ln_b, tm):
    M = h32.shape[0]
    tk = FFN_TF
    vec = pl.BlockSpec((1, D_MODEL), lambda m, k: (0, 0))
    row = pl.BlockSpec((tm, D_MODEL), lambda m, k: (m, 0))
    return pl.pallas_call(
        _ffn_down_kernel,
        grid=(M // tm, D_FF // tk),
        in_specs=[pl.BlockSpec((tm, tk), lambda m, k: (m, k)),
                  pl.BlockSpec((tk, D_MODEL), lambda m, k: (k, 0)), row, vec, vec],
        out_specs=[row, row],
        out_shape=[jax.ShapeDtypeStruct((M, D_MODEL), F32), jax.ShapeDtypeStruct((M, D_MODEL), BF16)],
        scratch_shapes=[pltpu.VMEM((tm, D_MODEL), F32)],
        compiler_params=_cparams(("parallel", "arbitrary")),
        name="ffn_down_ln2",
    )(g16, wdown16, h32, ln_g, ln_b)


def _rope_tables(pos):
    half = HEAD_DIM // 2
    inv = ROPE_THETA ** (-jnp.arange(half, dtype=F32) / half)
    ang = pos.astype(F32)[:, None] * inv[None, :]
    cos, sin = jnp.cos(ang), jnp.sin(ang)
    return jnp.concatenate([cos, cos], -1), jnp.concatenate([-sin, sin], -1)


def _permute_w_in(w):
    q, kv, g, conv, gm, sb = jnp.split(w, np.cumsum([GROUP_WIDTH, 6 * HEAD_DIM, 3 * HEADS, 2 * GROUP_WIDTH,
                                                     2 * GROUP_WIDTH]).tolist(), axis=1)
    pad = jnp.zeros((w.shape[0], N_PROJ - C_G - 3 * HEADS), w.dtype)
    return jnp.concatenate([conv, gm, sb, q, kv, g, pad], axis=1).astype(BF16)


def _row2(v):
    return v.reshape(1, -1)


def _layer_prompt(x32, x16, lw, B, T):
    M = B * T
    proj32, proj16 = _proj(x16, lw["w_in"], min(1024, M))
    pos = jnp.tile(jnp.arange(T), B)
    cos2, sin2 = _rope_tables(pos)
    tm_r = min(256, T)
    pe_t = jnp.tile(lw["nsa_pe"], (1, tm_r // NSA_BLOCK, 1))
    q16, kv32, kv16, xkv = _rope(proj32, cos2, sin2, pe_t, tm_r)

    n_blk = T // NSA_BLOCK
    cmp = _compress(xkv.reshape(2, B * n_blk, NSA_BLOCK * HEAD_DIM), lw["nsa_cw1"], lw["nsa_cw2"],
                    min(128, B * n_blk))
    kcmp = cmp[0].reshape(B, n_blk, HEAD_DIM)
    vcmp = cmp[1].reshape(B, n_blk, HEAD_DIM)
    o_nsa = _nsa_prompt(q16, kv16, kcmp, vcmp, proj32, B, T)

    o_conv, conv_tail = _conv_prompt(proj32, lw["conv_w"], lw["conv_b"], lw["conv_ln_g"], lw["conv_ln_b"], B, T)
    o_gm = _gm_prompt(proj32, lw["gm_ws"], lw["gm_bias_t"], lw["gm_ln_g"], lw["gm_ln_b"], B, T)
    o_sb = _sb_prompt(proj16, B, T)

    h32, h16 = _outproj((o_nsa, o_conv, o_gm, o_sb), lw["w_out"], x32, lw["ln1_g"], lw["ln1_b"], min(512, M))
    g16, tail_a, tail_b = _ffn_up(h16, lw["ffn_up"], lw["ffn_conv_w"], lw["ffn_conv_b"], None, shift=1, seq_len=T)
    y32, y16 = _ffn_down(g16, lw["ffn_down"], h32, lw["ln2_g"], lw["ln2_b"], min(512, M))

    nsa_rows = kv32[:, :4 * HEAD_DIM].reshape(B, T, 4, HEAD_DIM)
    wlen = min(NSA_WINDOW, T)
    win_rows = kv32[:, 4 * HEAD_DIM:].reshape(B, T, 2, HEAD_DIM)[:, T - wlen:]
    sb_rows = proj32[:, C_SK:C_SK + 2 * GROUP_WIDTH].reshape(B, T, 2, HEADS, HEAD_DIM)
    conv_rows = conv_tail[:, CONV_HIST - (CONV_WIDTH - 1):]
    tiles_per_seq = tail_a.shape[0] // B
    ffn_tail = jnp.concatenate([tail_a, tail_b], -1)[tiles_per_seq - 1::tiles_per_seq]
    ffn_rows = ffn_tail[:, FFN_CARRY - (FFN_CONV_WIDTH - 1):]
    return y32, y16, (nsa_rows, sb_rows, win_rows, conv_rows, ffn_rows)


def _layer_sample(x32, x16, lw, layer, B, T, cache_nsa, cache_sb, win_all, conv_state, ffn_state, page_table):
    M = B * T
    win_state = win_all[layer].reshape(B, win_all.shape[2] // 2, 2, HEAD_DIM)
    n_pages = page_table.shape[1]
    P = n_pages * PAGE_SIZE
    proj32, proj16 = _proj(x16, lw["w_in"], M)
    pos = jnp.repeat(P + jnp.arange(T), B)
    cos2, sin2 = _rope_tables(pos)
    tm_r = min(256, M)
    pe_t = jnp.tile(lw["nsa_pe"], (1, tm_r // NSA_BLOCK, 1))
    q16, kv32, kv16, _ = _rope(proj32, cos2, sin2, pe_t, tm_r)

    to_b = lambda a: a.reshape(T, B, -1).swapaxes(0, 1)
    kv32_b = to_b(kv32)

    pe_page = jnp.tile(lw["nsa_pe"], (1, PAGE_SIZE // NSA_BLOCK, 1))
    xkv, ksv = _gather_nsa(cache_nsa, layer, page_table, pe_page)
    n_past = P // NSA_BLOCK
    cmp_past = _compress(xkv.reshape(2, B * n_past, NSA_BLOCK * HEAD_DIM), lw["nsa_cw1"], lw["nsa_cw2"],
                         min(512, B * n_past)).reshape(2, B, n_past, HEAD_DIM)
    n_blk = -(-(P + T) // NSA_BLOCK)
    n_new = n_blk - n_past
    new_rows = jnp.pad(kv32_b[:, :, :2 * HEAD_DIM], ((0, 0), (0, n_new * NSA_BLOCK - T), (0, 0)))
    new_rows = new_rows.reshape(B, n_new, NSA_BLOCK, 2, HEAD_DIM)
    x_new = jnp.stack([new_rows[:, :, :, 0] + lw["nsa_pe"][0], new_rows[:, :, :, 1] + lw["nsa_pe"][1]])
    x_new = x_new.reshape(2, B * n_new, NSA_BLOCK * HEAD_DIM).astype(BF16)
    cmp_new = _compress(x_new, lw["nsa_cw1"], lw["nsa_cw2"], B * n_new).reshape(2, B, n_new, HEAD_DIM)
    n_pad = -(-n_blk // SUBLANES) * SUBLANES
    cmp = jnp.concatenate([cmp_past, cmp_new, jnp.zeros((2, B, n_pad - n_blk, HEAD_DIM), F32)], axis=2)

    g32_b = to_b(proj32[:, C_G:C_G + LANES])
    o_nsa = _nsa_sample(to_b(q16).astype(F32), to_b(kv16).astype(F32), win_all, layer, cmp[0], cmp[1], g32_b, ksv)
    o_sb = _sb_sample(to_b(proj16[:, C_SQ:C_SQ + 3 * GROUP_WIDTH]).astype(F32), cache_sb, layer, page_table)
    to_t = lambda a: a.swapaxes(0, 1).reshape(M, -1).astype(BF16)
    o_nsa, o_sb = to_t(o_nsa), to_t(o_sb)

    proj32_3d = proj32.reshape(T, B, N_PROJ)
    o_conv, glu = _conv_sample(conv_state.swapaxes(0, 1), proj32_3d, lw["conv_w"], lw["conv_b"],
                               lw["conv_ln_g"], lw["conv_ln_b"])
    o_gm, vn = _gm_sample(proj32_3d, lw["gm_wexp"], lw["gm_bexp"], lw["gm_ln_g"], lw["gm_ln_b"])

    mix = (o_nsa, o_conv.reshape(M, GROUP_WIDTH), o_gm.reshape(M, GROUP_WIDTH), o_sb)
    h32, h16 = _outproj(mix, lw["w_out"], x32, lw["ln1_g"], lw["ln1_b"], min(512, M))
    ffn_state_tm = ffn_state.swapaxes(0, 1).reshape(2 * B, 2 * D_FF)
    g16, tail_a, tail_b = _ffn_up(h16, lw["ffn_up"], lw["ffn_conv_w"], lw["ffn_conv_b"], ffn_state_tm,
                                  shift=B, seq_len=T)
    y32, y16 = _ffn_down(g16, lw["ffn_down"], h32, lw["ln2_g"], lw["ln2_b"], min(512, M))

    nsa_rows = kv32_b[:, :, :4 * HEAD_DIM].reshape(B, T, 4, HEAD_DIM)
    win_new = kv32_b[:, :, 4 * HEAD_DIM:].reshape(B, T, 2, HEAD_DIM)
    wcat = jnp.concatenate([win_state, win_new], 1)
    win_rows = wcat[:, wcat.shape[1] - min(NSA_WINDOW, P + T):]
    sb_rows = to_b(proj32[:, C_SK:C_SK + 2 * GROUP_WIDTH]).reshape(B, T, 2, HEADS, HEAD_DIM)
    ccat = jnp.concatenate([conv_state, glu.swapaxes(0, 1)], 1)
    conv_rows = ccat[:, ccat.shape[1] - (CONV_WIDTH - 1):]
    ffn_rows = jnp.concatenate([tail_a, tail_b], -1).reshape(2, B, 2 * D_FF).swapaxes(0, 1)
    gm_v = vn.swapaxes(0, 1)
    return y32, y16, (nsa_rows, sb_rows, win_rows, conv_rows, ffn_rows, gm_v)


def _layer_weights(l, T_s, w_in, nsa_pe, nsa_cw1, nsa_cw2, conv_w, conv_b, conv_ln_g, conv_ln_b,
                   gm_ln_g, gm_ln_b, gm_ws, gm_bias, w_out, ln1_g, ln1_b,
                   ffn_up, ffn_conv_w, ffn_conv_b, ffn_down, ln2_g, ln2_b):
    cs = min(GM_CHUNK, T_s)
    ws_s = jnp.tril(gm_ws[l][:, :cs, :cs])
    wexp = jnp.repeat(ws_s.transpose(1, 2, 0), HEAD_DIM, axis=2)
    bexp = jnp.repeat(gm_bias[l][:, :cs].T, HEAD_DIM, axis=1)
    return {
        "w_in": _permute_w_in(w_in[l]),
        "nsa_pe": nsa_pe[l],
        "nsa_cw1": nsa_cw1[l].astype(BF16), "nsa_cw2": nsa_cw2[l].astype(BF16),
        "conv_w": conv_w[l], "conv_b": _row2(conv_b[l]),
        "conv_ln_g": _row2(conv_ln_g[l]), "conv_ln_b": _row2(conv_ln_b[l]),
        "gm_ln_g": _row2(gm_ln_g[l]), "gm_ln_b": _row2(gm_ln_b[l]),
        "gm_ws": gm_ws[l], "gm_bias_t": gm_bias[l].T, "gm_wexp": wexp, "gm_bexp": bexp,
        "w_out": w_out[l].astype(BF16), "ln1_g": _row2(ln1_g[l]), "ln1_b": _row2(ln1_b[l]),
        "ffn_up": ffn_up[l].astype(BF16), "ffn_conv_w": ffn_conv_w[l], "ffn_conv_b": _row2(ffn_conv_b[l]),
        "ffn_down": ffn_down[l].astype(BF16), "ln2_g": _row2(ln2_g[l]), "ln2_b": _row2(ln2_b[l]),
    }


def kernel(x_prompt, x_sample, cache_nsa_kv, cache_sb_kv, state_nsa_win, state_conv, state_ffn, page_table,
           w_in, nsa_pe, nsa_cw1, nsa_cw2, conv_w, conv_b, conv_ln_g, conv_ln_b, gm_ln_g, gm_ln_b, gm_ws, gm_bias,
           w_out, ln1_g, ln1_b, ffn_up, ffn_conv_w, ffn_conv_b, ffn_down, ln2_g, ln2_b):
    Bp, Tp, _ = x_prompt.shape
    Bs, Ts, _ = x_sample.shape
    depth = w_in.shape[0]
    cache_nsa = cache_nsa_kv.reshape(cache_nsa_kv.shape[:2] + (-1, HEAD_DIM))
    cache_sb = cache_sb_kv.reshape(cache_sb_kv.shape[:2] + (-1, HEAD_DIM))
    win_all = state_nsa_win.reshape(state_nsa_win.shape[:2] + (-1, HEAD_DIM))

    xp32 = x_prompt.reshape(Bp * Tp, D_MODEL)
    xs32 = x_sample.swapaxes(0, 1).reshape(Ts * Bs, D_MODEL)
    xp16, xs16 = xp32.astype(BF16), xs32.astype(BF16)
    outs_p, outs_s = [], []
    for l in range(depth):
        lw = _layer_weights(l, Ts, w_in, nsa_pe, nsa_cw1, nsa_cw2, conv_w, conv_b, conv_ln_g, conv_ln_b,
                            gm_ln_g, gm_ln_b, gm_ws, gm_bias, w_out, ln1_g, ln1_b,
                            ffn_up, ffn_conv_w, ffn_conv_b, ffn_down, ln2_g, ln2_b)
        xp32, xp16, sp = _layer_prompt(xp32, xp16, lw, Bp, Tp)
        xs32, xs16, ss = _layer_sample(xs32, xs16, lw, l, Bs, Ts, cache_nsa, cache_sb, win_all,
                                       state_conv[l], state_ffn[l], page_table)
        outs_p.append(sp)
        outs_s.append(ss)
    y_p = xp32.reshape(Bp, Tp, D_MODEL)
    y_s = xs32.reshape(Ts, Bs, D_MODEL).swapaxes(0, 1)
    st = lambda outs, i: jnp.stack([o[i] for o in outs])
    return (y_p, y_s, st(outs_p, 0), st(outs_s, 0), st(outs_p, 1), st(outs_s, 1), st(outs_p, 2), st(outs_s, 2),
            st(outs_p, 3), st(outs_s, 3), st(outs_p, 4), st(outs_s, 4), st(outs_s, 5))
```

```python
import functools
import math

import jax
import jax.numpy as jnp
import numpy as np
from jax import lax
from jax.experimental import pallas as pl
from jax.experimental.pallas import tpu as pltpu

F32 = jnp.float32
BF16 = jnp.bfloat16

D_MODEL = 2048
HEAD_DIM = 128
HEADS = 4
GROUP_WIDTH = HEADS * HEAD_DIM
NSA_BLOCK = 64
NSA_TOP_N = 16
NSA_WINDOW = 512
CONV_WIDTH = 31
GM_CHUNK = 128
D_FF = (11 * D_MODEL) // 4
FFN_CONV_WIDTH = 3
ROPE_THETA = 10000.0
LN_EPS = 1e-5
DEPTH = 2
ALPHA = (2 * DEPTH) ** 0.25
PAGE_SIZE = 128
SCALE = HEAD_DIM ** -0.5

VMEM_LIMIT_V7X = 56 * 1024 * 1024
SUBLANES = 8
LANES = 128

C_CONV = 0
C_GM = C_CONV + 2 * GROUP_WIDTH
C_SQ = C_GM + 2 * GROUP_WIDTH
C_SK = C_SQ + GROUP_WIDTH
C_SV = C_SK + GROUP_WIDTH
C_Q = C_SV + GROUP_WIDTH
C_KV = C_Q + GROUP_WIDTH
C_G = C_KV + 6 * HEAD_DIM
N_PROJ = 5120
PROJ_TN = 512


def _cparams(sem):
    return pltpu.CompilerParams(dimension_semantics=sem, vmem_limit_bytes=VMEM_LIMIT_V7X)


def _layer_norm(x, g, b):
    mu = jnp.mean(x, -1, keepdims=True)
    xc = x - mu
    var = jnp.mean(xc * xc, -1, keepdims=True)
    return xc * lax.rsqrt(var + LN_EPS) * g + b


def _masked_softmax(s, mask):
    sm = jnp.where(mask, s, -1e30)
    m = jnp.max(sm, -1, keepdims=True)
    e = jnp.where(mask, jnp.exp(sm - m), 0.0)
    return e / jnp.maximum(jnp.sum(e, -1, keepdims=True), jnp.finfo(F32).tiny)


def _dot_t(a, b):
    return lax.dot_general(a, b, (((1,), (1,)), ((), ())), preferred_element_type=F32)


def _dot(a, b):
    return jnp.dot(a, b, preferred_element_type=F32)


def _split_dot(x, u16):
    hi = x.astype(BF16)
    lo = (x - hi.astype(F32)).astype(BF16)
    return _dot(hi, u16) + _dot(lo, u16)


def _proj_kernel(x_ref, wt_ref, o32_ref, o16_ref):
    acc = _dot_t(x_ref[...], wt_ref[...])
    o32_ref[...] = acc
    o16_ref[...] = acc.astype(BF16)


def _proj(x16, wt16, tm):
    M, K = x16.shape
    N = wt16.shape[0]
    return pl.pallas_call(
        _proj_kernel,
        grid=(M // tm, N // PROJ_TN),
        in_specs=[pl.BlockSpec((tm, K), lambda i, j: (i, 0)),
                  pl.BlockSpec((PROJ_TN, K), lambda i, j: (j, 0))],
        out_specs=[pl.BlockSpec((tm, PROJ_TN), lambda i, j: (i, j)),
                   pl.BlockSpec((tm, PROJ_TN), lambda i, j: (i, j))],
        out_shape=[jax.ShapeDtypeStruct((M, N), F32), jax.ShapeDtypeStruct((M, N), BF16)],
        compiler_params=_cparams(("parallel", "parallel")),
        name="proj",
    )(x16, wt16)


def _rope_kernel(q_ref, kva_ref, kvb_ref, kvc_ref, cos_ref, sin_ref, pe_ref,
                 q16_ref, kv32_ref, kv16_ref, xkv_ref):
    cos = cos_ref[...]
    sin = sin_ref[...]

    def rot(x):
        return x * cos + pltpu.roll(x, HEAD_DIM // 2, axis=1) * sin

    for h in range(HEADS):
        sl = slice(h * HEAD_DIM, (h + 1) * HEAD_DIM)
        q16_ref[:, sl] = rot(q_ref[:, sl]).astype(BF16)
    for p, ref in enumerate((kva_ref, kvb_ref, kvc_ref)):
        k = rot(ref[:, :HEAD_DIM])
        v = ref[:, HEAD_DIM:]
        ks = slice(2 * p * HEAD_DIM, (2 * p + 1) * HEAD_DIM)
        vs = slice((2 * p + 1) * HEAD_DIM, (2 * p + 2) * HEAD_DIM)
        kv32_ref[:, ks] = k
        kv32_ref[:, vs] = v
        kv16_ref[:, ks] = k.astype(BF16)
        kv16_ref[:, vs] = v.astype(BF16)
        if p == 0:
            xkv_ref[0] = (k + pe_ref[0]).astype(BF16)
            xkv_ref[1] = (v + pe_ref[1]).astype(BF16)


def _rope(proj32, cos2, sin2, pe_t, tm):
    M = proj32.shape[0]
    qb = C_Q // GROUP_WIDTH
    kb = C_KV // (2 * HEAD_DIM)
    row = lambda i: (i, 0)
    return pl.pallas_call(
        _rope_kernel,
        grid=(M // tm,),
        in_specs=[pl.BlockSpec((tm, GROUP_WIDTH), lambda i: (i, qb)),
                  pl.BlockSpec((tm, 2 * HEAD_DIM), lambda i: (i, kb)),
                  pl.BlockSpec((tm, 2 * HEAD_DIM), lambda i: (i, kb + 1)),
                  pl.BlockSpec((tm, 2 * HEAD_DIM), lambda i: (i, kb + 2)),
                  pl.BlockSpec((tm, HEAD_DIM), row),
                  pl.BlockSpec((tm, HEAD_DIM), row),
                  pl.BlockSpec((2, tm, HEAD_DIM), lambda i: (0, 0, 0))],
        out_specs=[pl.BlockSpec((tm, GROUP_WIDTH), row),
                   pl.BlockSpec((tm, 6 * HEAD_DIM), row),
                   pl.BlockSpec((tm, 6 * HEAD_DIM), row),
                   pl.BlockSpec((2, tm, HEAD_DIM), lambda i: (0, i, 0))],
        out_shape=[jax.ShapeDtypeStruct((M, GROUP_WIDTH), BF16),
                   jax.ShapeDtypeStruct((M, 6 * HEAD_DIM), F32),
                   jax.ShapeDtypeStruct((M, 6 * HEAD_DIM), BF16),
                   jax.ShapeDtypeStruct((2, M, HEAD_DIM), BF16)],
        compiler_params=_cparams(("parallel",)),
        name="rope",
    )(proj32, proj32, proj32, proj32, cos2, sin2, pe_t)


def _gelu_tanh(x):
    return x * (0.5 * (1.0 + jnp.tanh(math.sqrt(2.0 / math.pi) * (x + 0.044715 * (x * x * x)))))


def _cmp_kernel(x_ref, w1_ref, w2_ref, o_ref):
    h = _gelu_tanh(_dot(x_ref[0], w1_ref[0]))
    o_ref[0] = _dot(h.astype(BF16), w2_ref[0])


def _compress(x16, w1, w2, tm):
    _, R, K = x16.shape
    d = w1.shape[2]
    return pl.pallas_call(
        _cmp_kernel,
        grid=(2, R // tm),
        in_specs=[pl.BlockSpec((1, tm, K), lambda s, i: (s, i, 0)),
                  pl.BlockSpec((1, K, d), lambda s, i: (s, 0, 0)),
                  pl.BlockSpec((1, d, d), lambda s, i: (s, 0, 0))],
        out_specs=pl.BlockSpec((1, tm, d), lambda s, i: (s, i, 0)),
        out_shape=jax.ShapeDtypeStruct((2, R, d), F32),
        compiler_params=_cparams(("parallel", "parallel")),
        name="nsa_compress",
    )(x16, w1, w2)


def _select_blocks(imp, tpos, n_blk, n_real):
    nidx = lax.broadcasted_iota(jnp.int32, imp.shape, 1)
    cur = tpos // NSA_BLOCK
    forced = (nidx == 0) | (nidx == cur) | (nidx == cur - 1)
    started = nidx * NSA_BLOCK <= tpos
    score = jnp.where(forced, jnp.inf, jnp.where(started, imp, -jnp.inf))
    score = jnp.where(nidx < n_real, score, -jnp.inf)
    rank = jnp.zeros(imp.shape, jnp.int32)
    for n2 in range(n_real):
        c = score[:, n2:n2 + 1]
        ahead = (c > score) | ((c == score) & (nidx > n2))
        rank = rank + ahead.astype(jnp.int32)
    return (rank < min(NSA_TOP_N, n_real)) & (nidx < n_real)


def _gate_cols(gs, h):
    return [gs[:, 3 * h + c:3 * h + c + 1] for c in range(3)]


MASKED = -1e30


def _nsa_prompt_kernel(q_ref, ks_ref, vs_ref, kw_ref, vw_ref, kc_ref, vc_ref, g_ref, o_ref, selk_ref, *, T, TQ, KC):
    i = pl.program_id(1)
    n_blk = T // NSA_BLOCK
    R = HEADS * TQ
    t0 = i * TQ
    tpos = t0 + lax.broadcasted_iota(jnp.int32, (TQ, 1), 0)
    tpos_r = t0 + lax.broadcasted_iota(jnp.int32, (R, 1), 0) % TQ
    per_head = lambda x: jnp.concatenate([x] * HEADS, axis=0)
    q = jnp.concatenate([q_ref[:, h * HEAD_DIM:(h + 1) * HEAD_DIM] for h in range(HEADS)], axis=0)
    gs = jax.nn.sigmoid(g_ref[...])

    kc = kc_ref[0].astype(BF16)
    vc = vc_ref[0].astype(BF16)
    cmask = (lax.broadcasted_iota(jnp.int32, (R, n_blk), 1) + 1) * NSA_BLOCK - 1 <= tpos_r
    p_c = _masked_softmax(_dot_t(q, kc) * SCALE, cmask)
    o_cmp = _dot(p_c.astype(BF16), vc)
    imp = p_c[0:TQ]
    for h in range(1, HEADS):
        imp = imp + p_c[h * TQ:(h + 1) * TQ]

    sel = _select_blocks(imp, tpos, n_blk, n_blk)
    expand = (lax.broadcasted_iota(jnp.int32, (n_blk, T), 1) // NSA_BLOCK
              == lax.broadcasted_iota(jnp.int32, (n_blk, T), 0))
    selk_ref[...] = _dot(jnp.where(sel, 1.0, 0.0).astype(BF16), jnp.where(expand, 1.0, 0.0).astype(BF16))

    lane = lax.broadcasted_iota(jnp.int32, (TQ, KC), 1)

    def body(c, carry):
        m, l, acc = carry
        k0 = pl.multiple_of(c * KC, KC)
        keep = (selk_ref[:, pl.ds(k0, KC)] > 0.5) & (k0 + lane <= tpos)
        s = _dot_t(q, ks_ref[pl.ds(k0, KC), :]) * SCALE + per_head(jnp.where(keep, 0.0, MASKED))
        m_new = jnp.maximum(m, jnp.max(s, -1, keepdims=True))
        scale_old = jnp.exp(m - m_new)
        p = jnp.exp(s - m_new)
        l = scale_old * l + jnp.sum(p, -1, keepdims=True)
        acc = scale_old * acc + _dot(p.astype(BF16), vs_ref[pl.ds(k0, KC), :])
        return m_new, l, acc

    n_chunks = (t0 + TQ + KC - 1) // KC
    _, l, acc = lax.fori_loop(0, n_chunks, body, (jnp.full((R, 1), MASKED, F32), jnp.zeros((R, 1), F32),
                                                  jnp.zeros((R, HEAD_DIM), F32)))
    o_sel = acc / l

    band = min(NSA_WINDOW + TQ, T)
    w0 = pl.multiple_of(jnp.clip(t0 - NSA_WINDOW, 0, T - band), LANES)
    wpos = w0 + lax.broadcasted_iota(jnp.int32, (TQ, band), 1)
    keep = (wpos <= tpos) & (wpos > tpos - NSA_WINDOW)
    s = _dot_t(q, kw_ref[pl.ds(w0, band), :]) * SCALE + per_head(jnp.where(keep, 0.0, MASKED))
    p = jnp.exp(s - jnp.max(s, -1, keepdims=True))
    o_win = _dot(p.astype(BF16), vw_ref[pl.ds(w0, band), :]) / jnp.sum(p, -1, keepdims=True)

    for h in range(HEADS):
        g0, g1, g2 = _gate_cols(gs, h)
        rs = slice(h * TQ, (h + 1) * TQ)
        o = g0 * o_cmp[rs] + g1 * o_sel[rs] + g2 * o_win[rs]
        o_ref[:, h * HEAD_DIM:(h + 1) * HEAD_DIM] = o.astype(BF16)


def _nsa_prompt(q16, kv16, kcmp, vcmp, proj32, B, T):
    TQ = min(128, T)
    KC = min(256, T)
    nq = T // TQ
    n_blk = T // NSA_BLOCK
    kvcol = lambda c: pl.BlockSpec((T, HEAD_DIM), lambda b, i: (b, c))
    return pl.pallas_call(
        functools.partial(_nsa_prompt_kernel, T=T, TQ=TQ, KC=KC),
        grid=(B, nq),
        scratch_shapes=[pltpu.VMEM((TQ, T), F32)],
        in_specs=[pl.BlockSpec((TQ, GROUP_WIDTH), lambda b, i: (b * nq + i, 0)),
                  kvcol(2), kvcol(3), kvcol(4), kvcol(5),
                  pl.BlockSpec((1, n_blk, HEAD_DIM), lambda b, i: (b, 0, 0)),
                  pl.BlockSpec((1, n_blk, HEAD_DIM), lambda b, i: (b, 0, 0)),
                  pl.BlockSpec((TQ, LANES), lambda b, i: (b * nq + i, C_G // LANES))],
        out_specs=pl.BlockSpec((TQ, GROUP_WIDTH), lambda b, i: (b * nq + i, 0)),
        out_shape=jax.ShapeDtypeStruct((B * T, GROUP_WIDTH), BF16),
        compiler_params=_cparams(("parallel", "arbitrary")),
        name="nsa_prompt",
    )(q16, kv16, kv16, kv16, kv16, kcmp, vcmp, proj32)


def _rows(ref, r, n, stride):
    return ref[0, 0, pl.ds(r, n, stride=stride), :]


def _gather_nsa_kernel(pt_ref, pe_ref, *refs, n_pages):
    pages = refs[:n_pages]
    xkv_ref, ksv_ref = refs[n_pages:]
    pe_k = pe_ref[0]
    pe_v = pe_ref[1]
    for j in range(n_pages):
        sl = slice(j * PAGE_SIZE, (j + 1) * PAGE_SIZE)
        xkv_ref[0, 0, sl, :] = (_rows(pages[j], 0, PAGE_SIZE, 4) + pe_k).astype(BF16)
        xkv_ref[1, 0, sl, :] = (_rows(pages[j], 1, PAGE_SIZE, 4) + pe_v).astype(BF16)
        ksv_ref[0, 0, sl, :] = _rows(pages[j], 2, PAGE_SIZE, 4).astype(BF16)
        ksv_ref[1, 0, sl, :] = _rows(pages[j], 3, PAGE_SIZE, 4).astype(BF16)


def _gather_nsa(cache, layer, page_table, pe_page):
    B, n_pages = page_table.shape
    P = n_pages * PAGE_SIZE

    def page_spec(j):
        return pl.BlockSpec((1, 1, PAGE_SIZE * 4, HEAD_DIM), lambda b, pt: (layer, pt[b, j], 0, 0))

    out = pl.BlockSpec((2, 1, P, HEAD_DIM), lambda b, pt: (0, b, 0, 0))
    return pl.pallas_call(
        functools.partial(_gather_nsa_kernel, n_pages=n_pages),
        grid_spec=pltpu.PrefetchScalarGridSpec(
            num_scalar_prefetch=1, grid=(B,),
            in_specs=[pl.BlockSpec((2, PAGE_SIZE, HEAD_DIM), lambda b, pt: (0, 0, 0))]
            + [page_spec(j) for j in range(n_pages)],
            out_specs=[out, out]),
        out_shape=[jax.ShapeDtypeStruct((2, B, P, HEAD_DIM), BF16)] * 2,
        compiler_params=_cparams(("arbitrary",)),
        name="nsa_gather",
    )(page_table, pe_page, *([cache] * n_pages))


def _nsa_sample_kernel(q_ref, kvn_ref, win_ref, kc_ref, vc_ref, g_ref, ks_ref, vs_ref, o_ref, *, T, P, n_blk, n_pad):
    R = HEADS * T
    row = lax.broadcasted_iota(jnp.int32, (R, 1), 0)
    tq = row % T
    tpos = P + tq
    q_bt = q_ref[0]
    q = jnp.concatenate([q_bt[:, h * HEAD_DIM:(h + 1) * HEAD_DIM] for h in range(HEADS)], axis=0).astype(BF16)
    gs = jax.nn.sigmoid(g_ref[0])

    kc = kc_ref[0].astype(BF16)
    vc = vc_ref[0].astype(BF16)
    nidx = lax.broadcasted_iota(jnp.int32, (R, n_pad), 1)
    cmask = ((nidx + 1) * NSA_BLOCK - 1 <= tpos) & (nidx < n_blk)
    p_c = _masked_softmax(_dot_t(q, kc) * SCALE, cmask)
    o_cmp = _dot(p_c.astype(BF16), vc)
    imp = p_c[0:T]
    for h in range(1, HEADS):
        imp = imp + p_c[h * T:(h + 1) * T]
    sel_t = _select_blocks(imp, P + lax.broadcasted_iota(jnp.int32, (T, 1), 0), n_pad, n_blk)
    sel = jnp.concatenate([jnp.where(sel_t, 1.0, 0.0)] * HEADS, axis=0)

    lane = lax.broadcasted_iota(jnp.int32, (R, PAGE_SIZE), 1)
    kvn = kvn_ref[0]
    pad_rows = jnp.zeros((PAGE_SIZE - T, HEAD_DIM), F32)
    new_chunk = lambda c: jnp.concatenate([kvn[:, c * HEAD_DIM:(c + 1) * HEAD_DIM], pad_rows], axis=0).astype(BF16)
    ks_new, vs_new, kw_new, vw_new = new_chunk(2), new_chunk(3), new_chunk(4), new_chunk(5)
    expand = (lax.broadcasted_iota(jnp.int32, (n_pad, P), 1) // NSA_BLOCK
              == lax.broadcasted_iota(jnp.int32, (n_pad, P), 0))
    selk = _dot(sel.astype(BF16), jnp.where(expand, 1.0, 0.0).astype(BF16))
    m_past = (selk > 0.5) & (lax.broadcasted_iota(jnp.int32, (R, P), 1) <= tpos)
    s_past = _dot_t(q, ks_ref[0, 0]) * SCALE
    n_past = P // NSA_BLOCK
    chosen = jnp.zeros((R, PAGE_SIZE), F32)
    for u in range(PAGE_SIZE // NSA_BLOCK):
        if n_past + u < n_blk:
            in_blk = (lane >= u * NSA_BLOCK) & (lane < (u + 1) * NSA_BLOCK)
            chosen = jnp.where(in_blk, sel[:, n_past + u:n_past + u + 1], chosen)
    m_new = (chosen > 0.5) & (P + lane <= tpos)
    s_new = _dot_t(q, ks_new) * SCALE
    o_sel = _multi_chunk_attention([s_past, s_new], [m_past, m_new], [vs_ref[0, 0], vs_new])

    W = win_ref.shape[2] // 2
    kw = _rows(win_ref, 0, W, 2).astype(BF16)
    vw = _rows(win_ref, 1, W, 2).astype(BF16)
    wpos = P - W + lax.broadcasted_iota(jnp.int32, (R, W), 1)
    m_w = (wpos <= tpos) & (wpos > tpos - NSA_WINDOW) & (wpos >= 0)
    m_wn = (P + lane <= tpos) & (P + lane > tpos - NSA_WINDOW)
    o_win = _multi_chunk_attention([_dot_t(q, kw) * SCALE, _dot_t(q, kw_new) * SCALE], [m_w, m_wn], [vw, vw_new])

    for h in range(HEADS):
        g0, g1, g2 = _gate_cols(gs, h)
        rs = slice(h * T, (h + 1) * T)
        o_ref[0, :, h * HEAD_DIM:(h + 1) * HEAD_DIM] = g0 * o_cmp[rs] + g1 * o_sel[rs] + g2 * o_win[rs]


def _multi_chunk_attention(s_list, m_list, v_list):
    sm = [jnp.where(m, s, -1e30) for s, m in zip(s_list, m_list)]
    mx = jnp.max(sm[0], -1, keepdims=True)
    for s in sm[1:]:
        mx = jnp.maximum(mx, jnp.max(s, -1, keepdims=True))
    es = [jnp.where(m, jnp.exp(s - mx), 0.0) for s, m in zip(sm, m_list)]
    tot = jnp.sum(es[0], -1, keepdims=True)
    for e in es[1:]:
        tot = tot + jnp.sum(e, -1, keepdims=True)
    den = jnp.maximum(tot, jnp.finfo(F32).tiny)
    acc = _dot(es[0].astype(BF16), v_list[0])
    for e, v in zip(es[1:], v_list[1:]):
        acc = acc + _dot(e.astype(BF16), v)
    return acc / den


def _nsa_sample(q_b, kv_b, win_all, layer, kcmp, vcmp, g32_b, ksv):
    B, T, _ = q_b.shape
    P = ksv.shape[2]
    n_blk = -(-(P + T) // NSA_BLOCK)
    n_pad = kcmp.shape[1]
    W2 = win_all.shape[2]
    per_b = lambda *tail: pl.BlockSpec((1,) + tail, lambda b: (b,) + (0,) * len(tail))
    return pl.pallas_call(
        functools.partial(_nsa_sample_kernel, T=T, P=P, n_blk=n_blk, n_pad=n_pad),
        grid=(B,),
        in_specs=[per_b(T, GROUP_WIDTH), per_b(T, 6 * HEAD_DIM),
                  pl.BlockSpec((1, 1, W2, HEAD_DIM), lambda b: (layer, b, 0, 0)),
                  per_b(n_pad, HEAD_DIM), per_b(n_pad, HEAD_DIM), per_b(T, LANES),
                  pl.BlockSpec((1, 1, P, HEAD_DIM), lambda b: (0, b, 0, 0)),
                  pl.BlockSpec((1, 1, P, HEAD_DIM), lambda b: (1, b, 0, 0))],
        out_specs=per_b(T, GROUP_WIDTH),
        out_shape=jax.ShapeDtypeStruct((B, T, GROUP_WIDTH), F32),
        compiler_params=_cparams(("parallel",)),
        name="nsa_sample",
    )(q_b, kv_b, win_all, kcmp, vcmp, g32_b, ksv, ksv)


def _log_sigmoid_pair(z):
    ls_pos = jnp.minimum(z, 0.0) - jnp.log1p(jnp.exp(-jnp.abs(z)))
    return ls_pos, ls_pos - z


def _strict_upper_ones(n):
    r = lax.broadcasted_iota(jnp.int32, (n, n), 0)
    c = lax.broadcasted_iota(jnp.int32, (n, n), 1)
    return jnp.where(r > c, 1.0, 0.0).astype(BF16)


def _sb_prompt_kernel(q_ref, k_ref, v_ref, o_ref, *, TQ, KC):
    qi = pl.program_id(2)
    q = q_ref[...]
    t0 = qi * TQ
    later = _strict_upper_ones(KC)
    tpos = t0 + lax.broadcasted_iota(jnp.int32, (TQ, 1), 0)
    lane = lax.broadcasted_iota(jnp.int32, (TQ, KC), 1)

    def chunk(k0, carry, diagonal):
        acc, after_c = carry
        z = _dot_t(q, k_ref[pl.ds(k0, KC), :]) * SCALE
        e = jnp.exp(-jnp.abs(z))
        log_keep = -(jnp.maximum(z, 0.0) + jnp.log(1.0 + e))
        r = 1.0 / (1.0 + e)
        beta = jnp.where(z >= 0.0, r, e * r)
        if diagonal:
            mask = k0 + lane < tpos
            log_keep = jnp.where(mask, log_keep, 0.0)
            beta = jnp.where(mask, beta, 0.0)
        a = beta * jnp.exp(_split_dot(log_keep, later) + after_c)
        acc = acc + _dot(a.astype(BF16), v_ref[pl.ds(k0, KC), :])
        return acc, after_c + jnp.sum(log_keep, -1, keepdims=True)

    carry = (jnp.zeros((TQ, HEAD_DIM), F32), jnp.zeros((TQ, 1), F32))
    n_diag = TQ // KC
    for d in range(n_diag):
        carry = chunk(pl.multiple_of(t0 + (n_diag - 1 - d) * KC, KC), carry, True)
    n_full = t0 // KC
    acc, _ = lax.fori_loop(0, n_full, lambda c, cr: chunk(pl.multiple_of((n_full - 1 - c) * KC, KC), cr, False), carry)
    o_ref[...] = acc.astype(BF16)


def _sb_prompt(proj16, B, T):
    TQ = min(512, T)
    KC = min(256, T)
    nq = T // TQ
    col = lambda base: (lambda b, h, i: (b, base // HEAD_DIM + h))
    return pl.pallas_call(
        functools.partial(_sb_prompt_kernel, TQ=TQ, KC=KC),
        grid=(B, HEADS, nq),
        in_specs=[pl.BlockSpec((TQ, HEAD_DIM), lambda b, h, i: (b * nq + i, C_SQ // HEAD_DIM + h)),
                  pl.BlockSpec((T, HEAD_DIM), col(C_SK)),
                  pl.BlockSpec((T, HEAD_DIM), col(C_SV))],
        out_specs=pl.BlockSpec((TQ, HEAD_DIM), lambda b, h, i: (b * nq + i, h)),
        out_shape=jax.ShapeDtypeStruct((B * T, GROUP_WIDTH), BF16),
        compiler_params=_cparams(("parallel", "parallel", "arbitrary")),
        name="sb_prompt",
    )(proj16, proj16, proj16)


def _sb_sample_kernel(pt_ref, qkv_ref, *refs, T, P, n_pages):
    pages = refs[:n_pages]
    o_ref = refs[n_pages]
    R = HEADS * T
    C = PAGE_SIZE
    head = lax.broadcasted_iota(jnp.int32, (R, 1), 0) // T
    qkv = qkv_ref[0]
    q_t = qkv[:, :GROUP_WIDTH]
    col_head = lax.broadcasted_iota(jnp.int32, (R, GROUP_WIDTH), 1) // HEAD_DIM
    q_bd = jnp.where(col_head == head, jnp.concatenate([q_t] * HEADS, axis=0), 0.0).astype(BF16)
    pad_rows = jnp.zeros((C - T, GROUP_WIDTH), F32)
    k_new = jnp.concatenate([qkv[:, GROUP_WIDTH:2 * GROUP_WIDTH], pad_rows], axis=0).astype(BF16)
    v_new = jnp.concatenate([qkv[:, 2 * GROUP_WIDTH:], pad_rows], axis=0).astype(BF16)

    n_ch = n_pages + 1
    zs, vs = [], []
    for j in range(n_ch):
        if j < n_pages:
            kj = jnp.concatenate([_rows(pages[j], h, C, 2 * HEADS) for h in range(HEADS)], axis=1).astype(BF16)
            vj = jnp.concatenate([_rows(pages[j], HEADS + h, C, 2 * HEADS) for h in range(HEADS)], axis=1).astype(BF16)
        else:
            kj, vj = k_new, v_new
        zs.append(_dot_t(q_bd, kj) * SCALE)
        vs.append(vj)
    z = jnp.concatenate(zs, axis=0)
    crow = lax.broadcasted_iota(jnp.int32, (n_ch * R, 1), 0)
    kpos = (crow // R) * C + lax.broadcasted_iota(jnp.int32, (n_ch * R, C), 1)
    mask = kpos < P + (crow % R) % T
    ls_pos, ls_neg = _log_sigmoid_pair(z)
    log_keep = jnp.where(mask, ls_neg, 0.0)
    after_local = _split_dot(log_keep, _strict_upper_ones(C))
    tot = jnp.sum(log_keep, -1, keepdims=True)
    carry = jnp.zeros((R, 1), F32)
    carries = [None] * n_ch
    for j in range(n_ch - 1, -1, -1):
        carries[j] = carry
        carry = carry + tot[j * R:(j + 1) * R]
    after = after_local + jnp.concatenate(carries, axis=0)
    a = jnp.where(mask, jnp.exp(ls_pos + after), 0.0).astype(BF16)
    acc = jnp.zeros((R, GROUP_WIDTH), F32)
    for j in range(n_ch):
        acc = acc + _dot(a[j * R:(j + 1) * R], vs[j])
    for h in range(HEADS):
        o_ref[0, :, h * HEAD_DIM:(h + 1) * HEAD_DIM] = acc[h * T:(h + 1) * T, h * HEAD_DIM:(h + 1) * HEAD_DIM]


def _sb_sample(qkv_b, cache_sb, layer, page_table):
    B, T, _ = qkv_b.shape
    n_pages = page_table.shape[1]
    P = n_pages * PAGE_SIZE

    def page_spec(j):
        return pl.BlockSpec((1, 1, PAGE_SIZE * 2 * HEADS, HEAD_DIM), lambda b, pt: (layer, pt[b, j], 0, 0))

    return pl.pallas_call(
        functools.partial(_sb_sample_kernel, T=T, P=P, n_pages=n_pages),
        grid_spec=pltpu.PrefetchScalarGridSpec(
            num_scalar_prefetch=1, grid=(B,),
            in_specs=[pl.BlockSpec((1, T, 3 * GROUP_WIDTH), lambda b, pt: (b, 0, 0))]
            + [page_spec(j) for j in range(n_pages)],
            out_specs=pl.BlockSpec((1, T, GROUP_WIDTH), lambda b, pt: (b, 0, 0))),
        out_shape=jax.ShapeDtypeStruct((B, T, GROUP_WIDTH), F32),
        compiler_params=_cparams(("arbitrary",)),
        name="sb_sample",
    )(page_table, qkv_b, *([cache_sb] * n_pages))


CONV_HIST = 32
CONV_ROWS = 32


def _conv_prompt_kernel(ab_ref, w_ref, b_ref, g_ref, be_ref, o_ref, tail_ref, buf_ref, *, tm):
    i = pl.program_id(1)

    @pl.when(i == 0)
    def _():
        buf_ref[0:CONV_HIST, :] = jnp.zeros((CONV_HIST, GROUP_WIDTH), F32)

    @pl.when(i > 0)
    def _():
        buf_ref[0:CONV_HIST, :] = buf_ref[tm:tm + CONV_HIST, :]

    a = ab_ref[:, :GROUP_WIDTH]
    glu = a * jax.nn.sigmoid(ab_ref[:, GROUP_WIDTH:])
    buf_ref[CONV_HIST:, :] = glu
    tail_ref[0] = glu[tm - CONV_HIST:, :]
    w = w_ref[...]
    for r in range(tm // CONV_ROWS):
        acc = jnp.broadcast_to(b_ref[...], (CONV_ROWS, GROUP_WIDTH))
        for k in range(CONV_WIDTH):
            start = CONV_HIST + r * CONV_ROWS - (CONV_WIDTH - 1) + k
            acc = acc + w[k:k + 1, :] * buf_ref[start:start + CONV_ROWS, :]
        y = _layer_norm(acc, g_ref[...], be_ref[...])
        o_ref[r * CONV_ROWS:(r + 1) * CONV_ROWS, :] = (y * jax.nn.sigmoid(y)).astype(BF16)


def _conv_prompt(proj32, w, b, g, be, B, T):
    tm = min(256, T)
    nt = T // tm
    vec = pl.BlockSpec((1, GROUP_WIDTH), lambda bb, i: (0, 0))
    return pl.pallas_call(
        functools.partial(_conv_prompt_kernel, tm=tm),
        grid=(B, nt),
        in_specs=[pl.BlockSpec((tm, 2 * GROUP_WIDTH), lambda bb, i: (bb * nt + i, C_CONV // (2 * GROUP_WIDTH))),
                  pl.BlockSpec((CONV_WIDTH, GROUP_WIDTH), lambda bb, i: (0, 0)), vec, vec, vec],
        out_specs=[pl.BlockSpec((tm, GROUP_WIDTH), lambda bb, i: (bb * nt + i, 0)),
                   pl.BlockSpec((1, CONV_HIST, GROUP_WIDTH), lambda bb, i: (bb, 0, 0))],
        out_shape=[jax.ShapeDtypeStruct((B * T, GROUP_WIDTH), BF16),
                   jax.ShapeDtypeStruct((B, CONV_HIST, GROUP_WIDTH), F32)],
        scratch_shapes=[pltpu.VMEM((CONV_HIST + tm, GROUP_WIDTH), F32)],
        compiler_params=_cparams(("parallel", "arbitrary")),
        name="conv_prompt",
    )(proj32, w, b, g, be)


def _conv_sample_kernel(st_ref, ab_ref, w_ref, b_ref, g_ref, be_ref, o_ref, glu_ref, *, T):
    S = CONV_WIDTH - 1
    w = w_ref[...]
    glu = []
    for t in range(T):
        ab = ab_ref[t]
        gt = ab[:, :GROUP_WIDTH] * jax.nn.sigmoid(ab[:, GROUP_WIDTH:])
        glu_ref[t] = gt
        glu.append(gt)
    for t in range(T):
        acc = jnp.broadcast_to(b_ref[...], glu[0].shape)
        for k in range(CONV_WIDTH):
            j = t + k
            acc = acc + w[k:k + 1, :] * (st_ref[j] if j < S else glu[j - S])
        y = _layer_norm(acc, g_ref[...], be_ref[...])
        o_ref[t] = (y * jax.nn.sigmoid(y)).astype(BF16)


def _conv_sample(state_tm, proj32_3d, w, b, g, be):
    S, B, _ = state_tm.shape
    T = proj32_3d.shape[0]
    bb = min(32, B)
    vec = pl.BlockSpec((1, GROUP_WIDTH), lambda i: (0, 0))
    out = pl.BlockSpec((T, bb, GROUP_WIDTH), lambda i: (0, i, 0))
    return pl.pallas_call(
        functools.partial(_conv_sample_kernel, T=T),
        grid=(B // bb,),
        in_specs=[pl.BlockSpec((S, bb, GROUP_WIDTH), lambda i: (0, i, 0)),
                  pl.BlockSpec((T, bb, 2 * GROUP_WIDTH), lambda i: (0, i, C_CONV // (2 * GROUP_WIDTH))),
                  pl.BlockSpec((CONV_WIDTH, GROUP_WIDTH), lambda i: (0, 0)), vec, vec, vec],
        out_specs=[out, out],
        out_shape=[jax.ShapeDtypeStruct((T, B, GROUP_WIDTH), BF16),
                   jax.ShapeDtypeStruct((T, B, GROUP_WIDTH), F32)],
        compiler_params=_cparams(("parallel",)),
        name="conv_sample",
    )(state_tm, proj32_3d, w, b, g, be)


def _gm_prompt_kernel(uv_ref, ws_ref, bias_ref, g_ref, be_ref, o_ref, *, cs):
    vn = _layer_norm(uv_ref[:, GROUP_WIDTH:], g_ref[...], be_ref[...])
    r = lax.broadcasted_iota(jnp.int32, (cs, cs), 0)
    c = lax.broadcasted_iota(jnp.int32, (cs, cs), 1)
    for gi in range(HEADS):
        sl = slice(gi * HEAD_DIM, (gi + 1) * HEAD_DIM)
        w = jnp.where(r >= c, ws_ref[gi], 0.0).astype(BF16)
        s = _dot(w, vn[:, sl].astype(BF16)) + bias_ref[:, gi:gi + 1]
        o_ref[:, sl] = (uv_ref[:, sl] * s).astype(BF16)


def _gm_prompt(proj32, ws, bias_t, g, be, B, T):
    cs = min(GM_CHUNK, T)
    n = B * T // cs
    vec = pl.BlockSpec((1, GROUP_WIDTH), lambda i: (0, 0))
    return pl.pallas_call(
        functools.partial(_gm_prompt_kernel, cs=cs),
        grid=(n,),
        in_specs=[pl.BlockSpec((cs, 2 * GROUP_WIDTH), lambda i: (i, C_GM // (2 * GROUP_WIDTH))),
                  pl.BlockSpec((HEADS, cs, cs), lambda i: (0, 0, 0)),
                  pl.BlockSpec((cs, HEADS), lambda i: (0, 0)), vec, vec],
        out_specs=pl.BlockSpec((cs, GROUP_WIDTH), lambda i: (i, 0)),
        out_shape=jax.ShapeDtypeStruct((B * T, GROUP_WIDTH), BF16),
        compiler_params=_cparams(("parallel",)),
        name="gm_prompt",
    )(proj32, ws, bias_t, g, be)


def _gm_sample_kernel(uv_ref, wexp_ref, bexp_ref, g_ref, be_ref, o_ref, vn_ref, *, T):
    vn = []
    for t in range(T):
        v = _layer_norm(uv_ref[t][:, GROUP_WIDTH:], g_ref[...], be_ref[...])
        vn_ref[t] = v
        vn.append(v)
    for i in range(T):
        s = jnp.broadcast_to(bexp_ref[i:i + 1, :], vn[0].shape)
        for j in range(i + 1):
            s = s + wexp_ref[i, j:j + 1, :] * vn[j]
        o_ref[i] = (uv_ref[i][:, :GROUP_WIDTH] * s).astype(BF16)


def _gm_sample(proj32_3d, wexp, bexp, g, be):
    T, B, _ = proj32_3d.shape
    bb = min(32, B)
    vec = pl.BlockSpec((1, GROUP_WIDTH), lambda i: (0, 0))
    out = pl.BlockSpec((T, bb, GROUP_WIDTH), lambda i: (0, i, 0))
    return pl.pallas_call(
        functools.partial(_gm_sample_kernel, T=T),
        grid=(B // bb,),
        in_specs=[pl.BlockSpec((T, bb, 2 * GROUP_WIDTH), lambda i: (0, i, C_GM // (2 * GROUP_WIDTH))),
                  pl.BlockSpec((T, T, GROUP_WIDTH), lambda i: (0, 0, 0)),
                  pl.BlockSpec((T, GROUP_WIDTH), lambda i: (0, 0)), vec, vec],
        out_specs=[out, out],
        out_shape=[jax.ShapeDtypeStruct((T, B, GROUP_WIDTH), BF16),
                   jax.ShapeDtypeStruct((T, B, GROUP_WIDTH), F32)],
        compiler_params=_cparams(("parallel",)),
        name="gm_sample",
    )(proj32_3d, wexp, bexp, g, be)


def _outproj_kernel(a0_ref, a1_ref, a2_ref, a3_ref, w_ref, x_ref, g_ref, b_ref, h32_ref, h16_ref):
    acc = _dot(a0_ref[...], w_ref[0:GROUP_WIDTH, :])
    for n, a_ref in enumerate((a1_ref, a2_ref, a3_ref), start=1):
        acc = acc + _dot(a_ref[...], w_ref[n * GROUP_WIDTH:(n + 1) * GROUP_WIDTH, :])
    h = _layer_norm(ALPHA * x_ref[...] + acc, g_ref[...], b_ref[...])
    h32_ref[...] = h
    h16_ref[...] = h.astype(BF16)


def _outproj(mix, w16, x32, g, b, tm):
    M = x32.shape[0]
    row = lambda i: (i, 0)
    vec = pl.BlockSpec((1, D_MODEL), lambda i: (0, 0))
    return pl.pallas_call(
        _outproj_kernel,
        grid=(M // tm,),
        in_specs=[pl.BlockSpec((tm, GROUP_WIDTH), row)] * 4
        + [pl.BlockSpec((D_MODEL, D_MODEL), lambda i: (0, 0)), pl.BlockSpec((tm, D_MODEL), row), vec, vec],
        out_specs=[pl.BlockSpec((tm, D_MODEL), row)] * 2,
        out_shape=[jax.ShapeDtypeStruct((M, D_MODEL), F32), jax.ShapeDtypeStruct((M, D_MODEL), BF16)],
        compiler_params=_cparams(("parallel",)),
        name="outproj_ln1",
    )(*mix, w16, x32, g, b)


FFN_TF = 512
FFN_CARRY = SUBLANES
FFN_TOP = 2 * SUBLANES
FFN_SUB = 256
FFN_DOWN_TK = D_FF // 4


def _ffn_up_kernel(h_ref, wa_ref, wb_ref, cwa_ref, cwb_ref, cba_ref, cbb_ref, *refs, shift, tiles_per_seq, tm):
    if shift == 1:
        g_ref, ta_ref, tb_ref, ca_ref, cb_ref = refs
    else:
        sa_ref, sb_ref, g_ref, ta_ref, tb_ref = refs
    m = pl.program_id(0)
    f = pl.program_id(1)
    h = h_ref[...]

    def conv3(cw, cb, p2, p1, x):
        return cw[0:1, :] * p2 + cw[1:2, :] * p1 + cw[2:3, :] * x + cb

    def gate(a, b):
        return (a * jax.nn.sigmoid(a) * b).astype(BF16)

    if shift == 1:
        @pl.when(m % tiles_per_seq == 0)
        def _():
            ca_ref[f] = jnp.zeros((FFN_CARRY, FFN_TF), F32)
            cb_ref[f] = jnp.zeros((FFN_CARRY, FFN_TF), F32)

    n_sub = FFN_TF // FFN_SUB
    matmuls = lambda c: [_dot(h, w_ref[:, c * FFN_SUB:(c + 1) * FFN_SUB]) for w_ref in (wa_ref, wb_ref)]
    ups_next = matmuls(0)
    for c in range(n_sub):
        cs = slice(c * FFN_SUB, (c + 1) * FFN_SUB)
        ups = ups_next
        if c + 1 < n_sub:
            ups_next = matmuls(c + 1)
        body, top = [], []
        for part in range(2):
            up = ups[part]
            cw = (cwa_ref, cwb_ref)[part][:, cs]
            cb = (cba_ref, cbb_ref)[part][:, cs]
            if shift == 1:
                carry_ref = (ca_ref, cb_ref)[part]
                prev = carry_ref[f][:, cs]
                body.append(conv3(cw, cb, pltpu.roll(up, 2, axis=0), pltpu.roll(up, 1, axis=0), up))
                hist = jnp.concatenate([prev, up[0:FFN_TOP]], axis=0)
                top.append(conv3(cw, cb, hist[FFN_CARRY - 2:FFN_CARRY - 2 + FFN_TOP],
                                 hist[FFN_CARRY - 1:FFN_CARRY - 1 + FFN_TOP], up[0:FFN_TOP]))
                tail_rows = up[tm - FFN_CARRY:, :]
                carry_ref[f, :, cs] = tail_rows
                (ta_ref, tb_ref)[part][0, :, cs] = tail_rows
            else:
                st = (sa_ref, sb_ref)[part][:, cs]
                p1 = jnp.concatenate([st[shift:], up[:tm - shift]], axis=0)
                p2 = jnp.concatenate([st, up[:tm - 2 * shift]], axis=0)
                (ta_ref, tb_ref)[part][:, cs] = up[tm - 2 * shift:, :]
                body.append(conv3(cw, cb, p2, p1, up))
        g_ref[:, cs] = gate(body[0], body[1])
        if shift == 1:
            g_ref[0:FFN_TOP, cs] = gate(top[0], top[1])


def _ffn_up(h16, wup16, cw, cb, state_tm, *, shift, seq_len):
    M = h16.shape[0]
    nf = D_FF // FFN_TF
    wa = pl.BlockSpec((D_MODEL, FFN_TF), lambda m, f: (0, f))
    wb = pl.BlockSpec((D_MODEL, FFN_TF), lambda m, f: (0, f + nf))
    cwa = pl.BlockSpec((FFN_CONV_WIDTH, FFN_TF), lambda m, f: (0, f))
    cwb = pl.BlockSpec((FFN_CONV_WIDTH, FFN_TF), lambda m, f: (0, f + nf))
    cba = pl.BlockSpec((1, FFN_TF), lambda m, f: (0, f))
    cbb = pl.BlockSpec((1, FFN_TF), lambda m, f: (0, f + nf))
    if shift == 1:
        tm = min(512, seq_len)
        nm = M // tm
        kern = functools.partial(_ffn_up_kernel, shift=1, tiles_per_seq=seq_len // tm, tm=tm)
        tail = pl.BlockSpec((1, FFN_CARRY, FFN_TF), lambda m, f: (m, 0, f))
        return pl.pallas_call(
            kern, grid=(nm, nf),
            in_specs=[pl.BlockSpec((tm, D_MODEL), lambda m, f: (m, 0)), wa, wb, cwa, cwb, cba, cbb],
            out_specs=[pl.BlockSpec((tm, FFN_TF), lambda m, f: (m, f)), tail, tail],
            out_shape=[jax.ShapeDtypeStruct((M, D_FF), BF16),
                       jax.ShapeDtypeStruct((nm, FFN_CARRY, D_FF), F32),
                       jax.ShapeDtypeStruct((nm, FFN_CARRY, D_FF), F32)],
            scratch_shapes=[pltpu.VMEM((nf, FFN_CARRY, FFN_TF), F32), pltpu.VMEM((nf, FFN_CARRY, FFN_TF), F32)],
            compiler_params=_cparams(("arbitrary", "arbitrary")),
            name="ffn_up_prompt",
        )(h16, wup16, wup16, cw, cw, cb, cb)
    tm = M
    kern = functools.partial(_ffn_up_kernel, shift=shift, tiles_per_seq=1, tm=tm)
    sa = pl.BlockSpec((2 * shift, FFN_TF), lambda m, f: (0, f))
    sb = pl.BlockSpec((2 * shift, FFN_TF), lambda m, f: (0, f + nf))
    tail = pl.BlockSpec((2 * shift, FFN_TF), lambda m, f: (0, f))
    return pl.pallas_call(
        kern, grid=(1, nf),
        in_specs=[pl.BlockSpec((tm, D_MODEL), lambda m, f: (0, 0)), wa, wb, cwa, cwb, cba, cbb, sa, sb],
        out_specs=[pl.BlockSpec((tm, FFN_TF), lambda m, f: (0, f)), tail, tail],
        out_shape=[jax.ShapeDtypeStruct((M, D_FF), BF16),
                   jax.ShapeDtypeStruct((2 * shift, D_FF), F32),
                   jax.ShapeDtypeStruct((2 * shift, D_FF), F32)],
        compiler_params=_cparams(("arbitrary", "arbitrary")),
        name="ffn_up_sample",
    )(h16, wup16, wup16, cw, cw, cb, cb, state_tm, state_tm)


def _ffn_down_kernel(g_ref, w_ref, h_ref, ln_g_ref, ln_b_ref, y32_ref, y16_ref, acc_ref):
    k = pl.program_id(1)
    last = pl.num_programs(1) - 1
    part = _dot(g_ref[...], w_ref[...])

    @pl.when(k == 0)
    def _():
        acc_ref[...] = part

    @pl.when((k > 0) & (k < last))
    def _():
        acc_ref[...] += part

    @pl.when(k == last)
    def _():
        y = _layer_norm(ALPHA * h_ref[...] + (acc_ref[...] + part), ln_g_ref[...], ln_b_ref[...])
        y32_ref[...] = y
        y16_ref[...] = y.astype(BF16)


def _ffn_down(g16, wdown16, h32, ln_g, ln_b, tm):
    M = h32.shape[0]
    tk = FFN_DOWN_TK
    vec = pl.BlockSpec((1, D_MODEL), lambda m, k: (0, 0))
    row = pl.BlockSpec((tm, D_MODEL), lambda m, k: (m, 0))
    return pl.pallas_call(
        _ffn_down_kernel,
        grid=(M // tm, D_FF // tk),
        in_specs=[pl.BlockSpec((tm, tk), lambda m, k: (m, k)),
                  pl.BlockSpec((tk, D_MODEL), lambda m, k: (k, 0)), row, vec, vec],
        out_specs=[row, row],
        out_shape=[jax.ShapeDtypeStruct((M, D_MODEL), F32), jax.ShapeDtypeStruct((M, D_MODEL), BF16)],
        scratch_shapes=[pltpu.VMEM((tm, D_MODEL), F32)],
        compiler_params=_cparams(("parallel", "arbitrary")),
        name="ffn_down_ln2",
    )(g16, wdown16, h32, ln_g, ln_b)


def _rope_tables(pos):
    half = HEAD_DIM // 2
    inv = ROPE_THETA ** (-jnp.arange(half, dtype=F32) / half)
    ang = pos.astype(F32)[:, None] * inv[None, :]
    cos, sin = jnp.cos(ang), jnp.sin(ang)
    return jnp.concatenate([cos, cos], -1), jnp.concatenate([-sin, sin], -1)


N_QKV = GROUP_WIDTH + 6 * HEAD_DIM
N_GATE = 3 * HEADS
N_REST = 2 * GROUP_WIDTH + 2 * GROUP_WIDTH + 3 * GROUP_WIDTH


W_IN_TILE = 256


def _w_in_rows_kernel(w_ref, o_ref, *, layer, depth):
    i = pl.program_id(0)
    n_kt = w_ref.shape[1] // depth
    row = lax.broadcasted_iota(jnp.int32, (W_IN_TILE, LANES), 0)
    keep = (i < pl.num_programs(0) - 1) | (row < N_GATE)
    for kt in range(n_kt):
        x = w_ref[:, kt * depth + layer, :]
        o_ref[:, kt * LANES:(kt + 1) * LANES] = jnp.where(keep, x, 0.0).astype(BF16)


def _w_in_rows(w_in, layer):
    depth, K, N = w_in.shape
    n_kt = K // LANES
    view = w_in.reshape(depth, n_kt, LANES, N).transpose(3, 1, 0, 2).reshape(N, n_kt * depth, LANES)
    n_rest, n_qkv = N_REST // W_IN_TILE, N_QKV // W_IN_TILE

    def src_row(i):
        return jnp.where(i < n_rest, N_QKV + N_GATE + i * W_IN_TILE,
                         jnp.where(i < n_rest + n_qkv, (i - n_rest) * W_IN_TILE, N_QKV))

    return pl.pallas_call(
        functools.partial(_w_in_rows_kernel, layer=layer, depth=depth),
        grid=(N_PROJ // W_IN_TILE,),
        in_specs=[pl.BlockSpec((pl.Element(W_IN_TILE), pl.Element(n_kt * depth), pl.Element(LANES)),
                               lambda i: (src_row(i), 0, 0))],
        out_specs=pl.BlockSpec((W_IN_TILE, K), lambda i: (i, 0)),
        out_shape=jax.ShapeDtypeStruct((N_PROJ, K), BF16),
        compiler_params=_cparams(("parallel",)),
        name="w_in_rows",
    )(view)


def _row2(v):
    return v.reshape(1, -1)


def _layer_prompt(x32, x16, lw, B, T):
    M = B * T
    proj32, proj16 = _proj(x16, lw["w_in"], min(1024, M))
    pos = jnp.tile(jnp.arange(T), B)
    cos2, sin2 = _rope_tables(pos)
    tm_r = min(256, T)
    pe_t = jnp.tile(lw["nsa_pe"], (1, tm_r // NSA_BLOCK, 1))
    q16, kv32, kv16, xkv = _rope(proj32, cos2, sin2, pe_t, tm_r)

    n_blk = T // NSA_BLOCK
    cmp = _compress(xkv.reshape(2, B * n_blk, NSA_BLOCK * HEAD_DIM), lw["nsa_cw1"], lw["nsa_cw2"],
                    min(128, B * n_blk))
    kcmp = cmp[0].reshape(B, n_blk, HEAD_DIM)
    vcmp = cmp[1].reshape(B, n_blk, HEAD_DIM)
    o_nsa = _nsa_prompt(q16, kv16, kcmp, vcmp, proj32, B, T)

    o_conv, conv_tail = _conv_prompt(proj32, lw["conv_w"], lw["conv_b"], lw["conv_ln_g"], lw["conv_ln_b"], B, T)
    o_gm = _gm_prompt(proj32, lw["gm_ws"], lw["gm_bias_t"], lw["gm_ln_g"], lw["gm_ln_b"], B, T)
    o_sb = _sb_prompt(proj16, B, T)

    h32, h16 = _outproj((o_nsa, o_conv, o_gm, o_sb), lw["w_out"], x32, lw["ln1_g"], lw["ln1_b"], min(512, M))
    g16, tail_a, tail_b = _ffn_up(h16, lw["ffn_up"], lw["ffn_conv_w"], lw["ffn_conv_b"], None, shift=1, seq_len=T)
    y32, y16 = _ffn_down(g16, lw["ffn_down"], h32, lw["ln2_g"], lw["ln2_b"], min(512, M))

    nsa_rows = kv32[:, :4 * HEAD_DIM].reshape(B, T, 4, HEAD_DIM)
    wlen = min(NSA_WINDOW, T)
    win_rows = kv32[:, 4 * HEAD_DIM:].reshape(B, T, 2, HEAD_DIM)[:, T - wlen:]
    sb_rows = proj32[:, C_SK:C_SK + 2 * GROUP_WIDTH].reshape(B, T, 2, HEADS, HEAD_DIM)
    conv_rows = conv_tail[:, CONV_HIST - (CONV_WIDTH - 1):]
    tiles_per_seq = tail_a.shape[0] // B
    ffn_tail = jnp.concatenate([tail_a, tail_b], -1)[tiles_per_seq - 1::tiles_per_seq]
    ffn_rows = ffn_tail[:, FFN_CARRY - (FFN_CONV_WIDTH - 1):]
    return y32, y16, (nsa_rows, sb_rows, win_rows, conv_rows, ffn_rows)


def _layer_sample(x32, x16, lw, layer, B, T, cache_nsa, cache_sb, win_all, conv_state, ffn_state, page_table):
    M = B * T
    win_state = win_all[layer].reshape(B, win_all.shape[2] // 2, 2, HEAD_DIM)
    n_pages = page_table.shape[1]
    P = n_pages * PAGE_SIZE
    proj32, proj16 = _proj(x16, lw["w_in"], M)
    pos = jnp.repeat(P + jnp.arange(T), B)
    cos2, sin2 = _rope_tables(pos)
    tm_r = min(256, M)
    pe_t = jnp.tile(lw["nsa_pe"], (1, tm_r // NSA_BLOCK, 1))
    q16, kv32, kv16, _ = _rope(proj32, cos2, sin2, pe_t, tm_r)

    to_b = lambda a: a.reshape(T, B, -1).swapaxes(0, 1)
    kv32_b = to_b(kv32)

    pe_page = jnp.tile(lw["nsa_pe"], (1, PAGE_SIZE // NSA_BLOCK, 1))
    xkv, ksv = _gather_nsa(cache_nsa, layer, page_table, pe_page)
    n_past = P // NSA_BLOCK
    cmp_past = _compress(xkv.reshape(2, B * n_past, NSA_BLOCK * HEAD_DIM), lw["nsa_cw1"], lw["nsa_cw2"],
                         min(512, B * n_past)).reshape(2, B, n_past, HEAD_DIM)
    n_blk = -(-(P + T) // NSA_BLOCK)
    n_new = n_blk - n_past
    new_rows = jnp.pad(kv32_b[:, :, :2 * HEAD_DIM], ((0, 0), (0, n_new * NSA_BLOCK - T), (0, 0)))
    new_rows = new_rows.reshape(B, n_new, NSA_BLOCK, 2, HEAD_DIM)
    x_new = jnp.stack([new_rows[:, :, :, 0] + lw["nsa_pe"][0], new_rows[:, :, :, 1] + lw["nsa_pe"][1]])
    x_new = x_new.reshape(2, B * n_new, NSA_BLOCK * HEAD_DIM).astype(BF16)
    cmp_new = _compress(x_new, lw["nsa_cw1"], lw["nsa_cw2"], B * n_new).reshape(2, B, n_new, HEAD_DIM)
    n_pad = -(-n_blk // SUBLANES) * SUBLANES
    cmp = jnp.concatenate([cmp_past, cmp_new, jnp.zeros((2, B, n_pad - n_blk, HEAD_DIM), F32)], axis=2)

    g32_b = to_b(proj32[:, C_G:C_G + LANES])
    o_nsa = _nsa_sample(to_b(q16).astype(F32), to_b(kv16).astype(F32), win_all, layer, cmp[0], cmp[1], g32_b, ksv)
    o_sb = _sb_sample(to_b(proj16[:, C_SQ:C_SQ + 3 * GROUP_WIDTH]).astype(F32), cache_sb, layer, page_table)
    to_t = lambda a: a.swapaxes(0, 1).reshape(M, -1).astype(BF16)
    o_nsa, o_sb = to_t(o_nsa), to_t(o_sb)

    proj32_3d = proj32.reshape(T, B, N_PROJ)
    o_conv, glu = _conv_sample(conv_state.swapaxes(0, 1), proj32_3d, lw["conv_w"], lw["conv_b"],
                               lw["conv_ln_g"], lw["conv_ln_b"])
    o_gm, vn = _gm_sample(proj32_3d, lw["gm_wexp"], lw["gm_bexp"], lw["gm_ln_g"], lw["gm_ln_b"])

    mix = (o_nsa, o_conv.reshape(M, GROUP_WIDTH), o_gm.reshape(M, GROUP_WIDTH), o_sb)
    h32, h16 = _outproj(mix, lw["w_out"], x32, lw["ln1_g"], lw["ln1_b"], min(512, M))
    ffn_state_tm = ffn_state.swapaxes(0, 1).reshape(2 * B, 2 * D_FF)
    g16, tail_a, tail_b = _ffn_up(h16, lw["ffn_up"], lw["ffn_conv_w"], lw["ffn_conv_b"], ffn_state_tm,
                                  shift=B, seq_len=T)
    y32, y16 = _ffn_down(g16, lw["ffn_down"], h32, lw["ln2_g"], lw["ln2_b"], min(512, M))

    nsa_rows = kv32_b[:, :, :4 * HEAD_DIM].reshape(B, T, 4, HEAD_DIM)
    win_new = kv32_b[:, :, 4 * HEAD_DIM:].reshape(B, T, 2, HEAD_DIM)
    wcat = jnp.concatenate([win_state, win_new], 1)
    win_rows = wcat[:, wcat.shape[1] - min(NSA_WINDOW, P + T):]
    sb_rows = to_b(proj32[:, C_SK:C_SK + 2 * GROUP_WIDTH]).reshape(B, T, 2, HEADS, HEAD_DIM)
    ccat = jnp.concatenate([conv_state, glu.swapaxes(0, 1)], 1)
    conv_rows = ccat[:, ccat.shape[1] - (CONV_WIDTH - 1):]
    ffn_rows = jnp.concatenate([tail_a, tail_b], -1).reshape(2, B, 2 * D_FF).swapaxes(0, 1)
    gm_v = vn.swapaxes(0, 1)
    return y32, y16, (nsa_rows, sb_rows, win_rows, conv_rows, ffn_rows, gm_v)


def _layer_weights(l, T_s, w_in, nsa_pe, nsa_cw1, nsa_cw2, conv_w, conv_b, conv_ln_g, conv_ln_b,
                   gm_ln_g, gm_ln_b, gm_ws, gm_bias, w_out, ln1_g, ln1_b,
                   ffn_up, ffn_conv_w, ffn_conv_b, ffn_down, ln2_g, ln2_b):
    cs = min(GM_CHUNK, T_s)
    ws_s = jnp.tril(gm_ws[l][:, :cs, :cs])
    wexp = jnp.repeat(ws_s.transpose(1, 2, 0), HEAD_DIM, axis=2)
    bexp = jnp.repeat(gm_bias[l][:, :cs].T, HEAD_DIM, axis=1)
    return {
        "w_in": _w_in_rows(w_in, l),
        "nsa_pe": nsa_pe[l],
        "nsa_cw1": nsa_cw1[l].astype(BF16), "nsa_cw2": nsa_cw2[l].astype(BF16),
        "conv_w": conv_w[l], "conv_b": _row2(conv_b[l]),
        "conv_ln_g": _row2(conv_ln_g[l]), "conv_ln_b": _row2(conv_ln_b[l]),
        "gm_ln_g": _row2(gm_ln_g[l]), "gm_ln_b": _row2(gm_ln_b[l]),
        "gm_ws": gm_ws[l], "gm_bias_t": gm_bias[l].T, "gm_wexp": wexp, "gm_bexp": bexp,
        "w_out": w_out[l].astype(BF16), "ln1_g": _row2(ln1_g[l]), "ln1_b": _row2(ln1_b[l]),
        "ffn_up": ffn_up[l].astype(BF16), "ffn_conv_w": ffn_conv_w[l], "ffn_conv_b": _row2(ffn_conv_b[l]),
        "ffn_down": ffn_down[l].astype(BF16), "ln2_g": _row2(ln2_g[l]), "ln2_b": _row2(ln2_b[l]),
    }


def kernel(x_prompt, x_sample, cache_nsa_kv, cache_sb_kv, state_nsa_win, state_conv, state_ffn, page_table,
           w_in, nsa_pe, nsa_cw1, nsa_cw2, conv_w, conv_b, conv_ln_g, conv_ln_b, gm_ln_g, gm_ln_b, gm_ws, gm_bias,
           w_out, ln1_g, ln1_b, ffn_up, ffn_conv_w, ffn_conv_b, ffn_down, ln2_g, ln2_b):
    Bp, Tp, _ = x_prompt.shape
    Bs, Ts, _ = x_sample.shape
    depth = w_in.shape[0]
    cache_nsa = cache_nsa_kv.reshape(cache_nsa_kv.shape[:2] + (-1, HEAD_DIM))
    cache_sb = cache_sb_kv.reshape(cache_sb_kv.shape[:2] + (-1, HEAD_DIM))
    win_all = state_nsa_win.reshape(state_nsa_win.shape[:2] + (-1, HEAD_DIM))

    xp32 = x_prompt.reshape(Bp * Tp, D_MODEL)
    xs32 = x_sample.swapaxes(0, 1).reshape(Ts * Bs, D_MODEL)
    xp16, xs16 = xp32.astype(BF16), xs32.astype(BF16)
    outs_p, outs_s = [], []
    for l in range(depth):
        lw = _layer_weights(l, Ts, w_in, nsa_pe, nsa_cw1, nsa_cw2, conv_w, conv_b, conv_ln_g, conv_ln_b,
                            gm_ln_g, gm_ln_b, gm_ws, gm_bias, w_out, ln1_g, ln1_b,
                            ffn_up, ffn_conv_w, ffn_conv_b, ffn_down, ln2_g, ln2_b)
        xp32, xp16, sp = _layer_prompt(xp32, xp16, lw, Bp, Tp)
        xs32, xs16, ss = _layer_sample(xs32, xs16, lw, l, Bs, Ts, cache_nsa, cache_sb, win_all,
                                       state_conv[l], state_ffn[l], page_table)
        outs_p.append(sp)
        outs_s.append(ss)
    y_p = xp32.reshape(Bp, Tp, D_MODEL)
    y_s = xs32.reshape(Ts, Bs, D_MODEL).swapaxes(0, 1)
    st = lambda outs, i: jnp.stack([o[i] for o in outs])
    return (y_p, y_s, st(outs_p, 0), st(outs_s, 0), st(outs_p, 1), st(outs_s, 1), st(outs_p, 2), st(outs_s, 2),
            st(outs_p, 3), st(outs_s, 3), st(outs_p, 4), st(outs_s, 4), st(outs_s, 5))
```

```python
import functools
import math

import jax
import jax.numpy as jnp
import numpy as np
from jax import lax
from jax.experimental import pallas as pl
from jax.experimental.pallas import tpu as pltpu

F32 = jnp.float32
BF16 = jnp.bfloat16

D_MODEL = 2048
HEAD_DIM = 128
HEADS = 4
GROUP_WIDTH = HEADS * HEAD_DIM
NSA_BLOCK = 64
NSA_TOP_N = 16
NSA_WINDOW = 512
CONV_WIDTH = 31
GM_CHUNK = 128
D_FF = (11 * D_MODEL) // 4
FFN_CONV_WIDTH = 3
ROPE_THETA = 10000.0
LN_EPS = 1e-5
DEPTH = 2
ALPHA = (2 * DEPTH) ** 0.25
PAGE_SIZE = 128
SCALE = HEAD_DIM ** -0.5

VMEM_LIMIT_V7X = 56 * 1024 * 1024
SUBLANES = 8
LANES = 128

C_CONV = 0
C_GM = C_CONV + 2 * GROUP_WIDTH
C_SQ = C_GM + 2 * GROUP_WIDTH
C_SK = C_SQ + GROUP_WIDTH
C_SV = C_SK + GROUP_WIDTH
C_Q = C_SV + GROUP_WIDTH
C_KV = C_Q + GROUP_WIDTH
C_G = C_KV + 6 * HEAD_DIM
N_PROJ = 5120
PROJ_TN = 512


def _cparams(sem):
    return pltpu.CompilerParams(dimension_semantics=sem, vmem_limit_bytes=VMEM_LIMIT_V7X)


def _layer_norm(x, g, b):
    mu = jnp.mean(x, -1, keepdims=True)
    xc = x - mu
    var = jnp.mean(xc * xc, -1, keepdims=True)
    return xc * lax.rsqrt(var + LN_EPS) * g + b


def _masked_softmax(s, mask):
    sm = jnp.where(mask, s, -1e30)
    m = jnp.max(sm, -1, keepdims=True)
    e = jnp.where(mask, jnp.exp(sm - m), 0.0)
    return e / jnp.maximum(jnp.sum(e, -1, keepdims=True), jnp.finfo(F32).tiny)


def _dot_t(a, b):
    return lax.dot_general(a, b, (((1,), (1,)), ((), ())), preferred_element_type=F32)


def _dot(a, b):
    return jnp.dot(a, b, preferred_element_type=F32)


def _split_dot(x, u16):
    hi = x.astype(BF16)
    lo = (x - hi.astype(F32)).astype(BF16)
    return _dot(hi, u16) + _dot(lo, u16)


def _proj_kernel(x_ref, wt_ref, o32_ref, o16_ref):
    acc = _dot_t(x_ref[...], wt_ref[...])
    o32_ref[...] = acc
    o16_ref[...] = acc.astype(BF16)


def _proj(x16, wt16, tm):
    M, K = x16.shape
    N = wt16.shape[0]
    return pl.pallas_call(
        _proj_kernel,
        grid=(M // tm, N // PROJ_TN),
        in_specs=[pl.BlockSpec((tm, K), lambda i, j: (i, 0)),
                  pl.BlockSpec((PROJ_TN, K), lambda i, j: (j, 0))],
        out_specs=[pl.BlockSpec((tm, PROJ_TN), lambda i, j: (i, j)),
                   pl.BlockSpec((tm, PROJ_TN), lambda i, j: (i, j))],
        out_shape=[jax.ShapeDtypeStruct((M, N), F32), jax.ShapeDtypeStruct((M, N), BF16)],
        compiler_params=_cparams(("parallel", "parallel")),
        name="proj",
    )(x16, wt16)


def _rope_kernel(q_ref, kva_ref, kvb_ref, kvc_ref, cos_ref, sin_ref, pe_ref,
                 q16_ref, kv32_ref, kv16_ref, xkv_ref):
    cos = cos_ref[...]
    sin = sin_ref[...]

    def rot(x):
        return x * cos + pltpu.roll(x, HEAD_DIM // 2, axis=1) * sin

    for h in range(HEADS):
        sl = slice(h * HEAD_DIM, (h + 1) * HEAD_DIM)
        q16_ref[:, sl] = rot(q_ref[:, sl]).astype(BF16)
    for p, ref in enumerate((kva_ref, kvb_ref, kvc_ref)):
        k = rot(ref[:, :HEAD_DIM])
        v = ref[:, HEAD_DIM:]
        ks = slice(2 * p * HEAD_DIM, (2 * p + 1) * HEAD_DIM)
        vs = slice((2 * p + 1) * HEAD_DIM, (2 * p + 2) * HEAD_DIM)
        kv32_ref[:, ks] = k
        kv32_ref[:, vs] = v
        kv16_ref[:, ks] = k.astype(BF16)
        kv16_ref[:, vs] = v.astype(BF16)
        if p == 0:
            xkv_ref[0] = (k + pe_ref[0]).astype(BF16)
            xkv_ref[1] = (v + pe_ref[1]).astype(BF16)


def _rope(proj32, cos2, sin2, pe_t, tm):
    M = proj32.shape[0]
    qb = C_Q // GROUP_WIDTH
    kb = C_KV // (2 * HEAD_DIM)
    row = lambda i: (i, 0)
    return pl.pallas_call(
        _rope_kernel,
        grid=(M // tm,),
        in_specs=[pl.BlockSpec((tm, GROUP_WIDTH), lambda i: (i, qb)),
                  pl.BlockSpec((tm, 2 * HEAD_DIM), lambda i: (i, kb)),
                  pl.BlockSpec((tm, 2 * HEAD_DIM), lambda i: (i, kb + 1)),
                  pl.BlockSpec((tm, 2 * HEAD_DIM), lambda i: (i, kb + 2)),
                  pl.BlockSpec((tm, HEAD_DIM), row),
                  pl.BlockSpec((tm, HEAD_DIM), row),
                  pl.BlockSpec((2, tm, HEAD_DIM), lambda i: (0, 0, 0))],
        out_specs=[pl.BlockSpec((tm, GROUP_WIDTH), row),
                   pl.BlockSpec((tm, 6 * HEAD_DIM), row),
                   pl.BlockSpec((tm, 6 * HEAD_DIM), row),
                   pl.BlockSpec((2, tm, HEAD_DIM), lambda i: (0, i, 0))],
        out_shape=[jax.ShapeDtypeStruct((M, GROUP_WIDTH), BF16),
                   jax.ShapeDtypeStruct((M, 6 * HEAD_DIM), F32),
                   jax.ShapeDtypeStruct((M, 6 * HEAD_DIM), BF16),
                   jax.ShapeDtypeStruct((2, M, HEAD_DIM), BF16)],
        compiler_params=_cparams(("parallel",)),
        name="rope",
    )(proj32, proj32, proj32, proj32, cos2, sin2, pe_t)


def _gelu_tanh(x):
    return x * (0.5 * (1.0 + jnp.tanh(math.sqrt(2.0 / math.pi) * (x + 0.044715 * (x * x * x)))))


def _cmp_kernel(x_ref, w1_ref, w2_ref, o_ref):
    h = _gelu_tanh(_dot(x_ref[0], w1_ref[0]))
    o_ref[0] = _dot(h.astype(BF16), w2_ref[0])


def _compress(x16, w1, w2, tm):
    _, R, K = x16.shape
    d = w1.shape[2]
    return pl.pallas_call(
        _cmp_kernel,
        grid=(2, R // tm),
        in_specs=[pl.BlockSpec((1, tm, K), lambda s, i: (s, i, 0)),
                  pl.BlockSpec((1, K, d), lambda s, i: (s, 0, 0)),
                  pl.BlockSpec((1, d, d), lambda s, i: (s, 0, 0))],
        out_specs=pl.BlockSpec((1, tm, d), lambda s, i: (s, i, 0)),
        out_shape=jax.ShapeDtypeStruct((2, R, d), F32),
        compiler_params=_cparams(("parallel", "parallel")),
        name="nsa_compress",
    )(x16, w1, w2)


CMP_PITCH = NSA_BLOCK + SUBLANES


def _cmp_rows_kernel(x_ref, w1_ref, w2_ref, o_ref, *, tm):
    acc = jnp.zeros((tm, HEAD_DIM), F32)
    for p in range(0, NSA_BLOCK, 2):
        lhs = jnp.concatenate([x_ref[0, pl.ds(p, tm, stride=CMP_PITCH), :],
                               x_ref[0, pl.ds(p + 1, tm, stride=CMP_PITCH), :]], axis=1).astype(BF16)
        acc = acc + _dot(lhs, w1_ref[0, p * HEAD_DIM:(p + 2) * HEAD_DIM, :])
    o_ref[0] = _dot(_gelu_tanh(acc).astype(BF16), w2_ref[0])


def _compress_rows(x32, w1, w2, tm):
    R = x32.shape[1] // CMP_PITCH
    K, d = w1.shape[1:]
    return pl.pallas_call(
        functools.partial(_cmp_rows_kernel, tm=tm),
        grid=(2, R // tm),
        in_specs=[pl.BlockSpec((1, tm * CMP_PITCH, d), lambda s, i: (s, i, 0)),
                  pl.BlockSpec((1, K, d), lambda s, i: (s, 0, 0)),
                  pl.BlockSpec((1, d, d), lambda s, i: (s, 0, 0))],
        out_specs=pl.BlockSpec((1, tm, d), lambda s, i: (s, i, 0)),
        out_shape=jax.ShapeDtypeStruct((2, R, d), F32),
        compiler_params=_cparams(("parallel", "parallel")),
        name="nsa_compress_rows",
    )(x32, w1, w2)


def _select_blocks(imp, tpos, n_blk, n_real):
    nidx = lax.broadcasted_iota(jnp.int32, imp.shape, 1)
    cur = tpos // NSA_BLOCK
    forced = (nidx == 0) | (nidx == cur) | (nidx == cur - 1)
    started = nidx * NSA_BLOCK <= tpos
    score = jnp.where(forced, jnp.inf, jnp.where(started, imp, -jnp.inf))
    score = jnp.where(nidx < n_real, score, -jnp.inf)
    rank = jnp.zeros(imp.shape, jnp.int32)
    for n2 in range(n_real):
        c = score[:, n2:n2 + 1]
        ahead = (c > score) | ((c == score) & (nidx > n2))
        rank = rank + ahead.astype(jnp.int32)
    return (rank < min(NSA_TOP_N, n_real)) & (nidx < n_real)


def _gate_cols(gs, h):
    return [gs[:, 3 * h + c:3 * h + c + 1] for c in range(3)]


MASKED = -1e30


def _nsa_prompt_kernel(q_ref, ks_ref, vs_ref, kw_ref, vw_ref, kc_ref, vc_ref, g_ref, o_ref, selk_ref, *, T, TQ, KC):
    i = pl.program_id(1)
    n_blk = T // NSA_BLOCK
    R = HEADS * TQ
    t0 = i * TQ
    tpos = t0 + lax.broadcasted_iota(jnp.int32, (TQ, 1), 0)
    tpos_r = t0 + lax.broadcasted_iota(jnp.int32, (R, 1), 0) % TQ
    per_head = lambda x: jnp.concatenate([x] * HEADS, axis=0)
    q = jnp.concatenate([q_ref[:, h * HEAD_DIM:(h + 1) * HEAD_DIM] for h in range(HEADS)], axis=0)
    gs = jax.nn.sigmoid(g_ref[...])

    kc = kc_ref[0].astype(BF16)
    vc = vc_ref[0].astype(BF16)
    cmask = (lax.broadcasted_iota(jnp.int32, (R, n_blk), 1) + 1) * NSA_BLOCK - 1 <= tpos_r
    p_c = _masked_softmax(_dot_t(q, kc) * SCALE, cmask)
    o_cmp = _dot(p_c.astype(BF16), vc)
    imp = p_c[0:TQ]
    for h in range(1, HEADS):
        imp = imp + p_c[h * TQ:(h + 1) * TQ]

    sel = _select_blocks(imp, tpos, n_blk, n_blk)
    expand = (lax.broadcasted_iota(jnp.int32, (n_blk, T), 1) // NSA_BLOCK
              == lax.broadcasted_iota(jnp.int32, (n_blk, T), 0))
    selk_ref[...] = _dot(jnp.where(sel, 1.0, 0.0).astype(BF16), jnp.where(expand, 1.0, 0.0).astype(BF16))

    lane = lax.broadcasted_iota(jnp.int32, (TQ, KC), 1)

    def body(c, carry):
        m, l, acc = carry
        k0 = pl.multiple_of(c * KC, KC)
        keep = (selk_ref[:, pl.ds(k0, KC)] > 0.5) & (k0 + lane <= tpos)
        s = _dot_t(q, ks_ref[pl.ds(k0, KC), :]) * SCALE + per_head(jnp.where(keep, 0.0, MASKED))
        m_new = jnp.maximum(m, jnp.max(s, -1, keepdims=True))
        scale_old = jnp.exp(m - m_new)
        p = jnp.exp(s - m_new)
        l = scale_old * l + jnp.sum(p, -1, keepdims=True)
        acc = scale_old * acc + _dot(p.astype(BF16), vs_ref[pl.ds(k0, KC), :])
        return m_new, l, acc

    n_chunks = (t0 + TQ + KC - 1) // KC
    _, l, acc = lax.fori_loop(0, n_chunks, body, (jnp.full((R, 1), MASKED, F32), jnp.zeros((R, 1), F32),
                                                  jnp.zeros((R, HEAD_DIM), F32)))
    o_sel = acc / l

    band = min(NSA_WINDOW + TQ, T)
    w0 = pl.multiple_of(jnp.clip(t0 - NSA_WINDOW, 0, T - band), LANES)
    wpos = w0 + lax.broadcasted_iota(jnp.int32, (TQ, band), 1)
    keep = (wpos <= tpos) & (wpos > tpos - NSA_WINDOW)
    s = _dot_t(q, kw_ref[pl.ds(w0, band), :]) * SCALE + per_head(jnp.where(keep, 0.0, MASKED))
    p = jnp.exp(s - jnp.max(s, -1, keepdims=True))
    o_win = _dot(p.astype(BF16), vw_ref[pl.ds(w0, band), :]) / jnp.sum(p, -1, keepdims=True)

    for h in range(HEADS):
        g0, g1, g2 = _gate_cols(gs, h)
        rs = slice(h * TQ, (h + 1) * TQ)
        o = g0 * o_cmp[rs] + g1 * o_sel[rs] + g2 * o_win[rs]
        o_ref[:, h * HEAD_DIM:(h + 1) * HEAD_DIM] = o.astype(BF16)


def _nsa_prompt(q16, kv16, kcmp, vcmp, proj32, B, T):
    TQ = min(128, T)
    KC = min(256, T)
    nq = T // TQ
    n_blk = T // NSA_BLOCK
    kvcol = lambda c: pl.BlockSpec((T, HEAD_DIM), lambda b, i: (b, c))
    return pl.pallas_call(
        functools.partial(_nsa_prompt_kernel, T=T, TQ=TQ, KC=KC),
        grid=(B, nq),
        scratch_shapes=[pltpu.VMEM((TQ, T), F32)],
        in_specs=[pl.BlockSpec((TQ, GROUP_WIDTH), lambda b, i: (b * nq + i, 0)),
                  kvcol(2), kvcol(3), kvcol(4), kvcol(5),
                  pl.BlockSpec((1, n_blk, HEAD_DIM), lambda b, i: (b, 0, 0)),
                  pl.BlockSpec((1, n_blk, HEAD_DIM), lambda b, i: (b, 0, 0)),
                  pl.BlockSpec((TQ, LANES), lambda b, i: (b * nq + i, C_G // LANES))],
        out_specs=pl.BlockSpec((TQ, GROUP_WIDTH), lambda b, i: (b * nq + i, 0)),
        out_shape=jax.ShapeDtypeStruct((B * T, GROUP_WIDTH), BF16),
        compiler_params=_cparams(("parallel", "arbitrary")),
        name="nsa_prompt",
    )(q16, kv16, kv16, kv16, kv16, kcmp, vcmp, proj32)


def _rows(ref, r, n, stride):
    return ref[0, 0, pl.ds(r, n, stride=stride), :]


def _gather_nsa_kernel(pt_ref, pe_ref, *refs, n_pages):
    pages = refs[:n_pages]
    xkv_ref, ksv_ref = refs[n_pages:]
    blocks_per_page = PAGE_SIZE // NSA_BLOCK
    pad = jnp.zeros((CMP_PITCH - NSA_BLOCK, HEAD_DIM), F32)
    for j in range(n_pages):
        sl = slice(j * PAGE_SIZE, (j + 1) * PAGE_SIZE)
        ksv_ref[0, 0, sl, :] = _rows(pages[j], 2, PAGE_SIZE, 4).astype(BF16)
        ksv_ref[1, 0, sl, :] = _rows(pages[j], 3, PAGE_SIZE, 4).astype(BF16)
        for s in range(2):
            x = _rows(pages[j], s, PAGE_SIZE, 4)
            for u in range(blocks_per_page):
                r0 = (j * blocks_per_page + u) * CMP_PITCH
                xkv_ref[s, r0:r0 + NSA_BLOCK, :] = x[u * NSA_BLOCK:(u + 1) * NSA_BLOCK] + pe_ref[s]
                xkv_ref[s, r0 + NSA_BLOCK:r0 + CMP_PITCH, :] = pad


def _gather_nsa(cache, layer, page_table, pe):
    B, n_pages = page_table.shape
    P = n_pages * PAGE_SIZE
    rows_b = (P // NSA_BLOCK) * CMP_PITCH

    def page_spec(j):
        return pl.BlockSpec((1, 1, PAGE_SIZE * 4, HEAD_DIM), lambda b, pt: (layer, pt[b, j], 0, 0))

    out = pl.BlockSpec((2, 1, P, HEAD_DIM), lambda b, pt: (0, b, 0, 0))
    return pl.pallas_call(
        functools.partial(_gather_nsa_kernel, n_pages=n_pages),
        grid_spec=pltpu.PrefetchScalarGridSpec(
            num_scalar_prefetch=1, grid=(B,),
            in_specs=[pl.BlockSpec((2, NSA_BLOCK, HEAD_DIM), lambda b, pt: (0, 0, 0))]
            + [page_spec(j) for j in range(n_pages)],
            out_specs=[pl.BlockSpec((2, rows_b, HEAD_DIM), lambda b, pt: (0, b, 0)), out]),
        out_shape=[jax.ShapeDtypeStruct((2, B * rows_b, HEAD_DIM), F32),
                   jax.ShapeDtypeStruct((2, B, P, HEAD_DIM), BF16)],
        compiler_params=_cparams(("arbitrary",)),
        name="nsa_gather",
    )(page_table, pe, *([cache] * n_pages))


def _nsa_sample_kernel(q_ref, kvn_ref, win_ref, kc_ref, vc_ref, g_ref, ks_ref, vs_ref, o_ref, *, T, P, n_blk, n_pad):
    R = HEADS * T
    row = lax.broadcasted_iota(jnp.int32, (R, 1), 0)
    tq = row % T
    tpos = P + tq
    q_bt = q_ref[0]
    q = jnp.concatenate([q_bt[:, h * HEAD_DIM:(h + 1) * HEAD_DIM] for h in range(HEADS)], axis=0).astype(BF16)
    gs = jax.nn.sigmoid(g_ref[0])

    kc = kc_ref[0].astype(BF16)
    vc = vc_ref[0].astype(BF16)
    nidx = lax.broadcasted_iota(jnp.int32, (R, n_pad), 1)
    cmask = ((nidx + 1) * NSA_BLOCK - 1 <= tpos) & (nidx < n_blk)
    p_c = _masked_softmax(_dot_t(q, kc) * SCALE, cmask)
    o_cmp = _dot(p_c.astype(BF16), vc)
    imp = p_c[0:T]
    for h in range(1, HEADS):
        imp = imp + p_c[h * T:(h + 1) * T]
    sel_t = _select_blocks(imp, P + lax.broadcasted_iota(jnp.int32, (T, 1), 0), n_pad, n_blk)
    sel = jnp.concatenate([jnp.where(sel_t, 1.0, 0.0)] * HEADS, axis=0)

    lane = lax.broadcasted_iota(jnp.int32, (R, PAGE_SIZE), 1)
    kvn = kvn_ref[0]
    pad_rows = jnp.zeros((PAGE_SIZE - T, HEAD_DIM), F32)
    new_chunk = lambda c: jnp.concatenate([kvn[:, c * HEAD_DIM:(c + 1) * HEAD_DIM], pad_rows], axis=0).astype(BF16)
    ks_new, vs_new, kw_new, vw_new = new_chunk(2), new_chunk(3), new_chunk(4), new_chunk(5)
    expand = (lax.broadcasted_iota(jnp.int32, (n_pad, P), 1) // NSA_BLOCK
              == lax.broadcasted_iota(jnp.int32, (n_pad, P), 0))
    selk = _dot(sel.astype(BF16), jnp.where(expand, 1.0, 0.0).astype(BF16))
    m_past = (selk > 0.5) & (lax.broadcasted_iota(jnp.int32, (R, P), 1) <= tpos)
    s_past = _dot_t(q, ks_ref[0, 0]) * SCALE
    n_past = P // NSA_BLOCK
    chosen = jnp.zeros((R, PAGE_SIZE), F32)
    for u in range(PAGE_SIZE // NSA_BLOCK):
        if n_past + u < n_blk:
            in_blk = (lane >= u * NSA_BLOCK) & (lane < (u + 1) * NSA_BLOCK)
            chosen = jnp.where(in_blk, sel[:, n_past + u:n_past + u + 1], chosen)
    m_new = (chosen > 0.5) & (P + lane <= tpos)
    s_new = _dot_t(q, ks_new) * SCALE
    o_sel = _multi_chunk_attention([s_past, s_new], [m_past, m_new], [vs_ref[0, 0], vs_new])

    W = win_ref.shape[2] // 2
    kw = _rows(win_ref, 0, W, 2).astype(BF16)
    vw = _rows(win_ref, 1, W, 2).astype(BF16)
    wpos = P - W + lax.broadcasted_iota(jnp.int32, (R, W), 1)
    m_w = (wpos <= tpos) & (wpos > tpos - NSA_WINDOW) & (wpos >= 0)
    m_wn = (P + lane <= tpos) & (P + lane > tpos - NSA_WINDOW)
    o_win = _multi_chunk_attention([_dot_t(q, kw) * SCALE, _dot_t(q, kw_new) * SCALE], [m_w, m_wn], [vw, vw_new])

    for h in range(HEADS):
        g0, g1, g2 = _gate_cols(gs, h)
        rs = slice(h * T, (h + 1) * T)
        o_ref[0, :, h * HEAD_DIM:(h + 1) * HEAD_DIM] = g0 * o_cmp[rs] + g1 * o_sel[rs] + g2 * o_win[rs]


def _multi_chunk_attention(s_list, m_list, v_list):
    sm = [jnp.where(m, s, -1e30) for s, m in zip(s_list, m_list)]
    mx = jnp.max(sm[0], -1, keepdims=True)
    for s in sm[1:]:
        mx = jnp.maximum(mx, jnp.max(s, -1, keepdims=True))
    es = [jnp.where(m, jnp.exp(s - mx), 0.0) for s, m in zip(sm, m_list)]
    tot = jnp.sum(es[0], -1, keepdims=True)
    for e in es[1:]:
        tot = tot + jnp.sum(e, -1, keepdims=True)
    den = jnp.maximum(tot, jnp.finfo(F32).tiny)
    acc = _dot(es[0].astype(BF16), v_list[0])
    for e, v in zip(es[1:], v_list[1:]):
        acc = acc + _dot(e.astype(BF16), v)
    return acc / den


def _nsa_sample(q_b, kv_b, win_all, layer, kcmp, vcmp, g32_b, ksv):
    B, T, _ = q_b.shape
    P = ksv.shape[2]
    n_blk = -(-(P + T) // NSA_BLOCK)
    n_pad = kcmp.shape[1]
    W2 = win_all.shape[2]
    per_b = lambda *tail: pl.BlockSpec((1,) + tail, lambda b: (b,) + (0,) * len(tail))
    return pl.pallas_call(
        functools.partial(_nsa_sample_kernel, T=T, P=P, n_blk=n_blk, n_pad=n_pad),
        grid=(B,),
        in_specs=[per_b(T, GROUP_WIDTH), per_b(T, 6 * HEAD_DIM),
                  pl.BlockSpec((1, 1, W2, HEAD_DIM), lambda b: (layer, b, 0, 0)),
                  per_b(n_pad, HEAD_DIM), per_b(n_pad, HEAD_DIM), per_b(T, LANES),
                  pl.BlockSpec((1, 1, P, HEAD_DIM), lambda b: (0, b, 0, 0)),
                  pl.BlockSpec((1, 1, P, HEAD_DIM), lambda b: (1, b, 0, 0))],
        out_specs=per_b(T, GROUP_WIDTH),
        out_shape=jax.ShapeDtypeStruct((B, T, GROUP_WIDTH), F32),
        compiler_params=_cparams(("parallel",)),
        name="nsa_sample",
    )(q_b, kv_b, win_all, kcmp, vcmp, g32_b, ksv, ksv)


def _log_sigmoid_pair(z):
    ls_pos = jnp.minimum(z, 0.0) - jnp.log1p(jnp.exp(-jnp.abs(z)))
    return ls_pos, ls_pos - z


def _strict_upper_ones(n):
    r = lax.broadcasted_iota(jnp.int32, (n, n), 0)
    c = lax.broadcasted_iota(jnp.int32, (n, n), 1)
    return jnp.where(r > c, 1.0, 0.0).astype(BF16)


def _sb_prompt_kernel(q_ref, k_ref, v_ref, o_ref, *, TQ, KC):
    qi = pl.program_id(2)
    q = q_ref[...]
    t0 = qi * TQ
    later = _strict_upper_ones(KC)
    tpos = t0 + lax.broadcasted_iota(jnp.int32, (TQ, 1), 0)
    lane = lax.broadcasted_iota(jnp.int32, (TQ, KC), 1)

    def chunk(k0, carry, diagonal):
        acc, after_c = carry
        z = _dot_t(q, k_ref[pl.ds(k0, KC), :]) * SCALE
        e = jnp.exp(-jnp.abs(z))
        log_keep = -(jnp.maximum(z, 0.0) + jnp.log(1.0 + e))
        r = 1.0 / (1.0 + e)
        beta = jnp.where(z >= 0.0, r, e * r)
        if diagonal:
            mask = k0 + lane < tpos
            log_keep = jnp.where(mask, log_keep, 0.0)
            beta = jnp.where(mask, beta, 0.0)
        a = beta * jnp.exp(_split_dot(log_keep, later) + after_c)
        acc = acc + _dot(a.astype(BF16), v_ref[pl.ds(k0, KC), :])
        return acc, after_c + jnp.sum(log_keep, -1, keepdims=True)

    carry = (jnp.zeros((TQ, HEAD_DIM), F32), jnp.zeros((TQ, 1), F32))
    n_diag = TQ // KC
    for d in range(n_diag):
        carry = chunk(pl.multiple_of(t0 + (n_diag - 1 - d) * KC, KC), carry, True)
    n_full = t0 // KC
    acc, _ = lax.fori_loop(0, n_full, lambda c, cr: chunk(pl.multiple_of((n_full - 1 - c) * KC, KC), cr, False), carry)
    o_ref[...] = acc.astype(BF16)


def _sb_prompt(proj16, B, T):
    TQ = min(512, T)
    KC = min(256, T)
    nq = T // TQ
    col = lambda base: (lambda b, h, i: (b, base // HEAD_DIM + h))
    return pl.pallas_call(
        functools.partial(_sb_prompt_kernel, TQ=TQ, KC=KC),
        grid=(B, HEADS, nq),
        in_specs=[pl.BlockSpec((TQ, HEAD_DIM), lambda b, h, i: (b * nq + i, C_SQ // HEAD_DIM + h)),
                  pl.BlockSpec((T, HEAD_DIM), col(C_SK)),
                  pl.BlockSpec((T, HEAD_DIM), col(C_SV))],
        out_specs=pl.BlockSpec((TQ, HEAD_DIM), lambda b, h, i: (b * nq + i, h)),
        out_shape=jax.ShapeDtypeStruct((B * T, GROUP_WIDTH), BF16),
        compiler_params=_cparams(("parallel", "parallel", "arbitrary")),
        name="sb_prompt",
    )(proj16, proj16, proj16)


def _sb_sample_kernel(pt_ref, qkv_ref, *refs, T, P, n_pages):
    pages = refs[:n_pages]
    o_ref = refs[n_pages]
    R = HEADS * T
    C = PAGE_SIZE
    head = lax.broadcasted_iota(jnp.int32, (R, 1), 0) // T
    qkv = qkv_ref[0]
    q_t = qkv[:, :GROUP_WIDTH]
    col_head = lax.broadcasted_iota(jnp.int32, (R, GROUP_WIDTH), 1) // HEAD_DIM
    q_bd = jnp.where(col_head == head, jnp.concatenate([q_t] * HEADS, axis=0), 0.0).astype(BF16)
    pad_rows = jnp.zeros((C - T, GROUP_WIDTH), F32)
    k_new = jnp.concatenate([qkv[:, GROUP_WIDTH:2 * GROUP_WIDTH], pad_rows], axis=0).astype(BF16)
    v_new = jnp.concatenate([qkv[:, 2 * GROUP_WIDTH:], pad_rows], axis=0).astype(BF16)

    n_ch = n_pages + 1
    zs, vs = [], []
    for j in range(n_ch):
        if j < n_pages:
            kj = jnp.concatenate([_rows(pages[j], h, C, 2 * HEADS) for h in range(HEADS)], axis=1).astype(BF16)
            vj = jnp.concatenate([_rows(pages[j], HEADS + h, C, 2 * HEADS) for h in range(HEADS)], axis=1).astype(BF16)
        else:
            kj, vj = k_new, v_new
        zs.append(_dot_t(q_bd, kj) * SCALE)
        vs.append(vj)
    z = jnp.concatenate(zs, axis=0)
    crow = lax.broadcasted_iota(jnp.int32, (n_ch * R, 1), 0)
    kpos = (crow // R) * C + lax.broadcasted_iota(jnp.int32, (n_ch * R, C), 1)
    mask = kpos < P + (crow % R) % T
    ls_pos, ls_neg = _log_sigmoid_pair(z)
    log_keep = jnp.where(mask, ls_neg, 0.0)
    after_local = _split_dot(log_keep, _strict_upper_ones(C))
    tot = jnp.sum(log_keep, -1, keepdims=True)
    carry = jnp.zeros((R, 1), F32)
    carries = [None] * n_ch
    for j in range(n_ch - 1, -1, -1):
        carries[j] = carry
        carry = carry + tot[j * R:(j + 1) * R]
    after = after_local + jnp.concatenate(carries, axis=0)
    a = jnp.where(mask, jnp.exp(ls_pos + after), 0.0).astype(BF16)
    acc = jnp.zeros((R, GROUP_WIDTH), F32)
    for j in range(n_ch):
        acc = acc + _dot(a[j * R:(j + 1) * R], vs[j])
    for h in range(HEADS):
        o_ref[0, :, h * HEAD_DIM:(h + 1) * HEAD_DIM] = acc[h * T:(h + 1) * T, h * HEAD_DIM:(h + 1) * HEAD_DIM]


def _sb_sample(qkv_b, cache_sb, layer, page_table):
    B, T, _ = qkv_b.shape
    n_pages = page_table.shape[1]
    P = n_pages * PAGE_SIZE

    def page_spec(j):
        return pl.BlockSpec((1, 1, PAGE_SIZE * 2 * HEADS, HEAD_DIM), lambda b, pt: (layer, pt[b, j], 0, 0))

    return pl.pallas_call(
        functools.partial(_sb_sample_kernel, T=T, P=P, n_pages=n_pages),
        grid_spec=pltpu.PrefetchScalarGridSpec(
            num_scalar_prefetch=1, grid=(B,),
            in_specs=[pl.BlockSpec((1, T, 3 * GROUP_WIDTH), lambda b, pt: (b, 0, 0))]
            + [page_spec(j) for j in range(n_pages)],
            out_specs=pl.BlockSpec((1, T, GROUP_WIDTH), lambda b, pt: (b, 0, 0))),
        out_shape=jax.ShapeDtypeStruct((B, T, GROUP_WIDTH), F32),
        compiler_params=_cparams(("arbitrary",)),
        name="sb_sample",
    )(page_table, qkv_b, *([cache_sb] * n_pages))


CONV_HIST = 32
CONV_ROWS = 32


def _conv_prompt_kernel(ab_ref, w_ref, b_ref, g_ref, be_ref, o_ref, tail_ref, buf_ref, *, tm):
    i = pl.program_id(1)

    @pl.when(i == 0)
    def _():
        buf_ref[0:CONV_HIST, :] = jnp.zeros((CONV_HIST, GROUP_WIDTH), F32)

    @pl.when(i > 0)
    def _():
        buf_ref[0:CONV_HIST, :] = buf_ref[tm:tm + CONV_HIST, :]

    a = ab_ref[:, :GROUP_WIDTH]
    glu = a * jax.nn.sigmoid(ab_ref[:, GROUP_WIDTH:])
    buf_ref[CONV_HIST:, :] = glu
    tail_ref[0] = glu[tm - CONV_HIST:, :]
    w = w_ref[...]
    for r in range(tm // CONV_ROWS):
        acc = jnp.broadcast_to(b_ref[...], (CONV_ROWS, GROUP_WIDTH))
        for k in range(CONV_WIDTH):
            start = CONV_HIST + r * CONV_ROWS - (CONV_WIDTH - 1) + k
            acc = acc + w[k:k + 1, :] * buf_ref[start:start + CONV_ROWS, :]
        y = _layer_norm(acc, g_ref[...], be_ref[...])
        o_ref[r * CONV_ROWS:(r + 1) * CONV_ROWS, :] = (y * jax.nn.sigmoid(y)).astype(BF16)


def _conv_prompt(proj32, w, b, g, be, B, T):
    tm = min(256, T)
    nt = T // tm
    vec = pl.BlockSpec((1, GROUP_WIDTH), lambda bb, i: (0, 0))
    return pl.pallas_call(
        functools.partial(_conv_prompt_kernel, tm=tm),
        grid=(B, nt),
        in_specs=[pl.BlockSpec((tm, 2 * GROUP_WIDTH), lambda bb, i: (bb * nt + i, C_CONV // (2 * GROUP_WIDTH))),
                  pl.BlockSpec((CONV_WIDTH, GROUP_WIDTH), lambda bb, i: (0, 0)), vec, vec, vec],
        out_specs=[pl.BlockSpec((tm, GROUP_WIDTH), lambda bb, i: (bb * nt + i, 0)),
                   pl.BlockSpec((1, CONV_HIST, GROUP_WIDTH), lambda bb, i: (bb, 0, 0))],
        out_shape=[jax.ShapeDtypeStruct((B * T, GROUP_WIDTH), BF16),
                   jax.ShapeDtypeStruct((B, CONV_HIST, GROUP_WIDTH), F32)],
        scratch_shapes=[pltpu.VMEM((CONV_HIST + tm, GROUP_WIDTH), F32)],
        compiler_params=_cparams(("parallel", "arbitrary")),
        name="conv_prompt",
    )(proj32, w, b, g, be)


def _conv_sample_kernel(st_ref, ab_ref, w_ref, b_ref, g_ref, be_ref, o_ref, glu_ref, *, T):
    S = CONV_WIDTH - 1
    w = w_ref[...]
    glu = []
    for t in range(T):
        ab = ab_ref[t]
        gt = ab[:, :GROUP_WIDTH] * jax.nn.sigmoid(ab[:, GROUP_WIDTH:])
        glu_ref[t] = gt
        glu.append(gt)
    for t in range(T):
        acc = jnp.broadcast_to(b_ref[...], glu[0].shape)
        for k in range(CONV_WIDTH):
            j = t + k
            acc = acc + w[k:k + 1, :] * (st_ref[j] if j < S else glu[j - S])
        y = _layer_norm(acc, g_ref[...], be_ref[...])
        o_ref[t] = (y * jax.nn.sigmoid(y)).astype(BF16)


def _conv_sample(state_tm, proj32_3d, w, b, g, be):
    S, B, _ = state_tm.shape
    T = proj32_3d.shape[0]
    bb = min(32, B)
    vec = pl.BlockSpec((1, GROUP_WIDTH), lambda i: (0, 0))
    out = pl.BlockSpec((T, bb, GROUP_WIDTH), lambda i: (0, i, 0))
    return pl.pallas_call(
        functools.partial(_conv_sample_kernel, T=T),
        grid=(B // bb,),
        in_specs=[pl.BlockSpec((S, bb, GROUP_WIDTH), lambda i: (0, i, 0)),
                  pl.BlockSpec((T, bb, 2 * GROUP_WIDTH), lambda i: (0, i, C_CONV // (2 * GROUP_WIDTH))),
                  pl.BlockSpec((CONV_WIDTH, GROUP_WIDTH), lambda i: (0, 0)), vec, vec, vec],
        out_specs=[out, out],
        out_shape=[jax.ShapeDtypeStruct((T, B, GROUP_WIDTH), BF16),
                   jax.ShapeDtypeStruct((T, B, GROUP_WIDTH), F32)],
        compiler_params=_cparams(("parallel",)),
        name="conv_sample",
    )(state_tm, proj32_3d, w, b, g, be)


def _gm_prompt_kernel(uv_ref, ws_ref, bias_ref, g_ref, be_ref, o_ref, *, cs):
    vn = _layer_norm(uv_ref[:, GROUP_WIDTH:], g_ref[...], be_ref[...])
    r = lax.broadcasted_iota(jnp.int32, (cs, cs), 0)
    c = lax.broadcasted_iota(jnp.int32, (cs, cs), 1)
    for gi in range(HEADS):
        sl = slice(gi * HEAD_DIM, (gi + 1) * HEAD_DIM)
        w = jnp.where(r >= c, ws_ref[gi], 0.0).astype(BF16)
        s = _dot(w, vn[:, sl].astype(BF16)) + bias_ref[:, gi:gi + 1]
        o_ref[:, sl] = (uv_ref[:, sl] * s).astype(BF16)


def _gm_prompt(proj32, ws, bias_t, g, be, B, T):
    cs = min(GM_CHUNK, T)
    n = B * T // cs
    vec = pl.BlockSpec((1, GROUP_WIDTH), lambda i: (0, 0))
    return pl.pallas_call(
        functools.partial(_gm_prompt_kernel, cs=cs),
        grid=(n,),
        in_specs=[pl.BlockSpec((cs, 2 * GROUP_WIDTH), lambda i: (i, C_GM // (2 * GROUP_WIDTH))),
                  pl.BlockSpec((HEADS, cs, cs), lambda i: (0, 0, 0)),
                  pl.BlockSpec((cs, HEADS), lambda i: (0, 0)), vec, vec],
        out_specs=pl.BlockSpec((cs, GROUP_WIDTH), lambda i: (i, 0)),
        out_shape=jax.ShapeDtypeStruct((B * T, GROUP_WIDTH), BF16),
        compiler_params=_cparams(("parallel",)),
        name="gm_prompt",
    )(proj32, ws, bias_t, g, be)


def _gm_sample_kernel(uv_ref, wexp_ref, bexp_ref, g_ref, be_ref, o_ref, vn_ref, *, T):
    vn = []
    for t in range(T):
        v = _layer_norm(uv_ref[t][:, GROUP_WIDTH:], g_ref[...], be_ref[...])
        vn_ref[t] = v
        vn.append(v)
    for i in range(T):
        s = jnp.broadcast_to(bexp_ref[i:i + 1, :], vn[0].shape)
        for j in range(i + 1):
            s = s + wexp_ref[i, j:j + 1, :] * vn[j]
        o_ref[i] = (uv_ref[i][:, :GROUP_WIDTH] * s).astype(BF16)


def _gm_sample(proj32_3d, wexp, bexp, g, be):
    T, B, _ = proj32_3d.shape
    bb = min(32, B)
    vec = pl.BlockSpec((1, GROUP_WIDTH), lambda i: (0, 0))
    out = pl.BlockSpec((T, bb, GROUP_WIDTH), lambda i: (0, i, 0))
    return pl.pallas_call(
        functools.partial(_gm_sample_kernel, T=T),
        grid=(B // bb,),
        in_specs=[pl.BlockSpec((T, bb, 2 * GROUP_WIDTH), lambda i: (0, i, C_GM // (2 * GROUP_WIDTH))),
                  pl.BlockSpec((T, T, GROUP_WIDTH), lambda i: (0, 0, 0)),
                  pl.BlockSpec((T, GROUP_WIDTH), lambda i: (0, 0)), vec, vec],
        out_specs=[out, out],
        out_shape=[jax.ShapeDtypeStruct((T, B, GROUP_WIDTH), BF16),
                   jax.ShapeDtypeStruct((T, B, GROUP_WIDTH), F32)],
        compiler_params=_cparams(("parallel",)),
        name="gm_sample",
    )(proj32_3d, wexp, bexp, g, be)


def _outproj_kernel(a0_ref, a1_ref, a2_ref, a3_ref, w_ref, x_ref, g_ref, b_ref, h32_ref, h16_ref):
    acc = _dot(a0_ref[...], w_ref[0:GROUP_WIDTH, :])
    for n, a_ref in enumerate((a1_ref, a2_ref, a3_ref), start=1):
        acc = acc + _dot(a_ref[...], w_ref[n * GROUP_WIDTH:(n + 1) * GROUP_WIDTH, :])
    h = _layer_norm(ALPHA * x_ref[...] + acc, g_ref[...], b_ref[...])
    h32_ref[...] = h
    h16_ref[...] = h.astype(BF16)


def _outproj(mix, w16, x32, g, b, tm):
    M = x32.shape[0]
    row = lambda i: (i, 0)
    vec = pl.BlockSpec((1, D_MODEL), lambda i: (0, 0))
    return pl.pallas_call(
        _outproj_kernel,
        grid=(M // tm,),
        in_specs=[pl.BlockSpec((tm, GROUP_WIDTH), row)] * 4
        + [pl.BlockSpec((D_MODEL, D_MODEL), lambda i: (0, 0)), pl.BlockSpec((tm, D_MODEL), row), vec, vec],
        out_specs=[pl.BlockSpec((tm, D_MODEL), row)] * 2,
        out_shape=[jax.ShapeDtypeStruct((M, D_MODEL), F32), jax.ShapeDtypeStruct((M, D_MODEL), BF16)],
        compiler_params=_cparams(("parallel",)),
        name="outproj_ln1",
    )(*mix, w16, x32, g, b)


FFN_TF = 512
FFN_CARRY = SUBLANES
FFN_TOP = 2 * SUBLANES
FFN_SUB = 256
FFN_DOWN_TK = D_FF // 4


def _ffn_up_kernel(h_ref, wa_ref, wb_ref, cwa_ref, cwb_ref, cba_ref, cbb_ref, *refs, shift, tiles_per_seq, tm):
    if shift == 1:
        g_ref, ta_ref, tb_ref, ca_ref, cb_ref = refs
    else:
        sa_ref, sb_ref, g_ref, ta_ref, tb_ref = refs
    m = pl.program_id(0)
    f = pl.program_id(1)
    h = h_ref[...]

    def conv3(cw, cb, p2, p1, x):
        return cw[0:1, :] * p2 + cw[1:2, :] * p1 + cw[2:3, :] * x + cb

    def gate(a, b):
        return (a * jax.nn.sigmoid(a) * b).astype(BF16)

    if shift == 1:
        @pl.when(m % tiles_per_seq == 0)
        def _():
            ca_ref[f] = jnp.zeros((FFN_CARRY, FFN_TF), F32)
            cb_ref[f] = jnp.zeros((FFN_CARRY, FFN_TF), F32)

    n_sub = FFN_TF // FFN_SUB
    matmuls = lambda c: [_dot(h, w_ref[:, c * FFN_SUB:(c + 1) * FFN_SUB]) for w_ref in (wa_ref, wb_ref)]
    ups_next = matmuls(0)
    for c in range(n_sub):
        cs = slice(c * FFN_SUB, (c + 1) * FFN_SUB)
        ups = ups_next
        if c + 1 < n_sub:
            ups_next = matmuls(c + 1)
        body, top = [], []
        for part in range(2):
            up = ups[part]
            cw = (cwa_ref, cwb_ref)[part][:, cs]
            cb = (cba_ref, cbb_ref)[part][:, cs]
            if shift == 1:
                carry_ref = (ca_ref, cb_ref)[part]
                prev = carry_ref[f][:, cs]
                body.append(conv3(cw, cb, pltpu.roll(up, 2, axis=0), pltpu.roll(up, 1, axis=0), up))
                hist = jnp.concatenate([prev, up[0:FFN_TOP]], axis=0)
                top.append(conv3(cw, cb, hist[FFN_CARRY - 2:FFN_CARRY - 2 + FFN_TOP],
                                 hist[FFN_CARRY - 1:FFN_CARRY - 1 + FFN_TOP], up[0:FFN_TOP]))
                tail_rows = up[tm - FFN_CARRY:, :]
                carry_ref[f, :, cs] = tail_rows
                (ta_ref, tb_ref)[part][0, :, cs] = tail_rows
            else:
                st = (sa_ref, sb_ref)[part][:, cs]
                p1 = jnp.concatenate([st[shift:], up[:tm - shift]], axis=0)
                p2 = jnp.concatenate([st, up[:tm - 2 * shift]], axis=0)
                (ta_ref, tb_ref)[part][:, cs] = up[tm - 2 * shift:, :]
                body.append(conv3(cw, cb, p2, p1, up))
        g_ref[:, cs] = gate(body[0], body[1])
        if shift == 1:
            g_ref[0:FFN_TOP, cs] = gate(top[0], top[1])


def _ffn_up(h16, wup16, cw, cb, state_tm, *, shift, seq_len):
    M = h16.shape[0]
    nf = D_FF // FFN_TF
    wa = pl.BlockSpec((D_MODEL, FFN_TF), lambda m, f: (0, f))
    wb = pl.BlockSpec((D_MODEL, FFN_TF), lambda m, f: (0, f + nf))
    cwa = pl.BlockSpec((FFN_CONV_WIDTH, FFN_TF), lambda m, f: (0, f))
    cwb = pl.BlockSpec((FFN_CONV_WIDTH, FFN_TF), lambda m, f: (0, f + nf))
    cba = pl.BlockSpec((1, FFN_TF), lambda m, f: (0, f))
    cbb = pl.BlockSpec((1, FFN_TF), lambda m, f: (0, f + nf))
    if shift == 1:
        tm = min(512, seq_len)
        nm = M // tm
        kern = functools.partial(_ffn_up_kernel, shift=1, tiles_per_seq=seq_len // tm, tm=tm)
        tail = pl.BlockSpec((1, FFN_CARRY, FFN_TF), lambda m, f: (m, 0, f))
        return pl.pallas_call(
            kern, grid=(nm, nf),
            in_specs=[pl.BlockSpec((tm, D_MODEL), lambda m, f: (m, 0)), wa, wb, cwa, cwb, cba, cbb],
            out_specs=[pl.BlockSpec((tm, FFN_TF), lambda m, f: (m, f)), tail, tail],
            out_shape=[jax.ShapeDtypeStruct((M, D_FF), BF16),
                       jax.ShapeDtypeStruct((nm, FFN_CARRY, D_FF), F32),
                       jax.ShapeDtypeStruct((nm, FFN_CARRY, D_FF), F32)],
            scratch_shapes=[pltpu.VMEM((nf, FFN_CARRY, FFN_TF), F32), pltpu.VMEM((nf, FFN_CARRY, FFN_TF), F32)],
            compiler_params=_cparams(("arbitrary", "arbitrary")),
            name="ffn_up_prompt",
        )(h16, wup16, wup16, cw, cw, cb, cb)
    tm = M
    kern = functools.partial(_ffn_up_kernel, shift=shift, tiles_per_seq=1, tm=tm)
    sa = pl.BlockSpec((2 * shift, FFN_TF), lambda m, f: (0, f))
    sb = pl.BlockSpec((2 * shift, FFN_TF), lambda m, f: (0, f + nf))
    tail = pl.BlockSpec((2 * shift, FFN_TF), lambda m, f: (0, f))
    return pl.pallas_call(
        kern, grid=(1, nf),
        in_specs=[pl.BlockSpec((tm, D_MODEL), lambda m, f: (0, 0)), wa, wb, cwa, cwb, cba, cbb, sa, sb],
        out_specs=[pl.BlockSpec((tm, FFN_TF), lambda m, f: (0, f)), tail, tail],
        out_shape=[jax.ShapeDtypeStruct((M, D_FF), BF16),
                   jax.ShapeDtypeStruct((2 * shift, D_FF), F32),
                   jax.ShapeDtypeStruct((2 * shift, D_FF), F32)],
        compiler_params=_cparams(("arbitrary", "arbitrary")),
        name="ffn_up_sample",
    )(h16, wup16, wup16, cw, cw, cb, cb, state_tm, state_tm)


def _ffn_down_kernel(g_ref, w_ref, h_ref, ln_g_ref, ln_b_ref, y32_ref, y16_ref, acc_ref):
    k = pl.program_id(1)
    last = pl.num_programs(1) - 1
    part = _dot(g_ref[...], w_ref[...])

    @pl.when(k == 0)
    def _():
        acc_ref[...] = part

    @pl.when((k > 0) & (k < last))
    def _():
        acc_ref[...] += part

    @pl.when(k == last)
    def _():
        y = _layer_norm(ALPHA * h_ref[...] + (acc_ref[...] + part), ln_g_ref[...], ln_b_ref[...])
        y32_ref[...] = y
        y16_ref[...] = y.astype(BF16)


def _ffn_down(g16, wdown16, h32, ln_g, ln_b, tm):
    M = h32.shape[0]
    tk = FFN_DOWN_TK
    vec = pl.BlockSpec((1, D_MODEL), lambda m, k: (0, 0))
    row = pl.BlockSpec((tm, D_MODEL), lambda m, k: (m, 0))
    return pl.pallas_call(
        _ffn_down_kernel,
        grid=(M // tm, D_FF // tk),
        in_specs=[pl.BlockSpec((tm, tk), lambda m, k: (m, k)),
                  pl.BlockSpec((tk, D_MODEL), lambda m, k: (k, 0)), row, vec, vec],
        out_specs=[row, row],
        out_shape=[jax.ShapeDtypeStruct((M, D_MODEL), F32), jax.ShapeDtypeStruct((M, D_MODEL), BF16)],
        scratch_shapes=[pltpu.VMEM((tm, D_MODEL), F32)],
        compiler_params=_cparams(("parallel", "arbitrary")),
        name="ffn_down_ln2",
    )(g16, wdown16, h32, ln_g, ln_b)


def _rope_tables(pos):
    half = HEAD_DIM // 2
    inv = ROPE_THETA ** (-jnp.arange(half, dtype=F32) / half)
    ang = pos.astype(F32)[:, None] * inv[None, :]
    cos, sin = jnp.cos(ang), jnp.sin(ang)
    return jnp.concatenate([cos, cos], -1), jnp.concatenate([-sin, sin], -1)


N_QKV = GROUP_WIDTH + 6 * HEAD_DIM
N_GATE = 3 * HEADS
N_REST = 2 * GROUP_WIDTH + 2 * GROUP_WIDTH + 3 * GROUP_WIDTH


W_IN_TILE = 256


def _w_in_rows_kernel(w_ref, o_ref, *, layer, depth):
    i = pl.program_id(0)
    n_kt = w_ref.shape[1] // depth
    row = lax.broadcasted_iota(jnp.int32, (W_IN_TILE, LANES), 0)
    keep = (i < pl.num_programs(0) - 1) | (row < N_GATE)
    for kt in range(n_kt):
        x = w_ref[:, kt * depth + layer, :]
        o_ref[:, kt * LANES:(kt + 1) * LANES] = jnp.where(keep, x, 0.0).astype(BF16)


def _w_in_rows(w_in, layer):
    depth, K, N = w_in.shape
    n_kt = K // LANES
    view = w_in.reshape(depth, n_kt, LANES, N).transpose(3, 1, 0, 2).reshape(N, n_kt * depth, LANES)
    n_rest, n_qkv = N_REST // W_IN_TILE, N_QKV // W_IN_TILE

    def src_row(i):
        return jnp.where(i < n_rest, N_QKV + N_GATE + i * W_IN_TILE,
                         jnp.where(i < n_rest + n_qkv, (i - n_rest) * W_IN_TILE, N_QKV))

    return pl.pallas_call(
        functools.partial(_w_in_rows_kernel, layer=layer, depth=depth),
        grid=(N_PROJ // W_IN_TILE,),
        in_specs=[pl.BlockSpec((pl.Element(W_IN_TILE), pl.Element(n_kt * depth), pl.Element(LANES)),
                               lambda i: (src_row(i), 0, 0))],
        out_specs=pl.BlockSpec((W_IN_TILE, K), lambda i: (i, 0)),
        out_shape=jax.ShapeDtypeStruct((N_PROJ, K), BF16),
        compiler_params=_cparams(("parallel",)),
        name="w_in_rows",
    )(view)


def _row2(v):
    return v.reshape(1, -1)


def _layer_prompt(x32, x16, lw, B, T):
    M = B * T
    proj32, proj16 = _proj(x16, lw["w_in"], min(1024, M))
    pos = jnp.tile(jnp.arange(T), B)
    cos2, sin2 = _rope_tables(pos)
    tm_r = min(256, T)
    pe_t = jnp.tile(lw["nsa_pe"], (1, tm_r // NSA_BLOCK, 1))
    q16, kv32, kv16, xkv = _rope(proj32, cos2, sin2, pe_t, tm_r)

    n_blk = T // NSA_BLOCK
    cmp = _compress(xkv.reshape(2, B * n_blk, NSA_BLOCK * HEAD_DIM), lw["nsa_cw1"], lw["nsa_cw2"],
                    min(128, B * n_blk))
    kcmp = cmp[0].reshape(B, n_blk, HEAD_DIM)
    vcmp = cmp[1].reshape(B, n_blk, HEAD_DIM)
    o_nsa = _nsa_prompt(q16, kv16, kcmp, vcmp, proj32, B, T)

    o_conv, conv_tail = _conv_prompt(proj32, lw["conv_w"], lw["conv_b"], lw["conv_ln_g"], lw["conv_ln_b"], B, T)
    o_gm = _gm_prompt(proj32, lw["gm_ws"], lw["gm_bias_t"], lw["gm_ln_g"], lw["gm_ln_b"], B, T)
    o_sb = _sb_prompt(proj16, B, T)

    h32, h16 = _outproj((o_nsa, o_conv, o_gm, o_sb), lw["w_out"], x32, lw["ln1_g"], lw["ln1_b"], min(512, M))
    g16, tail_a, tail_b = _ffn_up(h16, lw["ffn_up"], lw["ffn_conv_w"], lw["ffn_conv_b"], None, shift=1, seq_len=T)
    y32, y16 = _ffn_down(g16, lw["ffn_down"], h32, lw["ln2_g"], lw["ln2_b"], min(512, M))

    nsa_rows = kv32[:, :4 * HEAD_DIM].reshape(B, T, 4, HEAD_DIM)
    wlen = min(NSA_WINDOW, T)
    win_rows = kv32[:, 4 * HEAD_DIM:].reshape(B, T, 2, HEAD_DIM)[:, T - wlen:]
    sb_rows = proj32[:, C_SK:C_SK + 2 * GROUP_WIDTH].reshape(B, T, 2, HEADS, HEAD_DIM)
    conv_rows = conv_tail[:, CONV_HIST - (CONV_WIDTH - 1):]
    tiles_per_seq = tail_a.shape[0] // B
    ffn_tail = jnp.concatenate([tail_a, tail_b], -1)[tiles_per_seq - 1::tiles_per_seq]
    ffn_rows = ffn_tail[:, FFN_CARRY - (FFN_CONV_WIDTH - 1):]
    return y32, y16, (nsa_rows, sb_rows, win_rows, conv_rows, ffn_rows)


def _layer_sample(x32, x16, lw, layer, B, T, cache_nsa, cache_sb, win_all, conv_state, ffn_state, page_table):
    M = B * T
    n_pages = page_table.shape[1]
    P = n_pages * PAGE_SIZE
    proj32, proj16 = _proj(x16, lw["w_in"], M)
    pos = jnp.repeat(P + jnp.arange(T), B)
    cos2, sin2 = _rope_tables(pos)
    tm_r = min(256, M)
    pe_t = jnp.tile(lw["nsa_pe"], (1, tm_r // NSA_BLOCK, 1))
    q16, kv32, kv16, _ = _rope(proj32, cos2, sin2, pe_t, tm_r)

    to_b = lambda a: a.reshape(T, B, -1).swapaxes(0, 1)
    kv32_b = to_b(kv32)

    xkv, ksv = _gather_nsa(cache_nsa, layer, page_table, lw["nsa_pe"])
    n_past = P // NSA_BLOCK
    cmp_past = _compress_rows(xkv, lw["nsa_cw1"], lw["nsa_cw2"], min(256, B * n_past)).reshape(2, B, n_past, HEAD_DIM)
    n_blk = -(-(P + T) // NSA_BLOCK)
    n_new = n_blk - n_past
    new_rows = jnp.pad(kv32_b[:, :, :2 * HEAD_DIM], ((0, 0), (0, n_new * NSA_BLOCK - T), (0, 0)))
    new_rows = new_rows.reshape(B, n_new, NSA_BLOCK, 2, HEAD_DIM)
    x_new = jnp.stack([new_rows[:, :, :, 0] + lw["nsa_pe"][0], new_rows[:, :, :, 1] + lw["nsa_pe"][1]])
    x_new = x_new.reshape(2, B * n_new, NSA_BLOCK * HEAD_DIM).astype(BF16)
    cmp_new = _compress(x_new, lw["nsa_cw1"], lw["nsa_cw2"], B * n_new).reshape(2, B, n_new, HEAD_DIM)
    n_pad = -(-n_blk // SUBLANES) * SUBLANES
    cmp = jnp.concatenate([cmp_past, cmp_new, jnp.zeros((2, B, n_pad - n_blk, HEAD_DIM), F32)], axis=2)

    g32_b = to_b(proj32[:, C_G:C_G + LANES])
    o_nsa = _nsa_sample(to_b(q16).astype(F32), to_b(kv16).astype(F32), win_all, layer, cmp[0], cmp[1], g32_b, ksv)
    o_sb = _sb_sample(to_b(proj16[:, C_SQ:C_SQ + 3 * GROUP_WIDTH]).astype(F32), cache_sb, layer, page_table)
    to_t = lambda a: a.swapaxes(0, 1).reshape(M, -1).astype(BF16)
    o_nsa, o_sb = to_t(o_nsa), to_t(o_sb)

    proj32_3d = proj32.reshape(T, B, N_PROJ)
    o_conv, glu = _conv_sample(conv_state.swapaxes(0, 1), proj32_3d, lw["conv_w"], lw["conv_b"],
                               lw["conv_ln_g"], lw["conv_ln_b"])
    o_gm, vn = _gm_sample(proj32_3d, lw["gm_wexp"], lw["gm_bexp"], lw["gm_ln_g"], lw["gm_ln_b"])

    mix = (o_nsa, o_conv.reshape(M, GROUP_WIDTH), o_gm.reshape(M, GROUP_WIDTH), o_sb)
    h32, h16 = _outproj(mix, lw["w_out"], x32, lw["ln1_g"], lw["ln1_b"], min(512, M))
    ffn_state_tm = ffn_state.swapaxes(0, 1).reshape(2 * B, 2 * D_FF)
    g16, tail_a, tail_b = _ffn_up(h16, lw["ffn_up"], lw["ffn_conv_w"], lw["ffn_conv_b"], ffn_state_tm,
                                  shift=B, seq_len=T)
    y32, y16 = _ffn_down(g16, lw["ffn_down"], h32, lw["ln2_g"], lw["ln2_b"], min(512, M))

    nsa_rows = kv32_b[:, :, :4 * HEAD_DIM].reshape(B, T, 4, HEAD_DIM)
    win_rows = kv32_b[:, :, 4 * HEAD_DIM:].reshape(B, T, 2, HEAD_DIM)
    sb_rows = to_b(proj32[:, C_SK:C_SK + 2 * GROUP_WIDTH]).reshape(B, T, 2, HEADS, HEAD_DIM)
    ccat = jnp.concatenate([conv_state, glu.swapaxes(0, 1)], 1)
    conv_rows = ccat[:, ccat.shape[1] - (CONV_WIDTH - 1):]
    ffn_rows = jnp.concatenate([tail_a, tail_b], -1).reshape(2, B, 2 * D_FF).swapaxes(0, 1)
    gm_v = vn.swapaxes(0, 1)
    return y32, y16, (nsa_rows, sb_rows, win_rows, conv_rows, ffn_rows, gm_v)


def _layer_weights(l, T_s, w_in, nsa_pe, nsa_cw1, nsa_cw2, conv_w, conv_b, conv_ln_g, conv_ln_b,
                   gm_ln_g, gm_ln_b, gm_ws, gm_bias, w_out, ln1_g, ln1_b,
                   ffn_up, ffn_conv_w, ffn_conv_b, ffn_down, ln2_g, ln2_b):
    cs = min(GM_CHUNK, T_s)
    ws_s = jnp.tril(gm_ws[l][:, :cs, :cs])
    wexp = jnp.repeat(ws_s.transpose(1, 2, 0), HEAD_DIM, axis=2)
    bexp = jnp.repeat(gm_bias[l][:, :cs].T, HEAD_DIM, axis=1)
    return {
        "w_in": _w_in_rows(w_in, l),
        "nsa_pe": nsa_pe[l],
        "nsa_cw1": nsa_cw1[l].astype(BF16), "nsa_cw2": nsa_cw2[l].astype(BF16),
        "conv_w": conv_w[l], "conv_b": _row2(conv_b[l]),
        "conv_ln_g": _row2(conv_ln_g[l]), "conv_ln_b": _row2(conv_ln_b[l]),
        "gm_ln_g": _row2(gm_ln_g[l]), "gm_ln_b": _row2(gm_ln_b[l]),
        "gm_ws": gm_ws[l], "gm_bias_t": gm_bias[l].T, "gm_wexp": wexp, "gm_bexp": bexp,
        "w_out": w_out[l].astype(BF16), "ln1_g": _row2(ln1_g[l]), "ln1_b": _row2(ln1_b[l]),
        "ffn_up": ffn_up[l].astype(BF16), "ffn_conv_w": ffn_conv_w[l], "ffn_conv_b": _row2(ffn_conv_b[l]),
        "ffn_down": ffn_down[l].astype(BF16), "ln2_g": _row2(ln2_g[l]), "ln2_b": _row2(ln2_b[l]),
    }


def kernel(x_prompt, x_sample, cache_nsa_kv, cache_sb_kv, state_nsa_win, state_conv, state_ffn, page_table,
           w_in, nsa_pe, nsa_cw1, nsa_cw2, conv_w, conv_b, conv_ln_g, conv_ln_b, gm_ln_g, gm_ln_b, gm_ws, gm_bias,
           w_out, ln1_g, ln1_b, ffn_up, ffn_conv_w, ffn_conv_b, ffn_down, ln2_g, ln2_b):
    Bp, Tp, _ = x_prompt.shape
    Bs, Ts, _ = x_sample.shape
    depth = w_in.shape[0]
    cache_nsa = cache_nsa_kv.reshape(cache_nsa_kv.shape[:2] + (-1, HEAD_DIM))
    cache_sb = cache_sb_kv.reshape(cache_sb_kv.shape[:2] + (-1, HEAD_DIM))
    win_all = state_nsa_win.reshape(state_nsa_win.shape[:2] + (-1, HEAD_DIM))

    xp32 = x_prompt.reshape(Bp * Tp, D_MODEL)
    xs32 = x_sample.swapaxes(0, 1).reshape(Ts * Bs, D_MODEL)
    xp16, xs16 = xp32.astype(BF16), xs32.astype(BF16)
    outs_p, outs_s = [], []
    for l in range(depth):
        lw = _layer_weights(l, Ts, w_in, nsa_pe, nsa_cw1, nsa_cw2, conv_w, conv_b, conv_ln_g, conv_ln_b,
                            gm_ln_g, gm_ln_b, gm_ws, gm_bias, w_out, ln1_g, ln1_b,
                            ffn_up, ffn_conv_w, ffn_conv_b, ffn_down, ln2_g, ln2_b)
        xp32, xp16, sp = _layer_prompt(xp32, xp16, lw, Bp, Tp)
        xs32, xs16, ss = _layer_sample(xs32, xs16, lw, l, Bs, Ts, cache_nsa, cache_sb, win_all,
                                       state_conv[l], state_ffn[l], page_table)
        outs_p.append(sp)
        outs_s.append(ss)
    y_p = xp32.reshape(Bp, Tp, D_MODEL)
    y_s = xs32.reshape(Ts, Bs, D_MODEL).swapaxes(0, 1)
    st = lambda outs, i: jnp.stack([o[i] for o in outs])
    w_old = state_nsa_win.shape[2]
    keep = min(NSA_WINDOW, page_table.shape[1] * PAGE_SIZE + Ts)
    win_s = jnp.concatenate([state_nsa_win[:, :, w_old + Ts - keep:], st(outs_s, 2)], axis=2)
    return (y_p, y_s, st(outs_p, 0), st(outs_s, 0), st(outs_p, 1), st(outs_s, 1), st(outs_p, 2), win_s,
            st(outs_p, 3), st(outs_s, 3), st(outs_p, 4), st(outs_s, 4), st(outs_s, 5))
```

```python
import functools
import math

import jax
import jax.numpy as jnp
import numpy as np
from jax import lax
from jax.experimental import pallas as pl
from jax.experimental.pallas import tpu as pltpu

F32 = jnp.float32
BF16 = jnp.bfloat16

D_MODEL = 2048
HEAD_DIM = 128
HEADS = 4
GROUP_WIDTH = HEADS * HEAD_DIM
NSA_BLOCK = 64
NSA_TOP_N = 16
NSA_WINDOW = 512
CONV_WIDTH = 31
GM_CHUNK = 128
D_FF = (11 * D_MODEL) // 4
FFN_CONV_WIDTH = 3
ROPE_THETA = 10000.0
LN_EPS = 1e-5
DEPTH = 2
ALPHA = (2 * DEPTH) ** 0.25
PAGE_SIZE = 128
SCALE = HEAD_DIM ** -0.5

VMEM_LIMIT_V7X = 56 * 1024 * 1024
SUBLANES = 8
LANES = 128

C_CONV = 0
C_GM = C_CONV + 2 * GROUP_WIDTH
C_SQ = C_GM + 2 * GROUP_WIDTH
C_SK = C_SQ + GROUP_WIDTH
C_SV = C_SK + GROUP_WIDTH
C_Q = C_SV + GROUP_WIDTH
C_KV = C_Q + GROUP_WIDTH
C_G = C_KV + 6 * HEAD_DIM
N_PROJ = 5120
PROJ_TN = 512


def _cparams(sem):
    return pltpu.CompilerParams(dimension_semantics=sem, vmem_limit_bytes=VMEM_LIMIT_V7X)


def _layer_norm(x, g, b):
    mu = jnp.mean(x, -1, keepdims=True)
    xc = x - mu
    var = jnp.mean(xc * xc, -1, keepdims=True)
    return xc * lax.rsqrt(var + LN_EPS) * g + b


def _masked_softmax(s, mask):
    sm = jnp.where(mask, s, -1e30)
    m = jnp.max(sm, -1, keepdims=True)
    e = jnp.where(mask, jnp.exp(sm - m), 0.0)
    return e / jnp.maximum(jnp.sum(e, -1, keepdims=True), jnp.finfo(F32).tiny)


def _dot_t(a, b):
    return lax.dot_general(a, b, (((1,), (1,)), ((), ())), preferred_element_type=F32)


def _dot(a, b):
    return jnp.dot(a, b, preferred_element_type=F32)


def _split_dot(x, u16):
    hi = x.astype(BF16)
    lo = (x - hi.astype(F32)).astype(BF16)
    return _dot(hi, u16) + _dot(lo, u16)


def _proj_kernel(x_ref, wt_ref, o32_ref, o16_ref, sb_rows_ref, *, tm):
    j = pl.program_id(1)
    acc = _dot_t(x_ref[...], wt_ref[...])
    o32_ref[...] = acc
    o16_ref[...] = acc.astype(BF16)
    for kv, tile in enumerate((C_SK // PROJ_TN, C_SV // PROJ_TN)):
        @pl.when(j == tile)
        def _():
            for h in range(HEADS):
                sb_rows_ref[pl.ds(kv * HEADS + h, tm, stride=2 * HEADS), :] = acc[:, h * HEAD_DIM:(h + 1) * HEAD_DIM]


def _proj(x16, wt16, tm):
    M, K = x16.shape
    N = wt16.shape[0]
    return pl.pallas_call(
        functools.partial(_proj_kernel, tm=tm),
        grid=(M // tm, N // PROJ_TN),
        in_specs=[pl.BlockSpec((tm, K), lambda i, j: (i, 0)),
                  pl.BlockSpec((PROJ_TN, K), lambda i, j: (j, 0))],
        out_specs=[pl.BlockSpec((tm, PROJ_TN), lambda i, j: (i, j)),
                   pl.BlockSpec((tm, PROJ_TN), lambda i, j: (i, j)),
                   pl.BlockSpec((tm * 2 * HEADS, HEAD_DIM), lambda i, j: (i, 0))],
        out_shape=[jax.ShapeDtypeStruct((M, N), F32), jax.ShapeDtypeStruct((M, N), BF16),
                   jax.ShapeDtypeStruct((M * 2 * HEADS, HEAD_DIM), F32)],
        compiler_params=_cparams(("parallel", "arbitrary")),
        name="proj",
    )(x16, wt16)


def _rope_kernel(q_ref, kva_ref, kvb_ref, kvc_ref, cos_ref, sin_ref, pe_ref,
                 q16_ref, kv32_ref, kv16_ref, xkv_ref, rows4_ref, rows2_ref):
    tm = q_ref.shape[0]
    cos = cos_ref[...]
    sin = sin_ref[...]

    def rot(x):
        return x * cos + pltpu.roll(x, HEAD_DIM // 2, axis=1) * sin

    for h in range(HEADS):
        sl = slice(h * HEAD_DIM, (h + 1) * HEAD_DIM)
        q16_ref[:, sl] = rot(q_ref[:, sl]).astype(BF16)
    for p, ref in enumerate((kva_ref, kvb_ref, kvc_ref)):
        k = rot(ref[:, :HEAD_DIM])
        v = ref[:, HEAD_DIM:]
        ks = slice(2 * p * HEAD_DIM, (2 * p + 1) * HEAD_DIM)
        vs = slice((2 * p + 1) * HEAD_DIM, (2 * p + 2) * HEAD_DIM)
        kv32_ref[:, ks] = k
        kv32_ref[:, vs] = v
        kv16_ref[:, ks] = k.astype(BF16)
        kv16_ref[:, vs] = v.astype(BF16)
        if p < 2:
            rows4_ref[pl.ds(2 * p, tm, stride=4), :] = k
            rows4_ref[pl.ds(2 * p + 1, tm, stride=4), :] = v
        else:
            rows2_ref[pl.ds(0, tm, stride=2), :] = k
            rows2_ref[pl.ds(1, tm, stride=2), :] = v
        if p == 0:
            xkv_ref[0] = (k + pe_ref[0]).astype(BF16)
            xkv_ref[1] = (v + pe_ref[1]).astype(BF16)


def _rope(proj32, cos2, sin2, pe_t, tm):
    M = proj32.shape[0]
    qb = C_Q // GROUP_WIDTH
    kb = C_KV // (2 * HEAD_DIM)
    row = lambda i: (i, 0)
    return pl.pallas_call(
        _rope_kernel,
        grid=(M // tm,),
        in_specs=[pl.BlockSpec((tm, GROUP_WIDTH), lambda i: (i, qb)),
                  pl.BlockSpec((tm, 2 * HEAD_DIM), lambda i: (i, kb)),
                  pl.BlockSpec((tm, 2 * HEAD_DIM), lambda i: (i, kb + 1)),
                  pl.BlockSpec((tm, 2 * HEAD_DIM), lambda i: (i, kb + 2)),
                  pl.BlockSpec((tm, HEAD_DIM), row),
                  pl.BlockSpec((tm, HEAD_DIM), row),
                  pl.BlockSpec((2, tm, HEAD_DIM), lambda i: (0, 0, 0))],
        out_specs=[pl.BlockSpec((tm, GROUP_WIDTH), row),
                   pl.BlockSpec((tm, 6 * HEAD_DIM), row),
                   pl.BlockSpec((tm, 6 * HEAD_DIM), row),
                   pl.BlockSpec((2, tm, HEAD_DIM), lambda i: (0, i, 0)),
                   pl.BlockSpec((4 * tm, HEAD_DIM), row),
                   pl.BlockSpec((2 * tm, HEAD_DIM), row)],
        out_shape=[jax.ShapeDtypeStruct((M, GROUP_WIDTH), BF16),
                   jax.ShapeDtypeStruct((M, 6 * HEAD_DIM), F32),
                   jax.ShapeDtypeStruct((M, 6 * HEAD_DIM), BF16),
                   jax.ShapeDtypeStruct((2, M, HEAD_DIM), BF16),
                   jax.ShapeDtypeStruct((4 * M, HEAD_DIM), F32),
                   jax.ShapeDtypeStruct((2 * M, HEAD_DIM), F32)],
        compiler_params=_cparams(("parallel",)),
        name="rope",
    )(proj32, proj32, proj32, proj32, cos2, sin2, pe_t)


def _gelu_tanh(x):
    return x * (0.5 * (1.0 + jnp.tanh(math.sqrt(2.0 / math.pi) * (x + 0.044715 * (x * x * x)))))


def _cmp_kernel(x_ref, w1_ref, w2_ref, o_ref):
    h = _gelu_tanh(_dot(x_ref[0], w1_ref[0]))
    o_ref[0] = _dot(h.astype(BF16), w2_ref[0])


def _compress(x16, w1, w2, tm):
    _, R, K = x16.shape
    d = w1.shape[2]
    return pl.pallas_call(
        _cmp_kernel,
        grid=(2, R // tm),
        in_specs=[pl.BlockSpec((1, tm, K), lambda s, i: (s, i, 0)),
                  pl.BlockSpec((1, K, d), lambda s, i: (s, 0, 0)),
                  pl.BlockSpec((1, d, d), lambda s, i: (s, 0, 0))],
        out_specs=pl.BlockSpec((1, tm, d), lambda s, i: (s, i, 0)),
        out_shape=jax.ShapeDtypeStruct((2, R, d), F32),
        compiler_params=_cparams(("parallel", "parallel")),
        name="nsa_compress",
    )(x16, w1, w2)


CMP_PITCH = NSA_BLOCK + SUBLANES


def _cmp_rows_kernel(x_ref, w1_ref, w2_ref, o_ref, *, tm):
    acc = jnp.zeros((tm, HEAD_DIM), F32)
    for p in range(0, NSA_BLOCK, 2):
        lhs = jnp.concatenate([x_ref[0, pl.ds(p, tm, stride=CMP_PITCH), :],
                               x_ref[0, pl.ds(p + 1, tm, stride=CMP_PITCH), :]], axis=1).astype(BF16)
        acc = acc + _dot(lhs, w1_ref[0, p * HEAD_DIM:(p + 2) * HEAD_DIM, :])
    o_ref[0] = _dot(_gelu_tanh(acc).astype(BF16), w2_ref[0])


def _compress_rows(x32, w1, w2, tm):
    R = x32.shape[1] // CMP_PITCH
    K, d = w1.shape[1:]
    return pl.pallas_call(
        functools.partial(_cmp_rows_kernel, tm=tm),
        grid=(2, R // tm),
        in_specs=[pl.BlockSpec((1, tm * CMP_PITCH, d), lambda s, i: (s, i, 0)),
                  pl.BlockSpec((1, K, d), lambda s, i: (s, 0, 0)),
                  pl.BlockSpec((1, d, d), lambda s, i: (s, 0, 0))],
        out_specs=pl.BlockSpec((1, tm, d), lambda s, i: (s, i, 0)),
        out_shape=jax.ShapeDtypeStruct((2, R, d), F32),
        compiler_params=_cparams(("parallel", "parallel")),
        name="nsa_compress_rows",
    )(x32, w1, w2)


def _select_blocks(imp, tpos, n_blk, n_real):
    nidx = lax.broadcasted_iota(jnp.int32, imp.shape, 1)
    cur = tpos // NSA_BLOCK
    forced = (nidx == 0) | (nidx == cur) | (nidx == cur - 1)
    started = nidx * NSA_BLOCK <= tpos
    score = jnp.where(forced, jnp.inf, jnp.where(started, imp, -jnp.inf))
    score = jnp.where(nidx < n_real, score, -jnp.inf)
    rank = jnp.zeros(imp.shape, jnp.int32)
    for n2 in range(n_real):
        c = score[:, n2:n2 + 1]
        ahead = (c > score) | ((c == score) & (nidx > n2))
        rank = rank + ahead.astype(jnp.int32)
    return (rank < min(NSA_TOP_N, n_real)) & (nidx < n_real)


def _gate_cols(gs, h):
    return [gs[:, 3 * h + c:3 * h + c + 1] for c in range(3)]


MASKED = -1e30


def _nsa_prompt_kernel(q_ref, ks_ref, vs_ref, kw_ref, vw_ref, kc_ref, vc_ref, g_ref, o_ref, selk_ref, *, T, TQ, KC):
    i = pl.program_id(1)
    n_blk = T // NSA_BLOCK
    R = HEADS * TQ
    t0 = i * TQ
    tpos = t0 + lax.broadcasted_iota(jnp.int32, (TQ, 1), 0)
    tpos_r = t0 + lax.broadcasted_iota(jnp.int32, (R, 1), 0) % TQ
    per_head = lambda x: jnp.concatenate([x] * HEADS, axis=0)
    q = jnp.concatenate([q_ref[:, h * HEAD_DIM:(h + 1) * HEAD_DIM] for h in range(HEADS)], axis=0)
    gs = jax.nn.sigmoid(g_ref[...])

    kc = kc_ref[0].astype(BF16)
    vc = vc_ref[0].astype(BF16)
    cmask = (lax.broadcasted_iota(jnp.int32, (R, n_blk), 1) + 1) * NSA_BLOCK - 1 <= tpos_r
    p_c = _masked_softmax(_dot_t(q, kc) * SCALE, cmask)
    o_cmp = _dot(p_c.astype(BF16), vc)
    imp = p_c[0:TQ]
    for h in range(1, HEADS):
        imp = imp + p_c[h * TQ:(h + 1) * TQ]

    sel = _select_blocks(imp, tpos, n_blk, n_blk)
    expand = (lax.broadcasted_iota(jnp.int32, (n_blk, T), 1) // NSA_BLOCK
              == lax.broadcasted_iota(jnp.int32, (n_blk, T), 0))
    selk_ref[...] = _dot(jnp.where(sel, 1.0, 0.0).astype(BF16), jnp.where(expand, 1.0, 0.0).astype(BF16))

    lane = lax.broadcasted_iota(jnp.int32, (TQ, KC), 1)

    def body(c, carry):
        m, l, acc = carry
        k0 = pl.multiple_of(c * KC, KC)
        keep = (selk_ref[:, pl.ds(k0, KC)] > 0.5) & (k0 + lane <= tpos)
        s = _dot_t(q, ks_ref[pl.ds(k0, KC), :]) * SCALE + per_head(jnp.where(keep, 0.0, MASKED))
        m_new = jnp.maximum(m, jnp.max(s, -1, keepdims=True))
        scale_old = jnp.exp(m - m_new)
        p = jnp.exp(s - m_new)
        l = scale_old * l + jnp.sum(p, -1, keepdims=True)
        acc = scale_old * acc + _dot(p.astype(BF16), vs_ref[pl.ds(k0, KC), :])
        return m_new, l, acc

    n_chunks = (t0 + TQ + KC - 1) // KC
    _, l, acc = lax.fori_loop(0, n_chunks, body, (jnp.full((R, 1), MASKED, F32), jnp.zeros((R, 1), F32),
                                                  jnp.zeros((R, HEAD_DIM), F32)))
    o_sel = acc / l

    band = min(NSA_WINDOW + TQ, T)
    w0 = pl.multiple_of(jnp.clip(t0 - NSA_WINDOW, 0, T - band), LANES)
    wpos = w0 + lax.broadcasted_iota(jnp.int32, (TQ, band), 1)
    keep = (wpos <= tpos) & (wpos > tpos - NSA_WINDOW)
    s = _dot_t(q, kw_ref[pl.ds(w0, band), :]) * SCALE + per_head(jnp.where(keep, 0.0, MASKED))
    p = jnp.exp(s - jnp.max(s, -1, keepdims=True))
    o_win = _dot(p.astype(BF16), vw_ref[pl.ds(w0, band), :]) / jnp.sum(p, -1, keepdims=True)

    for h in range(HEADS):
        g0, g1, g2 = _gate_cols(gs, h)
        rs = slice(h * TQ, (h + 1) * TQ)
        o = g0 * o_cmp[rs] + g1 * o_sel[rs] + g2 * o_win[rs]
        o_ref[:, h * HEAD_DIM:(h + 1) * HEAD_DIM] = o.astype(BF16)


def _nsa_prompt(q16, kv16, kcmp, vcmp, proj32, B, T):
    TQ = min(128, T)
    KC = min(256, T)
    nq = T // TQ
    n_blk = T // NSA_BLOCK
    kvcol = lambda c: pl.BlockSpec((T, HEAD_DIM), lambda b, i: (b, c))
    return pl.pallas_call(
        functools.partial(_nsa_prompt_kernel, T=T, TQ=TQ, KC=KC),
        grid=(B, nq),
        scratch_shapes=[pltpu.VMEM((TQ, T), F32)],
        in_specs=[pl.BlockSpec((TQ, GROUP_WIDTH), lambda b, i: (b * nq + i, 0)),
                  kvcol(2), kvcol(3), kvcol(4), kvcol(5),
                  pl.BlockSpec((1, n_blk, HEAD_DIM), lambda b, i: (b, 0, 0)),
                  pl.BlockSpec((1, n_blk, HEAD_DIM), lambda b, i: (b, 0, 0)),
                  pl.BlockSpec((TQ, LANES), lambda b, i: (b * nq + i, C_G // LANES))],
        out_specs=pl.BlockSpec((TQ, GROUP_WIDTH), lambda b, i: (b * nq + i, 0)),
        out_shape=jax.ShapeDtypeStruct((B * T, GROUP_WIDTH), BF16),
        compiler_params=_cparams(("parallel", "arbitrary")),
        name="nsa_prompt",
    )(q16, kv16, kv16, kv16, kv16, kcmp, vcmp, proj32)


def _rows(ref, r, n, stride):
    return ref[0, 0, pl.ds(r, n, stride=stride), :]


def _gather_nsa_kernel(pt_ref, pe_ref, *refs, n_pages):
    pages = refs[:n_pages]
    xkv_ref, ksv_ref = refs[n_pages:]
    blocks_per_page = PAGE_SIZE // NSA_BLOCK
    pad = jnp.zeros((CMP_PITCH - NSA_BLOCK, HEAD_DIM), F32)
    for j in range(n_pages):
        sl = slice(j * PAGE_SIZE, (j + 1) * PAGE_SIZE)
        ksv_ref[0, 0, sl, :] = _rows(pages[j], 2, PAGE_SIZE, 4).astype(BF16)
        ksv_ref[1, 0, sl, :] = _rows(pages[j], 3, PAGE_SIZE, 4).astype(BF16)
        for s in range(2):
            x = _rows(pages[j], s, PAGE_SIZE, 4)
            for u in range(blocks_per_page):
                r0 = (j * blocks_per_page + u) * CMP_PITCH
                xkv_ref[s, r0:r0 + NSA_BLOCK, :] = x[u * NSA_BLOCK:(u + 1) * NSA_BLOCK] + pe_ref[s]
                xkv_ref[s, r0 + NSA_BLOCK:r0 + CMP_PITCH, :] = pad


def _gather_nsa(cache, layer, page_table, pe):
    B, n_pages = page_table.shape
    P = n_pages * PAGE_SIZE
    rows_b = (P // NSA_BLOCK) * CMP_PITCH

    def page_spec(j):
        return pl.BlockSpec((1, 1, PAGE_SIZE * 4, HEAD_DIM), lambda b, pt: (layer, pt[b, j], 0, 0))

    out = pl.BlockSpec((2, 1, P, HEAD_DIM), lambda b, pt: (0, b, 0, 0))
    return pl.pallas_call(
        functools.partial(_gather_nsa_kernel, n_pages=n_pages),
        grid_spec=pltpu.PrefetchScalarGridSpec(
            num_scalar_prefetch=1, grid=(B,),
            in_specs=[pl.BlockSpec((2, NSA_BLOCK, HEAD_DIM), lambda b, pt: (0, 0, 0))]
            + [page_spec(j) for j in range(n_pages)],
            out_specs=[pl.BlockSpec((2, rows_b, HEAD_DIM), lambda b, pt: (0, b, 0)), out]),
        out_shape=[jax.ShapeDtypeStruct((2, B * rows_b, HEAD_DIM), F32),
                   jax.ShapeDtypeStruct((2, B, P, HEAD_DIM), BF16)],
        compiler_params=_cparams(("arbitrary",)),
        name="nsa_gather",
    )(page_table, pe, *([cache] * n_pages))


def _nsa_sample_kernel(q_ref, kvn_ref, win_ref, kc_ref, vc_ref, g_ref, ks_ref, vs_ref, o_ref, *, T, P, n_blk, n_pad):
    R = HEADS * T
    row = lax.broadcasted_iota(jnp.int32, (R, 1), 0)
    tq = row % T
    tpos = P + tq
    q_bt = q_ref[0]
    q = jnp.concatenate([q_bt[:, h * HEAD_DIM:(h + 1) * HEAD_DIM] for h in range(HEADS)], axis=0).astype(BF16)
    gs = jax.nn.sigmoid(g_ref[0])

    kc = kc_ref[0].astype(BF16)
    vc = vc_ref[0].astype(BF16)
    nidx = lax.broadcasted_iota(jnp.int32, (R, n_pad), 1)
    cmask = ((nidx + 1) * NSA_BLOCK - 1 <= tpos) & (nidx < n_blk)
    p_c = _masked_softmax(_dot_t(q, kc) * SCALE, cmask)
    o_cmp = _dot(p_c.astype(BF16), vc)
    imp = p_c[0:T]
    for h in range(1, HEADS):
        imp = imp + p_c[h * T:(h + 1) * T]
    sel_t = _select_blocks(imp, P + lax.broadcasted_iota(jnp.int32, (T, 1), 0), n_pad, n_blk)
    sel = jnp.concatenate([jnp.where(sel_t, 1.0, 0.0)] * HEADS, axis=0)

    lane = lax.broadcasted_iota(jnp.int32, (R, PAGE_SIZE), 1)
    kvn = kvn_ref[0]
    pad_rows = jnp.zeros((PAGE_SIZE - T, HEAD_DIM), F32)
    new_chunk = lambda c: jnp.concatenate([kvn[:, c * HEAD_DIM:(c + 1) * HEAD_DIM], pad_rows], axis=0).astype(BF16)
    ks_new, vs_new, kw_new, vw_new = new_chunk(2), new_chunk(3), new_chunk(4), new_chunk(5)
    expand = (lax.broadcasted_iota(jnp.int32, (n_pad, P), 1) // NSA_BLOCK
              == lax.broadcasted_iota(jnp.int32, (n_pad, P), 0))
    selk = _dot(sel.astype(BF16), jnp.where(expand, 1.0, 0.0).astype(BF16))
    m_past = (selk > 0.5) & (lax.broadcasted_iota(jnp.int32, (R, P), 1) <= tpos)
    s_past = _dot_t(q, ks_ref[0, 0]) * SCALE
    n_past = P // NSA_BLOCK
    chosen = jnp.zeros((R, PAGE_SIZE), F32)
    for u in range(PAGE_SIZE // NSA_BLOCK):
        if n_past + u < n_blk:
            in_blk = (lane >= u * NSA_BLOCK) & (lane < (u + 1) * NSA_BLOCK)
            chosen = jnp.where(in_blk, sel[:, n_past + u:n_past + u + 1], chosen)
    m_new = (chosen > 0.5) & (P + lane <= tpos)
    s_new = _dot_t(q, ks_new) * SCALE
    o_sel = _multi_chunk_attention([s_past, s_new], [m_past, m_new], [vs_ref[0, 0], vs_new])

    W = win_ref.shape[2] // 2
    kw = _rows(win_ref, 0, W, 2).astype(BF16)
    vw = _rows(win_ref, 1, W, 2).astype(BF16)
    wpos = P - W + lax.broadcasted_iota(jnp.int32, (R, W), 1)
    m_w = (wpos <= tpos) & (wpos > tpos - NSA_WINDOW) & (wpos >= 0)
    m_wn = (P + lane <= tpos) & (P + lane > tpos - NSA_WINDOW)
    o_win = _multi_chunk_attention([_dot_t(q, kw) * SCALE, _dot_t(q, kw_new) * SCALE], [m_w, m_wn], [vw, vw_new])

    for h in range(HEADS):
        g0, g1, g2 = _gate_cols(gs, h)
        rs = slice(h * T, (h + 1) * T)
        o_ref[0, :, h * HEAD_DIM:(h + 1) * HEAD_DIM] = g0 * o_cmp[rs] + g1 * o_sel[rs] + g2 * o_win[rs]


def _multi_chunk_attention(s_list, m_list, v_list):
    sm = [jnp.where(m, s, -1e30) for s, m in zip(s_list, m_list)]
    mx = jnp.max(sm[0], -1, keepdims=True)
    for s in sm[1:]:
        mx = jnp.maximum(mx, jnp.max(s, -1, keepdims=True))
    es = [jnp.where(m, jnp.exp(s - mx), 0.0) for s, m in zip(sm, m_list)]
    tot = jnp.sum(es[0], -1, keepdims=True)
    for e in es[1:]:
        tot = tot + jnp.sum(e, -1, keepdims=True)
    den = jnp.maximum(tot, jnp.finfo(F32).tiny)
    acc = _dot(es[0].astype(BF16), v_list[0])
    for e, v in zip(es[1:], v_list[1:]):
        acc = acc + _dot(e.astype(BF16), v)
    return acc / den


def _nsa_sample(q_b, kv_b, win_all, layer, kcmp, vcmp, g32_b, ksv):
    B, T, _ = q_b.shape
    P = ksv.shape[2]
    n_blk = -(-(P + T) // NSA_BLOCK)
    n_pad = kcmp.shape[1]
    W2 = win_all.shape[2]
    per_b = lambda *tail: pl.BlockSpec((1,) + tail, lambda b: (b,) + (0,) * len(tail))
    return pl.pallas_call(
        functools.partial(_nsa_sample_kernel, T=T, P=P, n_blk=n_blk, n_pad=n_pad),
        grid=(B,),
        in_specs=[per_b(T, GROUP_WIDTH), per_b(T, 6 * HEAD_DIM),
                  pl.BlockSpec((1, 1, W2, HEAD_DIM), lambda b: (layer, b, 0, 0)),
                  per_b(n_pad, HEAD_DIM), per_b(n_pad, HEAD_DIM), per_b(T, LANES),
                  pl.BlockSpec((1, 1, P, HEAD_DIM), lambda b: (0, b, 0, 0)),
                  pl.BlockSpec((1, 1, P, HEAD_DIM), lambda b: (1, b, 0, 0))],
        out_specs=per_b(T, GROUP_WIDTH),
        out_shape=jax.ShapeDtypeStruct((B, T, GROUP_WIDTH), F32),
        compiler_params=_cparams(("parallel",)),
        name="nsa_sample",
    )(q_b, kv_b, win_all, kcmp, vcmp, g32_b, ksv, ksv)


def _log_sigmoid_pair(z):
    ls_pos = jnp.minimum(z, 0.0) - jnp.log1p(jnp.exp(-jnp.abs(z)))
    return ls_pos, ls_pos - z


def _strict_upper_ones(n):
    r = lax.broadcasted_iota(jnp.int32, (n, n), 0)
    c = lax.broadcasted_iota(jnp.int32, (n, n), 1)
    return jnp.where(r > c, 1.0, 0.0).astype(BF16)


def _sb_prompt_kernel(q_ref, k_ref, v_ref, o_ref, *, TQ, KC):
    qi = pl.program_id(2)
    q = q_ref[...]
    t0 = qi * TQ
    later = _strict_upper_ones(KC)
    tpos = t0 + lax.broadcasted_iota(jnp.int32, (TQ, 1), 0)
    lane = lax.broadcasted_iota(jnp.int32, (TQ, KC), 1)

    def chunk(k0, carry, diagonal):
        acc, after_c = carry
        z = _dot_t(q, k_ref[pl.ds(k0, KC), :]) * SCALE
        e = jnp.exp(-jnp.abs(z))
        log_keep = -(jnp.maximum(z, 0.0) + jnp.log(1.0 + e))
        r = 1.0 / (1.0 + e)
        beta = jnp.where(z >= 0.0, r, e * r)
        if diagonal:
            mask = k0 + lane < tpos
            log_keep = jnp.where(mask, log_keep, 0.0)
            beta = jnp.where(mask, beta, 0.0)
        a = beta * jnp.exp(_split_dot(log_keep, later) + after_c)
        acc = acc + _dot(a.astype(BF16), v_ref[pl.ds(k0, KC), :])
        return acc, after_c + jnp.sum(log_keep, -1, keepdims=True)

    carry = (jnp.zeros((TQ, HEAD_DIM), F32), jnp.zeros((TQ, 1), F32))
    n_diag = TQ // KC
    for d in range(n_diag):
        carry = chunk(pl.multiple_of(t0 + (n_diag - 1 - d) * KC, KC), carry, True)
    n_full = t0 // KC
    acc, _ = lax.fori_loop(0, n_full, lambda c, cr: chunk(pl.multiple_of((n_full - 1 - c) * KC, KC), cr, False), carry)
    o_ref[...] = acc.astype(BF16)


def _sb_prompt(proj16, B, T):
    TQ = min(512, T)
    KC = min(256, T)
    nq = T // TQ
    col = lambda base: (lambda b, h, i: (b, base // HEAD_DIM + h))
    return pl.pallas_call(
        functools.partial(_sb_prompt_kernel, TQ=TQ, KC=KC),
        grid=(B, HEADS, nq),
        in_specs=[pl.BlockSpec((TQ, HEAD_DIM), lambda b, h, i: (b * nq + i, C_SQ // HEAD_DIM + h)),
                  pl.BlockSpec((T, HEAD_DIM), col(C_SK)),
                  pl.BlockSpec((T, HEAD_DIM), col(C_SV))],
        out_specs=pl.BlockSpec((TQ, HEAD_DIM), lambda b, h, i: (b * nq + i, h)),
        out_shape=jax.ShapeDtypeStruct((B * T, GROUP_WIDTH), BF16),
        compiler_params=_cparams(("parallel", "parallel", "arbitrary")),
        name="sb_prompt",
    )(proj16, proj16, proj16)


def _sb_sample_kernel(pt_ref, qkv_ref, *refs, T, P, n_pages):
    pages = refs[:n_pages]
    o_ref = refs[n_pages]
    R = HEADS * T
    C = PAGE_SIZE
    head = lax.broadcasted_iota(jnp.int32, (R, 1), 0) // T
    qkv = qkv_ref[0]
    q_t = qkv[:, :GROUP_WIDTH]
    col_head = lax.broadcasted_iota(jnp.int32, (R, GROUP_WIDTH), 1) // HEAD_DIM
    q_bd = jnp.where(col_head == head, jnp.concatenate([q_t] * HEADS, axis=0), 0.0).astype(BF16)
    pad_rows = jnp.zeros((C - T, GROUP_WIDTH), F32)
    k_new = jnp.concatenate([qkv[:, GROUP_WIDTH:2 * GROUP_WIDTH], pad_rows], axis=0).astype(BF16)
    v_new = jnp.concatenate([qkv[:, 2 * GROUP_WIDTH:], pad_rows], axis=0).astype(BF16)

    n_ch = n_pages + 1
    zs, vs = [], []
    for j in range(n_ch):
        if j < n_pages:
            kj = jnp.concatenate([_rows(pages[j], h, C, 2 * HEADS) for h in range(HEADS)], axis=1).astype(BF16)
            vj = jnp.concatenate([_rows(pages[j], HEADS + h, C, 2 * HEADS) for h in range(HEADS)], axis=1).astype(BF16)
        else:
            kj, vj = k_new, v_new
        zs.append(_dot_t(q_bd, kj) * SCALE)
        vs.append(vj)
    z = jnp.concatenate(zs, axis=0)
    crow = lax.broadcasted_iota(jnp.int32, (n_ch * R, 1), 0)
    kpos = (crow // R) * C + lax.broadcasted_iota(jnp.int32, (n_ch * R, C), 1)
    mask = kpos < P + (crow % R) % T
    ls_pos, ls_neg = _log_sigmoid_pair(z)
    log_keep = jnp.where(mask, ls_neg, 0.0)
    after_local = _split_dot(log_keep, _strict_upper_ones(C))
    tot = jnp.sum(log_keep, -1, keepdims=True)
    carry = jnp.zeros((R, 1), F32)
    carries = [None] * n_ch
    for j in range(n_ch - 1, -1, -1):
        carries[j] = carry
        carry = carry + tot[j * R:(j + 1) * R]
    after = after_local + jnp.concatenate(carries, axis=0)
    a = jnp.where(mask, jnp.exp(ls_pos + after), 0.0).astype(BF16)
    acc = jnp.zeros((R, GROUP_WIDTH), F32)
    for j in range(n_ch):
        acc = acc + _dot(a[j * R:(j + 1) * R], vs[j])
    for h in range(HEADS):
        o_ref[0, :, h * HEAD_DIM:(h + 1) * HEAD_DIM] = acc[h * T:(h + 1) * T, h * HEAD_DIM:(h + 1) * HEAD_DIM]


def _sb_sample(qkv_b, cache_sb, layer, page_table):
    B, T, _ = qkv_b.shape
    n_pages = page_table.shape[1]
    P = n_pages * PAGE_SIZE

    def page_spec(j):
        return pl.BlockSpec((1, 1, PAGE_SIZE * 2 * HEADS, HEAD_DIM), lambda b, pt: (layer, pt[b, j], 0, 0))

    return pl.pallas_call(
        functools.partial(_sb_sample_kernel, T=T, P=P, n_pages=n_pages),
        grid_spec=pltpu.PrefetchScalarGridSpec(
            num_scalar_prefetch=1, grid=(B,),
            in_specs=[pl.BlockSpec((1, T, 3 * GROUP_WIDTH), lambda b, pt: (b, 0, 0))]
            + [page_spec(j) for j in range(n_pages)],
            out_specs=pl.BlockSpec((1, T, GROUP_WIDTH), lambda b, pt: (b, 0, 0))),
        out_shape=jax.ShapeDtypeStruct((B, T, GROUP_WIDTH), F32),
        compiler_params=_cparams(("arbitrary",)),
        name="sb_sample",
    )(page_table, qkv_b, *([cache_sb] * n_pages))


CONV_HIST = 32
CONV_ROWS = 32


def _put_rows(ref, slab, row, x):
    ref[slab, pl.ds(2 * row, x.shape[0], stride=2), :] = x


def _get_rows(ref, slab, row, n):
    return ref[slab, pl.ds(2 * row, n, stride=2), :]


def _conv_prompt_kernel(ab_ref, w_ref, b_ref, g_ref, be_ref, o_ref, tail_ref, buf_ref, *, tm):
    i = pl.program_id(1)
    n_slab = GROUP_WIDTH // LANES

    @pl.when(i == 0)
    def _():
        for c in range(n_slab):
            _put_rows(buf_ref, c, 0, jnp.zeros((CONV_HIST, LANES), F32))

    @pl.when(i > 0)
    def _():
        for c in range(n_slab):
            _put_rows(buf_ref, c, 0, _get_rows(buf_ref, c, tm, CONV_HIST))

    a = ab_ref[:, :GROUP_WIDTH]
    glu = a * jax.nn.sigmoid(ab_ref[:, GROUP_WIDTH:])
    for c in range(n_slab):
        _put_rows(buf_ref, c, CONV_HIST, glu[:, c * LANES:(c + 1) * LANES])
    tail_ref[0] = glu[tm - CONV_HIST:, :]
    w = w_ref[...]
    bias = b_ref[...]
    for r in range(tm // CONV_ROWS):
        slabs = []
        for c in range(n_slab):
            ls = slice(c * LANES, (c + 1) * LANES)
            acc = jnp.broadcast_to(bias[:, ls], (CONV_ROWS, LANES))
            for k in range(CONV_WIDTH):
                start = CONV_HIST + r * CONV_ROWS - (CONV_WIDTH - 1) + k
                acc = acc + w[k:k + 1, ls] * _get_rows(buf_ref, c, start, CONV_ROWS)
            slabs.append(acc)
        y = _layer_norm(jnp.concatenate(slabs, axis=1), g_ref[...], be_ref[...])
        o_ref[r * CONV_ROWS:(r + 1) * CONV_ROWS, :] = (y * jax.nn.sigmoid(y)).astype(BF16)


def _conv_prompt(proj32, w, b, g, be, B, T):
    tm = min(256, T)
    nt = T // tm
    vec = pl.BlockSpec((1, GROUP_WIDTH), lambda bb, i: (0, 0))
    return pl.pallas_call(
        functools.partial(_conv_prompt_kernel, tm=tm),
        grid=(B, nt),
        in_specs=[pl.BlockSpec((tm, 2 * GROUP_WIDTH), lambda bb, i: (bb * nt + i, C_CONV // (2 * GROUP_WIDTH))),
                  pl.BlockSpec((CONV_WIDTH, GROUP_WIDTH), lambda bb, i: (0, 0)), vec, vec, vec],
        out_specs=[pl.BlockSpec((tm, GROUP_WIDTH), lambda bb, i: (bb * nt + i, 0)),
                   pl.BlockSpec((1, CONV_HIST, GROUP_WIDTH), lambda bb, i: (bb, 0, 0))],
        out_shape=[jax.ShapeDtypeStruct((B * T, GROUP_WIDTH), BF16),
                   jax.ShapeDtypeStruct((B, CONV_HIST, GROUP_WIDTH), F32)],
        scratch_shapes=[pltpu.VMEM((GROUP_WIDTH // LANES, 2 * (CONV_HIST + tm), LANES), F32)],
        compiler_params=_cparams(("parallel", "arbitrary")),
        name="conv_prompt",
    )(proj32, w, b, g, be)


def _conv_sample_kernel(st_ref, ab_ref, w_ref, b_ref, g_ref, be_ref, o_ref, glu_ref, *, T):
    S = CONV_WIDTH - 1
    w = w_ref[...]
    glu = []
    for t in range(T):
        ab = ab_ref[t]
        gt = ab[:, :GROUP_WIDTH] * jax.nn.sigmoid(ab[:, GROUP_WIDTH:])
        glu_ref[t] = gt
        glu.append(gt)
    for t in range(T):
        acc = jnp.broadcast_to(b_ref[...], glu[0].shape)
        for k in range(CONV_WIDTH):
            j = t + k
            acc = acc + w[k:k + 1, :] * (st_ref[j] if j < S else glu[j - S])
        y = _layer_norm(acc, g_ref[...], be_ref[...])
        o_ref[t] = (y * jax.nn.sigmoid(y)).astype(BF16)


def _conv_sample(state_tm, proj32_3d, w, b, g, be):
    S, B, _ = state_tm.shape
    T = proj32_3d.shape[0]
    bb = min(32, B)
    vec = pl.BlockSpec((1, GROUP_WIDTH), lambda i: (0, 0))
    out = pl.BlockSpec((T, bb, GROUP_WIDTH), lambda i: (0, i, 0))
    return pl.pallas_call(
        functools.partial(_conv_sample_kernel, T=T),
        grid=(B // bb,),
        in_specs=[pl.BlockSpec((S, bb, GROUP_WIDTH), lambda i: (0, i, 0)),
                  pl.BlockSpec((T, bb, 2 * GROUP_WIDTH), lambda i: (0, i, C_CONV // (2 * GROUP_WIDTH))),
                  pl.BlockSpec((CONV_WIDTH, GROUP_WIDTH), lambda i: (0, 0)), vec, vec, vec],
        out_specs=[out, out],
        out_shape=[jax.ShapeDtypeStruct((T, B, GROUP_WIDTH), BF16),
                   jax.ShapeDtypeStruct((T, B, GROUP_WIDTH), F32)],
        compiler_params=_cparams(("parallel",)),
        name="conv_sample",
    )(state_tm, proj32_3d, w, b, g, be)


def _gm_prompt_kernel(uv_ref, ws_ref, bias_ref, g_ref, be_ref, o_ref, *, cs):
    vn = _layer_norm(uv_ref[:, GROUP_WIDTH:], g_ref[...], be_ref[...])
    r = lax.broadcasted_iota(jnp.int32, (cs, cs), 0)
    c = lax.broadcasted_iota(jnp.int32, (cs, cs), 1)
    for gi in range(HEADS):
        sl = slice(gi * HEAD_DIM, (gi + 1) * HEAD_DIM)
        w = jnp.where(r >= c, ws_ref[gi], 0.0).astype(BF16)
        s = _dot(w, vn[:, sl].astype(BF16)) + bias_ref[:, gi:gi + 1]
        o_ref[:, sl] = (uv_ref[:, sl] * s).astype(BF16)


def _gm_prompt(proj32, ws, bias_t, g, be, B, T):
    cs = min(GM_CHUNK, T)
    n = B * T // cs
    vec = pl.BlockSpec((1, GROUP_WIDTH), lambda i: (0, 0))
    return pl.pallas_call(
        functools.partial(_gm_prompt_kernel, cs=cs),
        grid=(n,),
        in_specs=[pl.BlockSpec((cs, 2 * GROUP_WIDTH), lambda i: (i, C_GM // (2 * GROUP_WIDTH))),
                  pl.BlockSpec((HEADS, cs, cs), lambda i: (0, 0, 0)),
                  pl.BlockSpec((cs, HEADS), lambda i: (0, 0)), vec, vec],
        out_specs=pl.BlockSpec((cs, GROUP_WIDTH), lambda i: (i, 0)),
        out_shape=jax.ShapeDtypeStruct((B * T, GROUP_WIDTH), BF16),
        compiler_params=_cparams(("parallel",)),
        name="gm_prompt",
    )(proj32, ws, bias_t, g, be)


def _gm_sample_kernel(uv_ref, wexp_ref, bexp_ref, g_ref, be_ref, o_ref, vn_ref, *, T):
    vn = []
    for t in range(T):
        v = _layer_norm(uv_ref[t][:, GROUP_WIDTH:], g_ref[...], be_ref[...])
        vn_ref[t] = v
        vn.append(v)
    for i in range(T):
        s = jnp.broadcast_to(bexp_ref[i:i + 1, :], vn[0].shape)
        for j in range(i + 1):
            s = s + wexp_ref[i, j:j + 1, :] * vn[j]
        o_ref[i] = (uv_ref[i][:, :GROUP_WIDTH] * s).astype(BF16)


def _gm_sample(proj32_3d, wexp, bexp, g, be):
    T, B, _ = proj32_3d.shape
    bb = min(32, B)
    vec = pl.BlockSpec((1, GROUP_WIDTH), lambda i: (0, 0))
    out = pl.BlockSpec((T, bb, GROUP_WIDTH), lambda i: (0, i, 0))
    return pl.pallas_call(
        functools.partial(_gm_sample_kernel, T=T),
        grid=(B // bb,),
        in_specs=[pl.BlockSpec((T, bb, 2 * GROUP_WIDTH), lambda i: (0, i, C_GM // (2 * GROUP_WIDTH))),
                  pl.BlockSpec((T, T, GROUP_WIDTH), lambda i: (0, 0, 0)),
                  pl.BlockSpec((T, GROUP_WIDTH), lambda i: (0, 0)), vec, vec],
        out_specs=[out, out],
        out_shape=[jax.ShapeDtypeStruct((T, B, GROUP_WIDTH), BF16),
                   jax.ShapeDtypeStruct((T, B, GROUP_WIDTH), F32)],
        compiler_params=_cparams(("parallel",)),
        name="gm_sample",
    )(proj32_3d, wexp, bexp, g, be)


def _outproj_kernel(a0_ref, a1_ref, a2_ref, a3_ref, w_ref, x_ref, g_ref, b_ref, h32_ref, h16_ref):
    acc = _dot(a0_ref[...], w_ref[0:GROUP_WIDTH, :])
    for n, a_ref in enumerate((a1_ref, a2_ref, a3_ref), start=1):
        acc = acc + _dot(a_ref[...], w_ref[n * GROUP_WIDTH:(n + 1) * GROUP_WIDTH, :])
    h = _layer_norm(ALPHA * x_ref[...] + acc, g_ref[...], b_ref[...])
    h32_ref[...] = h
    h16_ref[...] = h.astype(BF16)


def _outproj(mix, w16, x32, g, b, tm):
    M = x32.shape[0]
    row = lambda i: (i, 0)
    vec = pl.BlockSpec((1, D_MODEL), lambda i: (0, 0))
    return pl.pallas_call(
        _outproj_kernel,
        grid=(M // tm,),
        in_specs=[pl.BlockSpec((tm, GROUP_WIDTH), row)] * 4
        + [pl.BlockSpec((D_MODEL, D_MODEL), lambda i: (0, 0)), pl.BlockSpec((tm, D_MODEL), row), vec, vec],
        out_specs=[pl.BlockSpec((tm, D_MODEL), row)] * 2,
        out_shape=[jax.ShapeDtypeStruct((M, D_MODEL), F32), jax.ShapeDtypeStruct((M, D_MODEL), BF16)],
        compiler_params=_cparams(("parallel",)),
        name="outproj_ln1",
    )(*mix, w16, x32, g, b)


FFN_TF = 512
FFN_CARRY = SUBLANES
FFN_SUB = 256
FFN_DOWN_TK = D_FF // 4


def _ffn_up_kernel(h_ref, wa_ref, wb_ref, cwa_ref, cwb_ref, cba_ref, cbb_ref, *refs, shift, tiles_per_seq, tm):
    if shift == 1:
        g_ref, ta_ref, tb_ref, ca_ref, cb_ref, sh_ref = refs
    else:
        sa_ref, sb_ref, g_ref, ta_ref, tb_ref = refs
    m = pl.program_id(0)
    f = pl.program_id(1)
    h = h_ref[...]

    def conv3(cw, cb, p2, p1, x):
        return cw[0:1, :] * p2 + cw[1:2, :] * p1 + cw[2:3, :] * x + cb

    def gate(a, b):
        return (a * jax.nn.sigmoid(a) * b).astype(BF16)

    if shift == 1:
        @pl.when(m % tiles_per_seq == 0)
        def _():
            ca_ref[f] = jnp.zeros((FFN_CARRY, FFN_TF), F32)
            cb_ref[f] = jnp.zeros((FFN_CARRY, FFN_TF), F32)

    for c in range(FFN_TF // FFN_SUB):
        cs = slice(c * FFN_SUB, (c + 1) * FFN_SUB)
        body = []
        for part in range(2):
            up = _dot(h, (wa_ref, wb_ref)[part][:, cs])
            cw = (cwa_ref, cwb_ref)[part][:, cs]
            cb = (cba_ref, cbb_ref)[part][:, cs]
            if shift == 1:
                carry_ref = (ca_ref, cb_ref)[part]
                prev = carry_ref[f][:, cs]
                slabs = []
                for s in range(FFN_SUB // LANES):
                    ls = slice(s * LANES, (s + 1) * LANES)
                    slab = part * (FFN_SUB // LANES) + s
                    _put_rows(sh_ref, slab, 0, prev[:, ls])
                    _put_rows(sh_ref, slab, FFN_CARRY, up[:, ls])
                    slabs.append(conv3(cw[:, ls], cb[:, ls], _get_rows(sh_ref, slab, FFN_CARRY - 2, tm),
                                       _get_rows(sh_ref, slab, FFN_CARRY - 1, tm), up[:, ls]))
                body.append(jnp.concatenate(slabs, axis=1))
                tail_rows = up[tm - FFN_CARRY:, :]
                carry_ref[f, :, cs] = tail_rows
                (ta_ref, tb_ref)[part][0, :, cs] = tail_rows
            else:
                st = (sa_ref, sb_ref)[part][:, cs]
                p1 = jnp.concatenate([st[shift:], up[:tm - shift]], axis=0)
                p2 = jnp.concatenate([st, up[:tm - 2 * shift]], axis=0)
                (ta_ref, tb_ref)[part][:, cs] = up[tm - 2 * shift:, :]
                body.append(conv3(cw, cb, p2, p1, up))
        g_ref[:, cs] = gate(body[0], body[1])


def _ffn_up(h16, wup16, cw, cb, state_tm, *, shift, seq_len):
    M = h16.shape[0]
    nf = D_FF // FFN_TF
    wa = pl.BlockSpec((D_MODEL, FFN_TF), lambda m, f: (0, f))
    wb = pl.BlockSpec((D_MODEL, FFN_TF), lambda m, f: (0, f + nf))
    cwa = pl.BlockSpec((FFN_CONV_WIDTH, FFN_TF), lambda m, f: (0, f))
    cwb = pl.BlockSpec((FFN_CONV_WIDTH, FFN_TF), lambda m, f: (0, f + nf))
    cba = pl.BlockSpec((1, FFN_TF), lambda m, f: (0, f))
    cbb = pl.BlockSpec((1, FFN_TF), lambda m, f: (0, f + nf))
    if shift == 1:
        tm = min(512, seq_len)
        nm = M // tm
        kern = functools.partial(_ffn_up_kernel, shift=1, tiles_per_seq=seq_len // tm, tm=tm)
        tail = pl.BlockSpec((1, FFN_CARRY, FFN_TF), lambda m, f: (m, 0, f))
        return pl.pallas_call(
            kern, grid=(nm, nf),
            in_specs=[pl.BlockSpec((tm, D_MODEL), lambda m, f: (m, 0)), wa, wb, cwa, cwb, cba, cbb],
            out_specs=[pl.BlockSpec((tm, FFN_TF), lambda m, f: (m, f)), tail, tail],
            out_shape=[jax.ShapeDtypeStruct((M, D_FF), BF16),
                       jax.ShapeDtypeStruct((nm, FFN_CARRY, D_FF), F32),
                       jax.ShapeDtypeStruct((nm, FFN_CARRY, D_FF), F32)],
            scratch_shapes=[pltpu.VMEM((nf, FFN_CARRY, FFN_TF), F32), pltpu.VMEM((nf, FFN_CARRY, FFN_TF), F32),
                            pltpu.VMEM((2 * FFN_SUB // LANES, 2 * (FFN_CARRY + tm), LANES), F32)],
            compiler_params=_cparams(("arbitrary", "arbitrary")),
            name="ffn_up_prompt",
        )(h16, wup16, wup16, cw, cw, cb, cb)
    tm = M
    kern = functools.partial(_ffn_up_kernel, shift=shift, tiles_per_seq=1, tm=tm)
    sa = pl.BlockSpec((2 * shift, FFN_TF), lambda m, f: (0, f))
    sb = pl.BlockSpec((2 * shift, FFN_TF), lambda m, f: (0, f + nf))
    tail = pl.BlockSpec((2 * shift, FFN_TF), lambda m, f: (0, f))
    return pl.pallas_call(
        kern, grid=(1, nf),
        in_specs=[pl.BlockSpec((tm, D_MODEL), lambda m, f: (0, 0)), wa, wb, cwa, cwb, cba, cbb, sa, sb],
        out_specs=[pl.BlockSpec((tm, FFN_TF), lambda m, f: (0, f)), tail, tail],
        out_shape=[jax.ShapeDtypeStruct((M, D_FF), BF16),
                   jax.ShapeDtypeStruct((2 * shift, D_FF), F32),
                   jax.ShapeDtypeStruct((2 * shift, D_FF), F32)],
        compiler_params=_cparams(("arbitrary", "arbitrary")),
        name="ffn_up_sample",
    )(h16, wup16, wup16, cw, cw, cb, cb, state_tm, state_tm)


def _ffn_down_kernel(g_ref, w_ref, h_ref, ln_g_ref, ln_b_ref, y32_ref, y16_ref, acc_ref):
    k = pl.program_id(1)
    last = pl.num_programs(1) - 1
    part = _dot(g_ref[...], w_ref[...])

    @pl.when(k == 0)
    def _():
        acc_ref[...] = part

    @pl.when((k > 0) & (k < last))
    def _():
        acc_ref[...] += part

    @pl.when(k == last)
    def _():
        y = _layer_norm(ALPHA * h_ref[...] + (acc_ref[...] + part), ln_g_ref[...], ln_b_ref[...])
        y32_ref[...] = y
        y16_ref[...] = y.astype(BF16)


def _ffn_down(g16, wdown16, h32, ln_g, ln_b, tm):
    M = h32.shape[0]
    tk = FFN_DOWN_TK
    vec = pl.BlockSpec((1, D_MODEL), lambda m, k: (0, 0))
    row = pl.BlockSpec((tm, D_MODEL), lambda m, k: (m, 0))
    return pl.pallas_call(
        _ffn_down_kernel,
        grid=(M // tm, D_FF // tk),
        in_specs=[pl.BlockSpec((tm, tk), lambda m, k: (m, k)),
                  pl.BlockSpec((tk, D_MODEL), lambda m, k: (k, 0)), row, vec, vec],
        out_specs=[row, row],
        out_shape=[jax.ShapeDtypeStruct((M, D_MODEL), F32), jax.ShapeDtypeStruct((M, D_MODEL), BF16)],
        scratch_shapes=[pltpu.VMEM((tm, D_MODEL), F32)],
        compiler_params=_cparams(("parallel", "arbitrary")),
        name="ffn_down_ln2",
    )(g16, wdown16, h32, ln_g, ln_b)


def _rope_tables(pos):
    half = HEAD_DIM // 2
    inv = ROPE_THETA ** (-jnp.arange(half, dtype=F32) / half)
    ang = pos.astype(F32)[:, None] * inv[None, :]
    cos, sin = jnp.cos(ang), jnp.sin(ang)
    return jnp.concatenate([cos, cos], -1), jnp.concatenate([-sin, sin], -1)


N_QKV = GROUP_WIDTH + 6 * HEAD_DIM
N_GATE = 3 * HEADS
N_REST = 2 * GROUP_WIDTH + 2 * GROUP_WIDTH + 3 * GROUP_WIDTH


W_IN_TILE = 256


def _w_in_rows_kernel(w_ref, o_ref, *, layer, depth):
    i = pl.program_id(0)
    n_kt = w_ref.shape[1] // depth
    row = lax.broadcasted_iota(jnp.int32, (W_IN_TILE, LANES), 0)
    keep = (i < pl.num_programs(0) - 1) | (row < N_GATE)
    for kt in range(n_kt):
        x = w_ref[:, kt * depth + layer, :]
        o_ref[:, kt * LANES:(kt + 1) * LANES] = jnp.where(keep, x, 0.0).astype(BF16)


def _w_in_rows(w_in, layer):
    depth, K, N = w_in.shape
    n_kt = K // LANES
    view = w_in.reshape(depth, n_kt, LANES, N).transpose(3, 1, 0, 2).reshape(N, n_kt * depth, LANES)
    n_rest, n_qkv = N_REST // W_IN_TILE, N_QKV // W_IN_TILE

    def src_row(i):
        return jnp.where(i < n_rest, N_QKV + N_GATE + i * W_IN_TILE,
                         jnp.where(i < n_rest + n_qkv, (i - n_rest) * W_IN_TILE, N_QKV))

    return pl.pallas_call(
        functools.partial(_w_in_rows_kernel, layer=layer, depth=depth),
        grid=(N_PROJ // W_IN_TILE,),
        in_specs=[pl.BlockSpec((pl.Element(W_IN_TILE), pl.Element(n_kt * depth), pl.Element(LANES)),
                               lambda i: (src_row(i), 0, 0))],
        out_specs=pl.BlockSpec((W_IN_TILE, K), lambda i: (i, 0)),
        out_shape=jax.ShapeDtypeStruct((N_PROJ, K), BF16),
        compiler_params=_cparams(("parallel",)),
        name="w_in_rows",
    )(view)


def _row2(v):
    return v.reshape(1, -1)


def _layer_prompt(x32, x16, lw, B, T):
    M = B * T
    proj32, proj16, sb_state = _proj(x16, lw["w_in"], min(1024, M))
    pos = jnp.tile(jnp.arange(T), B)
    cos2, sin2 = _rope_tables(pos)
    tm_r = min(256, T)
    pe_t = jnp.tile(lw["nsa_pe"], (1, tm_r // NSA_BLOCK, 1))
    q16, _, kv16, xkv, rows4, rows2 = _rope(proj32, cos2, sin2, pe_t, tm_r)

    n_blk = T // NSA_BLOCK
    cmp = _compress(xkv.reshape(2, B * n_blk, NSA_BLOCK * HEAD_DIM), lw["nsa_cw1"], lw["nsa_cw2"],
                    min(128, B * n_blk))
    kcmp = cmp[0].reshape(B, n_blk, HEAD_DIM)
    vcmp = cmp[1].reshape(B, n_blk, HEAD_DIM)
    o_nsa = _nsa_prompt(q16, kv16, kcmp, vcmp, proj32, B, T)

    o_conv, conv_tail = _conv_prompt(proj32, lw["conv_w"], lw["conv_b"], lw["conv_ln_g"], lw["conv_ln_b"], B, T)
    o_gm = _gm_prompt(proj32, lw["gm_ws"], lw["gm_bias_t"], lw["gm_ln_g"], lw["gm_ln_b"], B, T)
    o_sb = _sb_prompt(proj16, B, T)

    h32, h16 = _outproj((o_nsa, o_conv, o_gm, o_sb), lw["w_out"], x32, lw["ln1_g"], lw["ln1_b"], min(512, M))
    g16, tail_a, tail_b = _ffn_up(h16, lw["ffn_up"], lw["ffn_conv_w"], lw["ffn_conv_b"], None, shift=1, seq_len=T)
    y32, y16 = _ffn_down(g16, lw["ffn_down"], h32, lw["ln2_g"], lw["ln2_b"], min(512, M))

    nsa_rows = rows4.reshape(B, T, 4, HEAD_DIM)
    wlen = min(NSA_WINDOW, T)
    win_rows = rows2.reshape(B, T, 2, HEAD_DIM)[:, T - wlen:]
    sb_rows = sb_state.reshape(B, T, 2, HEADS, HEAD_DIM)
    conv_rows = conv_tail[:, CONV_HIST - (CONV_WIDTH - 1):]
    tiles_per_seq = tail_a.shape[0] // B
    ffn_tail = jnp.concatenate([tail_a, tail_b], -1)[tiles_per_seq - 1::tiles_per_seq]
    ffn_rows = ffn_tail[:, FFN_CARRY - (FFN_CONV_WIDTH - 1):]
    return y32, y16, (nsa_rows, sb_rows, win_rows, conv_rows, ffn_rows)


def _layer_sample(x32, x16, lw, layer, B, T, cache_nsa, cache_sb, win_all, conv_state, ffn_state, page_table):
    M = B * T
    n_pages = page_table.shape[1]
    P = n_pages * PAGE_SIZE
    proj32, proj16, sb_state = _proj(x16, lw["w_in"], M)
    pos = jnp.repeat(P + jnp.arange(T), B)
    cos2, sin2 = _rope_tables(pos)
    tm_r = min(256, M)
    pe_t = jnp.tile(lw["nsa_pe"], (1, tm_r // NSA_BLOCK, 1))
    q16, kv32, kv16, _, _, _ = _rope(proj32, cos2, sin2, pe_t, tm_r)

    to_b = lambda a: a.reshape(T, B, -1).swapaxes(0, 1)
    kv32_b = to_b(kv32)

    xkv, ksv = _gather_nsa(cache_nsa, layer, page_table, lw["nsa_pe"])
    n_past = P // NSA_BLOCK
    cmp_past = _compress_rows(xkv, lw["nsa_cw1"], lw["nsa_cw2"], min(256, B * n_past)).reshape(2, B, n_past, HEAD_DIM)
    n_blk = -(-(P + T) // NSA_BLOCK)
    n_new = n_blk - n_past
    new_rows = jnp.pad(kv32_b[:, :, :2 * HEAD_DIM], ((0, 0), (0, n_new * NSA_BLOCK - T), (0, 0)))
    new_rows = new_rows.reshape(B, n_new, NSA_BLOCK, 2, HEAD_DIM)
    x_new = jnp.stack([new_rows[:, :, :, 0] + lw["nsa_pe"][0], new_rows[:, :, :, 1] + lw["nsa_pe"][1]])
    x_new = x_new.reshape(2, B * n_new, NSA_BLOCK * HEAD_DIM).astype(BF16)
    cmp_new = _compress(x_new, lw["nsa_cw1"], lw["nsa_cw2"], B * n_new).reshape(2, B, n_new, HEAD_DIM)
    n_pad = -(-n_blk // SUBLANES) * SUBLANES
    cmp = jnp.concatenate([cmp_past, cmp_new, jnp.zeros((2, B, n_pad - n_blk, HEAD_DIM), F32)], axis=2)

    g32_b = to_b(proj32[:, C_G:C_G + LANES])
    o_nsa = _nsa_sample(to_b(q16).astype(F32), to_b(kv16).astype(F32), win_all, layer, cmp[0], cmp[1], g32_b, ksv)
    o_sb = _sb_sample(to_b(proj16[:, C_SQ:C_SQ + 3 * GROUP_WIDTH]).astype(F32), cache_sb, layer, page_table)
    to_t = lambda a: a.swapaxes(0, 1).reshape(M, -1).astype(BF16)
    o_nsa, o_sb = to_t(o_nsa), to_t(o_sb)

    proj32_3d = proj32.reshape(T, B, N_PROJ)
    o_conv, glu = _conv_sample(conv_state.swapaxes(0, 1), proj32_3d, lw["conv_w"], lw["conv_b"],
                               lw["conv_ln_g"], lw["conv_ln_b"])
    o_gm, vn = _gm_sample(proj32_3d, lw["gm_wexp"], lw["gm_bexp"], lw["gm_ln_g"], lw["gm_ln_b"])

    mix = (o_nsa, o_conv.reshape(M, GROUP_WIDTH), o_gm.reshape(M, GROUP_WIDTH), o_sb)
    h32, h16 = _outproj(mix, lw["w_out"], x32, lw["ln1_g"], lw["ln1_b"], min(512, M))
    ffn_state_tm = ffn_state.swapaxes(0, 1).reshape(2 * B, 2 * D_FF)
    g16, tail_a, tail_b = _ffn_up(h16, lw["ffn_up"], lw["ffn_conv_w"], lw["ffn_conv_b"], ffn_state_tm,
                                  shift=B, seq_len=T)
    y32, y16 = _ffn_down(g16, lw["ffn_down"], h32, lw["ln2_g"], lw["ln2_b"], min(512, M))

    nsa_rows = kv32_b[:, :, :4 * HEAD_DIM].reshape(B, T, 4, HEAD_DIM)
    win_rows = kv32_b[:, :, 4 * HEAD_DIM:].reshape(B, T, 2, HEAD_DIM)
    sb_rows = sb_state.reshape(T, B, 2, HEADS, HEAD_DIM).swapaxes(0, 1)
    ccat = jnp.concatenate([conv_state, glu.swapaxes(0, 1)], 1)
    conv_rows = ccat[:, ccat.shape[1] - (CONV_WIDTH - 1):]
    ffn_rows = jnp.concatenate([tail_a, tail_b], -1).reshape(2, B, 2 * D_FF).swapaxes(0, 1)
    gm_v = vn.swapaxes(0, 1)
    return y32, y16, (nsa_rows, sb_rows, win_rows, conv_rows, ffn_rows, gm_v)


def _win_join_kernel(st_ref, new_ref, o_ref, *, drop):
    kept = st_ref.shape[2] - drop
    o_ref[0, :, 0:kept, :] = st_ref[0, :, drop:, :]
    o_ref[0, :, kept:, :] = new_ref[0]


def _win_join(win_all, new_all, keep):
    depth, B, w2, d = win_all.shape
    t2 = new_all.shape[2]
    bb = min(8, B)
    blk = lambda rows: pl.BlockSpec((1, bb, rows, d), lambda l, i: (l, i, 0, 0))
    return pl.pallas_call(
        functools.partial(_win_join_kernel, drop=w2 + t2 - 2 * keep),
        grid=(depth, B // bb),
        in_specs=[blk(w2), blk(t2)],
        out_specs=blk(2 * keep),
        out_shape=jax.ShapeDtypeStruct((depth, B, 2 * keep, d), F32),
        compiler_params=_cparams(("parallel", "parallel")),
        name="win_join",
    )(win_all, new_all)


def _layer_weights(l, T_s, w_in, nsa_pe, nsa_cw1, nsa_cw2, conv_w, conv_b, conv_ln_g, conv_ln_b,
                   gm_ln_g, gm_ln_b, gm_ws, gm_bias, w_out, ln1_g, ln1_b,
                   ffn_up, ffn_conv_w, ffn_conv_b, ffn_down, ln2_g, ln2_b):
    cs = min(GM_CHUNK, T_s)
    ws_s = jnp.tril(gm_ws[l][:, :cs, :cs])
    wexp = jnp.repeat(ws_s.transpose(1, 2, 0), HEAD_DIM, axis=2)
    bexp = jnp.repeat(gm_bias[l][:, :cs].T, HEAD_DIM, axis=1)
    return {
        "w_in": _w_in_rows(w_in, l),
        "nsa_pe": nsa_pe[l],
        "nsa_cw1": nsa_cw1[l].astype(BF16), "nsa_cw2": nsa_cw2[l].astype(BF16),
        "conv_w": conv_w[l], "conv_b": _row2(conv_b[l]),
        "conv_ln_g": _row2(conv_ln_g[l]), "conv_ln_b": _row2(conv_ln_b[l]),
        "gm_ln_g": _row2(gm_ln_g[l]), "gm_ln_b": _row2(gm_ln_b[l]),
        "gm_ws": gm_ws[l], "gm_bias_t": gm_bias[l].T, "gm_wexp": wexp, "gm_bexp": bexp,
        "w_out": w_out[l].astype(BF16), "ln1_g": _row2(ln1_g[l]), "ln1_b": _row2(ln1_b[l]),
        "ffn_up": ffn_up[l].astype(BF16), "ffn_conv_w": ffn_conv_w[l], "ffn_conv_b": _row2(ffn_conv_b[l]),
        "ffn_down": ffn_down[l].astype(BF16), "ln2_g": _row2(ln2_g[l]), "ln2_b": _row2(ln2_b[l]),
    }


def kernel(x_prompt, x_sample, cache_nsa_kv, cache_sb_kv, state_nsa_win, state_conv, state_ffn, page_table,
           w_in, nsa_pe, nsa_cw1, nsa_cw2, conv_w, conv_b, conv_ln_g, conv_ln_b, gm_ln_g, gm_ln_b, gm_ws, gm_bias,
           w_out, ln1_g, ln1_b, ffn_up, ffn_conv_w, ffn_conv_b, ffn_down, ln2_g, ln2_b):
    Bp, Tp, _ = x_prompt.shape
    Bs, Ts, _ = x_sample.shape
    depth = w_in.shape[0]
    cache_nsa = cache_nsa_kv.reshape(cache_nsa_kv.shape[:2] + (-1, HEAD_DIM))
    cache_sb = cache_sb_kv.reshape(cache_sb_kv.shape[:2] + (-1, HEAD_DIM))
    win_all = state_nsa_win.reshape(state_nsa_win.shape[:2] + (-1, HEAD_DIM))

    xp32 = x_prompt.reshape(Bp * Tp, D_MODEL)
    xs32 = x_sample.swapaxes(0, 1).reshape(Ts * Bs, D_MODEL)
    xp16, xs16 = xp32.astype(BF16), xs32.astype(BF16)
    outs_p, outs_s = [], []
    for l in range(depth):
        lw = _layer_weights(l, Ts, w_in, nsa_pe, nsa_cw1, nsa_cw2, conv_w, conv_b, conv_ln_g, conv_ln_b,
                            gm_ln_g, gm_ln_b, gm_ws, gm_bias, w_out, ln1_g, ln1_b,
                            ffn_up, ffn_conv_w, ffn_conv_b, ffn_down, ln2_g, ln2_b)
        xp32, xp16, sp = _layer_prompt(xp32, xp16, lw, Bp, Tp)
        xs32, xs16, ss = _layer_sample(xs32, xs16, lw, l, Bs, Ts, cache_nsa, cache_sb, win_all,
                                       state_conv[l], state_ffn[l], page_table)
        outs_p.append(sp)
        outs_s.append(ss)
    y_p = xp32.reshape(Bp, Tp, D_MODEL)
    y_s = xs32.reshape(Ts, Bs, D_MODEL).swapaxes(0, 1)
    st = lambda outs, i: jnp.stack([o[i] for o in outs])
    keep = min(NSA_WINDOW, page_table.shape[1] * PAGE_SIZE + Ts)
    win_s = _win_join(win_all, st(outs_s, 2).reshape(depth, Bs, 2 * Ts, HEAD_DIM), keep)
    win_s = win_s.reshape(depth, Bs, keep, 2, HEAD_DIM)
    return (y_p, y_s, st(outs_p, 0), st(outs_s, 0), st(outs_p, 1), st(outs_s, 1), st(outs_p, 2), win_s,
            st(outs_p, 3), st(outs_s, 3), st(outs_p, 4), st(outs_s, 4), st(outs_s, 5))
```

```python
import functools
import math

import jax
import jax.numpy as jnp
import numpy as np
from jax import lax
from jax.experimental import pallas as pl
from jax.experimental.pallas import tpu as pltpu

F32 = jnp.float32
BF16 = jnp.bfloat16

D_MODEL = 2048
HEAD_DIM = 128
HEADS = 4
GROUP_WIDTH = HEADS * HEAD_DIM
NSA_BLOCK = 64
NSA_TOP_N = 16
NSA_WINDOW = 512
CONV_WIDTH = 31
GM_CHUNK = 128
D_FF = (11 * D_MODEL) // 4
FFN_CONV_WIDTH = 3
ROPE_THETA = 10000.0
LN_EPS = 1e-5
DEPTH = 2
ALPHA = (2 * DEPTH) ** 0.25
PAGE_SIZE = 128
SCALE = HEAD_DIM ** -0.5

VMEM_LIMIT_V7X = 56 * 1024 * 1024
SUBLANES = 8
LANES = 128

C_CONV = 0
C_GM = C_CONV + 2 * GROUP_WIDTH
C_SQ = C_GM + 2 * GROUP_WIDTH
C_SK = C_SQ + GROUP_WIDTH
C_SV = C_SK + GROUP_WIDTH
C_Q = C_SV + GROUP_WIDTH
C_KV = C_Q + GROUP_WIDTH
C_G = C_KV + 6 * HEAD_DIM
N_PROJ = 5120
PROJ_TM = 256


def _cparams(sem):
    return pltpu.CompilerParams(dimension_semantics=sem, vmem_limit_bytes=VMEM_LIMIT_V7X)


def _layer_norm(x, g, b):
    mu = jnp.mean(x, -1, keepdims=True)
    xc = x - mu
    var = jnp.mean(xc * xc, -1, keepdims=True)
    return xc * lax.rsqrt(var + LN_EPS) * g + b


def _masked_softmax(s, mask):
    sm = jnp.where(mask, s, -1e30)
    m = jnp.max(sm, -1, keepdims=True)
    e = jnp.where(mask, jnp.exp(sm - m), 0.0)
    return e / jnp.maximum(jnp.sum(e, -1, keepdims=True), jnp.finfo(F32).tiny)


def _dot_t(a, b):
    return lax.dot_general(a, b, (((1,), (1,)), ((), ())), preferred_element_type=F32)


def _dot(a, b):
    return jnp.dot(a, b, preferred_element_type=F32)


def _split_dot(x, u16):
    hi = x.astype(BF16)
    lo = (x - hi.astype(F32)).astype(BF16)
    return _dot(hi, u16) + _dot(lo, u16)


def _proj_kernel(x_ref, wt_ref, o32_ref, o16_ref, sb_rows_ref, *, tm):
    acc = _dot_t(x_ref[...], wt_ref[...])
    o32_ref[...] = acc
    o16_ref[...] = acc.astype(BF16)
    for kv, c0 in enumerate((C_SK, C_SV)):
        for h in range(HEADS):
            cols = slice(c0 + h * HEAD_DIM, c0 + (h + 1) * HEAD_DIM)
            sb_rows_ref[pl.ds(kv * HEADS + h, tm, stride=2 * HEADS), :] = acc[:, cols]


def _proj(x16, wt16, tm):
    M, K = x16.shape
    N = wt16.shape[0]
    row = lambda i: (i, 0)
    return pl.pallas_call(
        functools.partial(_proj_kernel, tm=tm),
        grid=(M // tm,),
        in_specs=[pl.BlockSpec((tm, K), row),
                  pl.BlockSpec((N, K), lambda i: (0, 0), pipeline_mode=pl.Buffered(1))],
        out_specs=[pl.BlockSpec((tm, N), row), pl.BlockSpec((tm, N), row),
                   pl.BlockSpec((tm * 2 * HEADS, HEAD_DIM), row)],
        out_shape=[jax.ShapeDtypeStruct((M, N), F32), jax.ShapeDtypeStruct((M, N), BF16),
                   jax.ShapeDtypeStruct((M * 2 * HEADS, HEAD_DIM), F32)],
        compiler_params=_cparams(("parallel",)),
        name="proj",
    )(x16, wt16)


def _rope_kernel(q_ref, kva_ref, kvb_ref, kvc_ref, cos_ref, sin_ref, pe_ref,
                 q16_ref, kv32_ref, kv16_ref, xkv_ref, rows4_ref, rows2_ref):
    tm = q_ref.shape[0]
    cos = cos_ref[...]
    sin = sin_ref[...]

    def rot(x):
        return x * cos + pltpu.roll(x, HEAD_DIM // 2, axis=1) * sin

    for h in range(HEADS):
        sl = slice(h * HEAD_DIM, (h + 1) * HEAD_DIM)
        q16_ref[:, sl] = rot(q_ref[:, sl]).astype(BF16)
    for p, ref in enumerate((kva_ref, kvb_ref, kvc_ref)):
        k = rot(ref[:, :HEAD_DIM])
        v = ref[:, HEAD_DIM:]
        ks = slice(2 * p * HEAD_DIM, (2 * p + 1) * HEAD_DIM)
        vs = slice((2 * p + 1) * HEAD_DIM, (2 * p + 2) * HEAD_DIM)
        kv32_ref[:, ks] = k
        kv32_ref[:, vs] = v
        kv16_ref[:, ks] = k.astype(BF16)
        kv16_ref[:, vs] = v.astype(BF16)
        if p < 2:
            rows4_ref[pl.ds(2 * p, tm, stride=4), :] = k
            rows4_ref[pl.ds(2 * p + 1, tm, stride=4), :] = v
        else:
            rows2_ref[pl.ds(0, tm, stride=2), :] = k
            rows2_ref[pl.ds(1, tm, stride=2), :] = v
        if p == 0:
            xkv_ref[0] = (k + pe_ref[0]).astype(BF16)
            xkv_ref[1] = (v + pe_ref[1]).astype(BF16)


def _rope(proj32, cos2, sin2, pe_t, tm):
    M = proj32.shape[0]
    qb = C_Q // GROUP_WIDTH
    kb = C_KV // (2 * HEAD_DIM)
    row = lambda i: (i, 0)
    return pl.pallas_call(
        _rope_kernel,
        grid=(M // tm,),
        in_specs=[pl.BlockSpec((tm, GROUP_WIDTH), lambda i: (i, qb)),
                  pl.BlockSpec((tm, 2 * HEAD_DIM), lambda i: (i, kb)),
                  pl.BlockSpec((tm, 2 * HEAD_DIM), lambda i: (i, kb + 1)),
                  pl.BlockSpec((tm, 2 * HEAD_DIM), lambda i: (i, kb + 2)),
                  pl.BlockSpec((tm, HEAD_DIM), row),
                  pl.BlockSpec((tm, HEAD_DIM), row),
                  pl.BlockSpec((2, tm, HEAD_DIM), lambda i: (0, 0, 0))],
        out_specs=[pl.BlockSpec((tm, GROUP_WIDTH), row),
                   pl.BlockSpec((tm, 6 * HEAD_DIM), row),
                   pl.BlockSpec((tm, 6 * HEAD_DIM), row),
                   pl.BlockSpec((2, tm, HEAD_DIM), lambda i: (0, i, 0)),
                   pl.BlockSpec((4 * tm, HEAD_DIM), row),
                   pl.BlockSpec((2 * tm, HEAD_DIM), row)],
        out_shape=[jax.ShapeDtypeStruct((M, GROUP_WIDTH), BF16),
                   jax.ShapeDtypeStruct((M, 6 * HEAD_DIM), F32),
                   jax.ShapeDtypeStruct((M, 6 * HEAD_DIM), BF16),
                   jax.ShapeDtypeStruct((2, M, HEAD_DIM), BF16),
                   jax.ShapeDtypeStruct((4 * M, HEAD_DIM), F32),
                   jax.ShapeDtypeStruct((2 * M, HEAD_DIM), F32)],
        compiler_params=_cparams(("parallel",)),
        name="rope",
    )(proj32, proj32, proj32, proj32, cos2, sin2, pe_t)


def _gelu_tanh(x):
    return x * (0.5 * (1.0 + jnp.tanh(math.sqrt(2.0 / math.pi) * (x + 0.044715 * (x * x * x)))))


def _cmp_kernel(x_ref, w1_ref, w2_ref, o_ref):
    h = _gelu_tanh(_dot(x_ref[0], w1_ref[0]))
    o_ref[0] = _dot(h.astype(BF16), w2_ref[0])


def _compress(x16, w1, w2, tm):
    _, R, K = x16.shape
    d = w1.shape[2]
    return pl.pallas_call(
        _cmp_kernel,
        grid=(2, R // tm),
        in_specs=[pl.BlockSpec((1, tm, K), lambda s, i: (s, i, 0)),
                  pl.BlockSpec((1, K, d), lambda s, i: (s, 0, 0)),
                  pl.BlockSpec((1, d, d), lambda s, i: (s, 0, 0))],
        out_specs=pl.BlockSpec((1, tm, d), lambda s, i: (s, i, 0)),
        out_shape=jax.ShapeDtypeStruct((2, R, d), F32),
        compiler_params=_cparams(("parallel", "parallel")),
        name="nsa_compress",
    )(x16, w1, w2)


CMP_PITCH = NSA_BLOCK + SUBLANES


def _cmp_rows_kernel(x_ref, w1_ref, w2_ref, o_ref, *, tm):
    acc = jnp.zeros((tm, HEAD_DIM), F32)
    for p in range(0, NSA_BLOCK, 2):
        lhs = jnp.concatenate([x_ref[0, pl.ds(p, tm, stride=CMP_PITCH), :],
                               x_ref[0, pl.ds(p + 1, tm, stride=CMP_PITCH), :]], axis=1).astype(BF16)
        acc = acc + _dot(lhs, w1_ref[0, p * HEAD_DIM:(p + 2) * HEAD_DIM, :])
    o_ref[0] = _dot(_gelu_tanh(acc).astype(BF16), w2_ref[0])


def _compress_rows(x32, w1, w2, tm):
    R = x32.shape[1] // CMP_PITCH
    K, d = w1.shape[1:]
    return pl.pallas_call(
        functools.partial(_cmp_rows_kernel, tm=tm),
        grid=(2, R // tm),
        in_specs=[pl.BlockSpec((1, tm * CMP_PITCH, d), lambda s, i: (s, i, 0)),
                  pl.BlockSpec((1, K, d), lambda s, i: (s, 0, 0)),
                  pl.BlockSpec((1, d, d), lambda s, i: (s, 0, 0))],
        out_specs=pl.BlockSpec((1, tm, d), lambda s, i: (s, i, 0)),
        out_shape=jax.ShapeDtypeStruct((2, R, d), F32),
        compiler_params=_cparams(("parallel", "parallel")),
        name="nsa_compress_rows",
    )(x32, w1, w2)


def _select_blocks(imp, tpos, n_blk, n_real):
    nidx = lax.broadcasted_iota(jnp.int32, imp.shape, 1)
    cur = tpos // NSA_BLOCK
    forced = (nidx == 0) | (nidx == cur) | (nidx == cur - 1)
    started = nidx * NSA_BLOCK <= tpos
    score = jnp.where(forced, jnp.inf, jnp.where(started, imp, -jnp.inf))
    score = jnp.where(nidx < n_real, score, -jnp.inf)
    rank = jnp.zeros(imp.shape, jnp.int32)
    for n2 in range(n_real):
        c = score[:, n2:n2 + 1]
        ahead = (c > score) | ((c == score) & (nidx > n2))
        rank = rank + ahead.astype(jnp.int32)
    return (rank < min(NSA_TOP_N, n_real)) & (nidx < n_real)


def _gate_cols(gs, h):
    return [gs[:, 3 * h + c:3 * h + c + 1] for c in range(3)]


MASKED = -1e30


def _nsa_prompt_kernel(q_ref, ks_ref, vs_ref, kw_ref, vw_ref, kc_ref, vc_ref, g_ref, o_ref, selk_ref, *, T, TQ, KC):
    i = pl.program_id(1)
    n_blk = T // NSA_BLOCK
    R = HEADS * TQ
    t0 = i * TQ
    tpos = t0 + lax.broadcasted_iota(jnp.int32, (TQ, 1), 0)
    tpos_r = t0 + lax.broadcasted_iota(jnp.int32, (R, 1), 0) % TQ
    per_head = lambda x: jnp.concatenate([x] * HEADS, axis=0)
    q = jnp.concatenate([q_ref[:, h * HEAD_DIM:(h + 1) * HEAD_DIM] for h in range(HEADS)], axis=0)
    gs = jax.nn.sigmoid(g_ref[...])

    kc = kc_ref[0].astype(BF16)
    vc = vc_ref[0].astype(BF16)
    cmask = (lax.broadcasted_iota(jnp.int32, (R, n_blk), 1) + 1) * NSA_BLOCK - 1 <= tpos_r
    p_c = _masked_softmax(_dot_t(q, kc) * SCALE, cmask)
    o_cmp = _dot(p_c.astype(BF16), vc)
    imp = p_c[0:TQ]
    for h in range(1, HEADS):
        imp = imp + p_c[h * TQ:(h + 1) * TQ]

    sel = _select_blocks(imp, tpos, n_blk, n_blk)
    expand = (lax.broadcasted_iota(jnp.int32, (n_blk, T), 1) // NSA_BLOCK
              == lax.broadcasted_iota(jnp.int32, (n_blk, T), 0))
    selk_ref[...] = _dot(jnp.where(sel, 1.0, 0.0).astype(BF16), jnp.where(expand, 1.0, 0.0).astype(BF16))

    lane = lax.broadcasted_iota(jnp.int32, (TQ, KC), 1)

    def body(c, carry):
        m, l, acc = carry
        k0 = pl.multiple_of(c * KC, KC)
        keep = (selk_ref[:, pl.ds(k0, KC)] > 0.5) & (k0 + lane <= tpos)
        s = _dot_t(q, ks_ref[pl.ds(k0, KC), :]) * SCALE + per_head(jnp.where(keep, 0.0, MASKED))
        m_new = jnp.maximum(m, jnp.max(s, -1, keepdims=True))
        scale_old = jnp.exp(m - m_new)
        p = jnp.exp(s - m_new)
        l = scale_old * l + jnp.sum(p, -1, keepdims=True)
        acc = scale_old * acc + _dot(p.astype(BF16), vs_ref[pl.ds(k0, KC), :])
        return m_new, l, acc

    n_chunks = (t0 + TQ + KC - 1) // KC
    _, l, acc = lax.fori_loop(0, n_chunks, body, (jnp.full((R, 1), MASKED, F32), jnp.zeros((R, 1), F32),
                                                  jnp.zeros((R, HEAD_DIM), F32)))
    o_sel = acc / l

    band = min(NSA_WINDOW + TQ, T)
    w0 = pl.multiple_of(jnp.clip(t0 - NSA_WINDOW, 0, T - band), LANES)
    wpos = w0 + lax.broadcasted_iota(jnp.int32, (TQ, band), 1)
    keep = (wpos <= tpos) & (wpos > tpos - NSA_WINDOW)
    s = _dot_t(q, kw_ref[pl.ds(w0, band), :]) * SCALE + per_head(jnp.where(keep, 0.0, MASKED))
    p = jnp.exp(s - jnp.max(s, -1, keepdims=True))
    o_win = _dot(p.astype(BF16), vw_ref[pl.ds(w0, band), :]) / jnp.sum(p, -1, keepdims=True)

    for h in range(HEADS):
        g0, g1, g2 = _gate_cols(gs, h)
        rs = slice(h * TQ, (h + 1) * TQ)
        o = g0 * o_cmp[rs] + g1 * o_sel[rs] + g2 * o_win[rs]
        o_ref[:, h * HEAD_DIM:(h + 1) * HEAD_DIM] = o.astype(BF16)


def _nsa_prompt(q16, kv16, kcmp, vcmp, proj32, B, T):
    TQ = min(128, T)
    KC = min(256, T)
    nq = T // TQ
    n_blk = T // NSA_BLOCK
    kvcol = lambda c: pl.BlockSpec((T, HEAD_DIM), lambda b, i: (b, c))
    return pl.pallas_call(
        functools.partial(_nsa_prompt_kernel, T=T, TQ=TQ, KC=KC),
        grid=(B, nq),
        scratch_shapes=[pltpu.VMEM((TQ, T), F32)],
        in_specs=[pl.BlockSpec((TQ, GROUP_WIDTH), lambda b, i: (b * nq + i, 0)),
                  kvcol(2), kvcol(3), kvcol(4), kvcol(5),
                  pl.BlockSpec((1, n_blk, HEAD_DIM), lambda b, i: (b, 0, 0)),
                  pl.BlockSpec((1, n_blk, HEAD_DIM), lambda b, i: (b, 0, 0)),
                  pl.BlockSpec((TQ, LANES), lambda b, i: (b * nq + i, C_G // LANES))],
        out_specs=pl.BlockSpec((TQ, GROUP_WIDTH), lambda b, i: (b * nq + i, 0)),
        out_shape=jax.ShapeDtypeStruct((B * T, GROUP_WIDTH), BF16),
        compiler_params=_cparams(("parallel", "arbitrary")),
        name="nsa_prompt",
    )(q16, kv16, kv16, kv16, kv16, kcmp, vcmp, proj32)


def _rows(ref, r, n, stride):
    return ref[0, 0, pl.ds(r, n, stride=stride), :]


def _gather_nsa_kernel(pt_ref, pe_ref, *refs, n_pages):
    pages = refs[:n_pages]
    xkv_ref, ksv_ref = refs[n_pages:]
    blocks_per_page = PAGE_SIZE // NSA_BLOCK
    pad = jnp.zeros((CMP_PITCH - NSA_BLOCK, HEAD_DIM), F32)
    for j in range(n_pages):
        sl = slice(j * PAGE_SIZE, (j + 1) * PAGE_SIZE)
        ksv_ref[0, 0, sl, :] = _rows(pages[j], 2, PAGE_SIZE, 4).astype(BF16)
        ksv_ref[1, 0, sl, :] = _rows(pages[j], 3, PAGE_SIZE, 4).astype(BF16)
        for s in range(2):
            x = _rows(pages[j], s, PAGE_SIZE, 4)
            for u in range(blocks_per_page):
                r0 = (j * blocks_per_page + u) * CMP_PITCH
                xkv_ref[s, r0:r0 + NSA_BLOCK, :] = x[u * NSA_BLOCK:(u + 1) * NSA_BLOCK] + pe_ref[s]
                xkv_ref[s, r0 + NSA_BLOCK:r0 + CMP_PITCH, :] = pad


def _gather_nsa(cache, layer, page_table, pe):
    B, n_pages = page_table.shape
    P = n_pages * PAGE_SIZE
    rows_b = (P // NSA_BLOCK) * CMP_PITCH

    def page_spec(j):
        return pl.BlockSpec((1, 1, PAGE_SIZE * 4, HEAD_DIM), lambda b, pt: (layer, pt[b, j], 0, 0))

    out = pl.BlockSpec((2, 1, P, HEAD_DIM), lambda b, pt: (0, b, 0, 0))
    return pl.pallas_call(
        functools.partial(_gather_nsa_kernel, n_pages=n_pages),
        grid_spec=pltpu.PrefetchScalarGridSpec(
            num_scalar_prefetch=1, grid=(B,),
            in_specs=[pl.BlockSpec((2, NSA_BLOCK, HEAD_DIM), lambda b, pt: (0, 0, 0))]
            + [page_spec(j) for j in range(n_pages)],
            out_specs=[pl.BlockSpec((2, rows_b, HEAD_DIM), lambda b, pt: (0, b, 0)), out]),
        out_shape=[jax.ShapeDtypeStruct((2, B * rows_b, HEAD_DIM), F32),
                   jax.ShapeDtypeStruct((2, B, P, HEAD_DIM), BF16)],
        compiler_params=_cparams(("arbitrary",)),
        name="nsa_gather",
    )(page_table, pe, *([cache] * n_pages))


def _nsa_sample_kernel(q_ref, kvn_ref, win_ref, kc_ref, vc_ref, g_ref, ks_ref, vs_ref, o_ref, *, T, P, n_blk, n_pad):
    R = HEADS * T
    row = lax.broadcasted_iota(jnp.int32, (R, 1), 0)
    tq = row % T
    tpos = P + tq
    q_bt = q_ref[0]
    q = jnp.concatenate([q_bt[:, h * HEAD_DIM:(h + 1) * HEAD_DIM] for h in range(HEADS)], axis=0).astype(BF16)
    gs = jax.nn.sigmoid(g_ref[0])

    kc = kc_ref[0].astype(BF16)
    vc = vc_ref[0].astype(BF16)
    nidx = lax.broadcasted_iota(jnp.int32, (R, n_pad), 1)
    cmask = ((nidx + 1) * NSA_BLOCK - 1 <= tpos) & (nidx < n_blk)
    p_c = _masked_softmax(_dot_t(q, kc) * SCALE, cmask)
    o_cmp = _dot(p_c.astype(BF16), vc)
    imp = p_c[0:T]
    for h in range(1, HEADS):
        imp = imp + p_c[h * T:(h + 1) * T]
    sel_t = _select_blocks(imp, P + lax.broadcasted_iota(jnp.int32, (T, 1), 0), n_pad, n_blk)
    sel = jnp.concatenate([jnp.where(sel_t, 1.0, 0.0)] * HEADS, axis=0)

    lane = lax.broadcasted_iota(jnp.int32, (R, PAGE_SIZE), 1)
    kvn = kvn_ref[0]
    pad_rows = jnp.zeros((PAGE_SIZE - T, HEAD_DIM), F32)
    new_chunk = lambda c: jnp.concatenate([kvn[:, c * HEAD_DIM:(c + 1) * HEAD_DIM], pad_rows], axis=0).astype(BF16)
    ks_new, vs_new, kw_new, vw_new = new_chunk(2), new_chunk(3), new_chunk(4), new_chunk(5)
    expand = (lax.broadcasted_iota(jnp.int32, (n_pad, P), 1) // NSA_BLOCK
              == lax.broadcasted_iota(jnp.int32, (n_pad, P), 0))
    selk = _dot(sel.astype(BF16), jnp.where(expand, 1.0, 0.0).astype(BF16))
    m_past = (selk > 0.5) & (lax.broadcasted_iota(jnp.int32, (R, P), 1) <= tpos)
    s_past = _dot_t(q, ks_ref[0, 0]) * SCALE
    n_past = P // NSA_BLOCK
    chosen = jnp.zeros((R, PAGE_SIZE), F32)
    for u in range(PAGE_SIZE // NSA_BLOCK):
        if n_past + u < n_blk:
            in_blk = (lane >= u * NSA_BLOCK) & (lane < (u + 1) * NSA_BLOCK)
            chosen = jnp.where(in_blk, sel[:, n_past + u:n_past + u + 1], chosen)
    m_new = (chosen > 0.5) & (P + lane <= tpos)
    s_new = _dot_t(q, ks_new) * SCALE
    o_sel = _multi_chunk_attention([s_past, s_new], [m_past, m_new], [vs_ref[0, 0], vs_new])

    W = win_ref.shape[2] // 2
    kw = _rows(win_ref, 0, W, 2).astype(BF16)
    vw = _rows(win_ref, 1, W, 2).astype(BF16)
    wpos = P - W + lax.broadcasted_iota(jnp.int32, (R, W), 1)
    m_w = (wpos <= tpos) & (wpos > tpos - NSA_WINDOW) & (wpos >= 0)
    m_wn = (P + lane <= tpos) & (P + lane > tpos - NSA_WINDOW)
    o_win = _multi_chunk_attention([_dot_t(q, kw) * SCALE, _dot_t(q, kw_new) * SCALE], [m_w, m_wn], [vw, vw_new])

    for h in range(HEADS):
        g0, g1, g2 = _gate_cols(gs, h)
        rs = slice(h * T, (h + 1) * T)
        o_ref[0, :, h * HEAD_DIM:(h + 1) * HEAD_DIM] = g0 * o_cmp[rs] + g1 * o_sel[rs] + g2 * o_win[rs]


def _multi_chunk_attention(s_list, m_list, v_list):
    sm = [jnp.where(m, s, -1e30) for s, m in zip(s_list, m_list)]
    mx = jnp.max(sm[0], -1, keepdims=True)
    for s in sm[1:]:
        mx = jnp.maximum(mx, jnp.max(s, -1, keepdims=True))
    es = [jnp.where(m, jnp.exp(s - mx), 0.0) for s, m in zip(sm, m_list)]
    tot = jnp.sum(es[0], -1, keepdims=True)
    for e in es[1:]:
        tot = tot + jnp.sum(e, -1, keepdims=True)
    den = jnp.maximum(tot, jnp.finfo(F32).tiny)
    acc = _dot(es[0].astype(BF16), v_list[0])
    for e, v in zip(es[1:], v_list[1:]):
        acc = acc + _dot(e.astype(BF16), v)
    return acc / den


def _nsa_sample(q_b, kv_b, win_all, layer, kcmp, vcmp, g32_b, ksv):
    B, T, _ = q_b.shape
    P = ksv.shape[2]
    n_blk = -(-(P + T) // NSA_BLOCK)
    n_pad = kcmp.shape[1]
    W2 = win_all.shape[2]
    per_b = lambda *tail: pl.BlockSpec((1,) + tail, lambda b: (b,) + (0,) * len(tail))
    return pl.pallas_call(
        functools.partial(_nsa_sample_kernel, T=T, P=P, n_blk=n_blk, n_pad=n_pad),
        grid=(B,),
        in_specs=[per_b(T, GROUP_WIDTH), per_b(T, 6 * HEAD_DIM),
                  pl.BlockSpec((1, 1, W2, HEAD_DIM), lambda b: (layer, b, 0, 0)),
                  per_b(n_pad, HEAD_DIM), per_b(n_pad, HEAD_DIM), per_b(T, LANES),
                  pl.BlockSpec((1, 1, P, HEAD_DIM), lambda b: (0, b, 0, 0)),
                  pl.BlockSpec((1, 1, P, HEAD_DIM), lambda b: (1, b, 0, 0))],
        out_specs=per_b(T, GROUP_WIDTH),
        out_shape=jax.ShapeDtypeStruct((B, T, GROUP_WIDTH), F32),
        compiler_params=_cparams(("parallel",)),
        name="nsa_sample",
    )(q_b, kv_b, win_all, kcmp, vcmp, g32_b, ksv, ksv)


def _log_sigmoid_pair(z):
    ls_pos = jnp.minimum(z, 0.0) - jnp.log1p(jnp.exp(-jnp.abs(z)))
    return ls_pos, ls_pos - z


def _strict_upper_ones(n):
    r = lax.broadcasted_iota(jnp.int32, (n, n), 0)
    c = lax.broadcasted_iota(jnp.int32, (n, n), 1)
    return jnp.where(r > c, 1.0, 0.0).astype(BF16)


SB_HEADS_PER_STEP = 1


def _sb_prompt_kernel(q_ref, k_ref, v_ref, o_ref, *, TQ, KC):
    qi = pl.program_id(2)
    t0 = qi * TQ
    later = _strict_upper_ones(KC)
    tpos = t0 + lax.broadcasted_iota(jnp.int32, (TQ, 1), 0)
    lane = lax.broadcasted_iota(jnp.int32, (TQ, KC), 1)
    head_cols = [slice(h * HEAD_DIM, (h + 1) * HEAD_DIM) for h in range(SB_HEADS_PER_STEP)]

    def chunk(k0, carries, diagonal):
        out = []
        for hc, (acc, after_c) in zip(head_cols, carries):
            z = _dot_t(q_ref[:, hc], k_ref[pl.ds(k0, KC), hc]) * SCALE
            e = jnp.exp(-jnp.abs(z))
            log_keep = -(jnp.maximum(z, 0.0) + jnp.log(1.0 + e))
            r = 1.0 / (1.0 + e)
            beta = jnp.where(z >= 0.0, r, e * r)
            if diagonal:
                mask = k0 + lane < tpos
                log_keep = jnp.where(mask, log_keep, 0.0)
                beta = jnp.where(mask, beta, 0.0)
            a = beta * jnp.exp(_split_dot(log_keep, later) + after_c)
            acc = acc + _dot(a.astype(BF16), v_ref[pl.ds(k0, KC), hc])
            out.append((acc, after_c + jnp.sum(log_keep, -1, keepdims=True)))
        return tuple(out)

    carries = tuple((jnp.zeros((TQ, HEAD_DIM), F32), jnp.zeros((TQ, 1), F32)) for _ in head_cols)
    n_diag = TQ // KC
    for d in range(n_diag):
        carries = chunk(pl.multiple_of(t0 + (n_diag - 1 - d) * KC, KC), carries, True)
    n_full = t0 // KC
    carries = lax.fori_loop(0, n_full, lambda c, cr: chunk(pl.multiple_of((n_full - 1 - c) * KC, KC), cr, False),
                            carries)
    for hc, (acc, _) in zip(head_cols, carries):
        o_ref[:, hc] = acc.astype(BF16)


def _sb_prompt(proj16, B, T):
    TQ = min(512, T)
    KC = min(256, T)
    nq = T // TQ
    width = SB_HEADS_PER_STEP * HEAD_DIM
    col = lambda base: (lambda b, h, i: (b, base // width + h))
    return pl.pallas_call(
        functools.partial(_sb_prompt_kernel, TQ=TQ, KC=KC),
        grid=(B, HEADS // SB_HEADS_PER_STEP, nq),
        in_specs=[pl.BlockSpec((TQ, width), lambda b, h, i: (b * nq + i, C_SQ // width + h)),
                  pl.BlockSpec((T, width), col(C_SK)),
                  pl.BlockSpec((T, width), col(C_SV))],
        out_specs=pl.BlockSpec((TQ, width), lambda b, h, i: (b * nq + i, h)),
        out_shape=jax.ShapeDtypeStruct((B * T, GROUP_WIDTH), BF16),
        compiler_params=_cparams(("parallel", "parallel", "arbitrary")),
        name="sb_prompt",
    )(proj16, proj16, proj16)


def _sb_sample_kernel(pt_ref, qkv_ref, *refs, T, P, n_pages):
    pages = refs[:n_pages]
    o_ref = refs[n_pages]
    R = HEADS * T
    C = PAGE_SIZE
    head = lax.broadcasted_iota(jnp.int32, (R, 1), 0) // T
    qkv = qkv_ref[0]
    q_t = qkv[:, :GROUP_WIDTH]
    col_head = lax.broadcasted_iota(jnp.int32, (R, GROUP_WIDTH), 1) // HEAD_DIM
    q_bd = jnp.where(col_head == head, jnp.concatenate([q_t] * HEADS, axis=0), 0.0).astype(BF16)
    pad_rows = jnp.zeros((C - T, GROUP_WIDTH), F32)
    k_new = jnp.concatenate([qkv[:, GROUP_WIDTH:2 * GROUP_WIDTH], pad_rows], axis=0).astype(BF16)
    v_new = jnp.concatenate([qkv[:, 2 * GROUP_WIDTH:], pad_rows], axis=0).astype(BF16)

    n_ch = n_pages + 1
    zs, vs = [], []
    for j in range(n_ch):
        if j < n_pages:
            kj = jnp.concatenate([_rows(pages[j], h, C, 2 * HEADS) for h in range(HEADS)], axis=1).astype(BF16)
            vj = jnp.concatenate([_rows(pages[j], HEADS + h, C, 2 * HEADS) for h in range(HEADS)], axis=1).astype(BF16)
        else:
            kj, vj = k_new, v_new
        zs.append(_dot_t(q_bd, kj) * SCALE)
        vs.append(vj)
    z = jnp.concatenate(zs, axis=0)
    crow = lax.broadcasted_iota(jnp.int32, (n_ch * R, 1), 0)
    kpos = (crow // R) * C + lax.broadcasted_iota(jnp.int32, (n_ch * R, C), 1)
    mask = kpos < P + (crow % R) % T
    ls_pos, ls_neg = _log_sigmoid_pair(z)
    log_keep = jnp.where(mask, ls_neg, 0.0)
    after_local = _split_dot(log_keep, _strict_upper_ones(C))
    tot = jnp.sum(log_keep, -1, keepdims=True)
    carry = jnp.zeros((R, 1), F32)
    carries = [None] * n_ch
    for j in range(n_ch - 1, -1, -1):
        carries[j] = carry
        carry = carry + tot[j * R:(j + 1) * R]
    after = after_local + jnp.concatenate(carries, axis=0)
    a = jnp.where(mask, jnp.exp(ls_pos + after), 0.0).astype(BF16)
    acc = jnp.zeros((R, GROUP_WIDTH), F32)
    for j in range(n_ch):
        acc = acc + _dot(a[j * R:(j + 1) * R], vs[j])
    for h in range(HEADS):
        o_ref[0, :, h * HEAD_DIM:(h + 1) * HEAD_DIM] = acc[h * T:(h + 1) * T, h * HEAD_DIM:(h + 1) * HEAD_DIM]


def _sb_sample(qkv_b, cache_sb, layer, page_table):
    B, T, _ = qkv_b.shape
    n_pages = page_table.shape[1]
    P = n_pages * PAGE_SIZE

    def page_spec(j):
        return pl.BlockSpec((1, 1, PAGE_SIZE * 2 * HEADS, HEAD_DIM), lambda b, pt: (layer, pt[b, j], 0, 0))

    return pl.pallas_call(
        functools.partial(_sb_sample_kernel, T=T, P=P, n_pages=n_pages),
        grid_spec=pltpu.PrefetchScalarGridSpec(
            num_scalar_prefetch=1, grid=(B,),
            in_specs=[pl.BlockSpec((1, T, 3 * GROUP_WIDTH), lambda b, pt: (b, 0, 0))]
            + [page_spec(j) for j in range(n_pages)],
            out_specs=pl.BlockSpec((1, T, GROUP_WIDTH), lambda b, pt: (b, 0, 0))),
        out_shape=jax.ShapeDtypeStruct((B, T, GROUP_WIDTH), F32),
        compiler_params=_cparams(("arbitrary",)),
        name="sb_sample",
    )(page_table, qkv_b, *([cache_sb] * n_pages))


CONV_HIST = 32
CONV_ROWS = 32


def _put_rows(ref, slab, row, x):
    ref[slab, pl.ds(2 * row, x.shape[0], stride=2), :] = x


def _get_rows(ref, slab, row, n):
    return ref[slab, pl.ds(2 * row, n, stride=2), :]


def _conv_prompt_kernel(ab_ref, w_ref, b_ref, g_ref, be_ref, o_ref, tail_ref, buf_ref, *, tm):
    i = pl.program_id(1)
    n_slab = GROUP_WIDTH // LANES

    @pl.when(i == 0)
    def _():
        for c in range(n_slab):
            _put_rows(buf_ref, c, 0, jnp.zeros((CONV_HIST, LANES), F32))

    @pl.when(i > 0)
    def _():
        for c in range(n_slab):
            _put_rows(buf_ref, c, 0, _get_rows(buf_ref, c, tm, CONV_HIST))

    a = ab_ref[:, :GROUP_WIDTH]
    glu = a * jax.nn.sigmoid(ab_ref[:, GROUP_WIDTH:])
    for c in range(n_slab):
        _put_rows(buf_ref, c, CONV_HIST, glu[:, c * LANES:(c + 1) * LANES])
    tail_ref[0] = glu[tm - CONV_HIST:, :]
    w = w_ref[...]
    bias = b_ref[...]
    for r in range(tm // CONV_ROWS):
        slabs = []
        for c in range(n_slab):
            ls = slice(c * LANES, (c + 1) * LANES)
            acc = jnp.broadcast_to(bias[:, ls], (CONV_ROWS, LANES))
            for k in range(CONV_WIDTH):
                start = CONV_HIST + r * CONV_ROWS - (CONV_WIDTH - 1) + k
                acc = acc + w[k:k + 1, ls] * _get_rows(buf_ref, c, start, CONV_ROWS)
            slabs.append(acc)
        y = _layer_norm(jnp.concatenate(slabs, axis=1), g_ref[...], be_ref[...])
        o_ref[r * CONV_ROWS:(r + 1) * CONV_ROWS, :] = (y * jax.nn.sigmoid(y)).astype(BF16)


def _conv_prompt(proj32, w, b, g, be, B, T):
    tm = min(256, T)
    nt = T // tm
    vec = pl.BlockSpec((1, GROUP_WIDTH), lambda bb, i: (0, 0))
    return pl.pallas_call(
        functools.partial(_conv_prompt_kernel, tm=tm),
        grid=(B, nt),
        in_specs=[pl.BlockSpec((tm, 2 * GROUP_WIDTH), lambda bb, i: (bb * nt + i, C_CONV // (2 * GROUP_WIDTH))),
                  pl.BlockSpec((CONV_WIDTH, GROUP_WIDTH), lambda bb, i: (0, 0)), vec, vec, vec],
        out_specs=[pl.BlockSpec((tm, GROUP_WIDTH), lambda bb, i: (bb * nt + i, 0)),
                   pl.BlockSpec((1, CONV_HIST, GROUP_WIDTH), lambda bb, i: (bb, 0, 0))],
        out_shape=[jax.ShapeDtypeStruct((B * T, GROUP_WIDTH), BF16),
                   jax.ShapeDtypeStruct((B, CONV_HIST, GROUP_WIDTH), F32)],
        scratch_shapes=[pltpu.VMEM((GROUP_WIDTH // LANES, 2 * (CONV_HIST + tm), LANES), F32)],
        compiler_params=_cparams(("parallel", "arbitrary")),
        name="conv_prompt",
    )(proj32, w, b, g, be)


def _conv_sample_kernel(st_ref, ab_ref, w_ref, b_ref, g_ref, be_ref, o_ref, glu_ref, *, T):
    S = CONV_WIDTH - 1
    w = w_ref[...]
    glu = []
    for t in range(T):
        ab = ab_ref[t]
        gt = ab[:, :GROUP_WIDTH] * jax.nn.sigmoid(ab[:, GROUP_WIDTH:])
        glu_ref[t] = gt
        glu.append(gt)
    for t in range(T):
        acc = jnp.broadcast_to(b_ref[...], glu[0].shape)
        for k in range(CONV_WIDTH):
            j = t + k
            acc = acc + w[k:k + 1, :] * (st_ref[j] if j < S else glu[j - S])
        y = _layer_norm(acc, g_ref[...], be_ref[...])
        o_ref[t] = (y * jax.nn.sigmoid(y)).astype(BF16)


def _conv_sample(state_tm, proj32_3d, w, b, g, be):
    S, B, _ = state_tm.shape
    T = proj32_3d.shape[0]
    bb = min(32, B)
    vec = pl.BlockSpec((1, GROUP_WIDTH), lambda i: (0, 0))
    out = pl.BlockSpec((T, bb, GROUP_WIDTH), lambda i: (0, i, 0))
    return pl.pallas_call(
        functools.partial(_conv_sample_kernel, T=T),
        grid=(B // bb,),
        in_specs=[pl.BlockSpec((S, bb, GROUP_WIDTH), lambda i: (0, i, 0)),
                  pl.BlockSpec((T, bb, 2 * GROUP_WIDTH), lambda i: (0, i, C_CONV // (2 * GROUP_WIDTH))),
                  pl.BlockSpec((CONV_WIDTH, GROUP_WIDTH), lambda i: (0, 0)), vec, vec, vec],
        out_specs=[out, out],
        out_shape=[jax.ShapeDtypeStruct((T, B, GROUP_WIDTH), BF16),
                   jax.ShapeDtypeStruct((T, B, GROUP_WIDTH), F32)],
        compiler_params=_cparams(("parallel",)),
        name="conv_sample",
    )(state_tm, proj32_3d, w, b, g, be)


def _gm_prompt_kernel(uv_ref, ws_ref, bias_ref, g_ref, be_ref, o_ref, *, cs):
    vn = _layer_norm(uv_ref[:, GROUP_WIDTH:], g_ref[...], be_ref[...])
    r = lax.broadcasted_iota(jnp.int32, (cs, cs), 0)
    c = lax.broadcasted_iota(jnp.int32, (cs, cs), 1)
    for gi in range(HEADS):
        sl = slice(gi * HEAD_DIM, (gi + 1) * HEAD_DIM)
        w = jnp.where(r >= c, ws_ref[gi], 0.0).astype(BF16)
        s = _dot(w, vn[:, sl].astype(BF16)) + bias_ref[:, gi:gi + 1]
        o_ref[:, sl] = (uv_ref[:, sl] * s).astype(BF16)


def _gm_prompt(proj32, ws, bias_t, g, be, B, T):
    cs = min(GM_CHUNK, T)
    n = B * T // cs
    vec = pl.BlockSpec((1, GROUP_WIDTH), lambda i: (0, 0))
    return pl.pallas_call(
        functools.partial(_gm_prompt_kernel, cs=cs),
        grid=(n,),
        in_specs=[pl.BlockSpec((cs, 2 * GROUP_WIDTH), lambda i: (i, C_GM // (2 * GROUP_WIDTH))),
                  pl.BlockSpec((HEADS, cs, cs), lambda i: (0, 0, 0)),
                  pl.BlockSpec((cs, HEADS), lambda i: (0, 0)), vec, vec],
        out_specs=pl.BlockSpec((cs, GROUP_WIDTH), lambda i: (i, 0)),
        out_shape=jax.ShapeDtypeStruct((B * T, GROUP_WIDTH), BF16),
        compiler_params=_cparams(("parallel",)),
        name="gm_prompt",
    )(proj32, ws, bias_t, g, be)


def _gm_sample_kernel(uv_ref, wexp_ref, bexp_ref, g_ref, be_ref, o_ref, vn_ref, *, T):
    vn = []
    for t in range(T):
        v = _layer_norm(uv_ref[t][:, GROUP_WIDTH:], g_ref[...], be_ref[...])
        vn_ref[t] = v
        vn.append(v)
    for i in range(T):
        s = jnp.broadcast_to(bexp_ref[i:i + 1, :], vn[0].shape)
        for j in range(i + 1):
            s = s + wexp_ref[i, j:j + 1, :] * vn[j]
        o_ref[i] = (uv_ref[i][:, :GROUP_WIDTH] * s).astype(BF16)


def _gm_sample(proj32_3d, wexp, bexp, g, be):
    T, B, _ = proj32_3d.shape
    bb = min(32, B)
    vec = pl.BlockSpec((1, GROUP_WIDTH), lambda i: (0, 0))
    out = pl.BlockSpec((T, bb, GROUP_WIDTH), lambda i: (0, i, 0))
    return pl.pallas_call(
        functools.partial(_gm_sample_kernel, T=T),
        grid=(B // bb,),
        in_specs=[pl.BlockSpec((T, bb, 2 * GROUP_WIDTH), lambda i: (0, i, C_GM // (2 * GROUP_WIDTH))),
                  pl.BlockSpec((T, T, GROUP_WIDTH), lambda i: (0, 0, 0)),
                  pl.BlockSpec((T, GROUP_WIDTH), lambda i: (0, 0)), vec, vec],
        out_specs=[out, out],
        out_shape=[jax.ShapeDtypeStruct((T, B, GROUP_WIDTH), BF16),
                   jax.ShapeDtypeStruct((T, B, GROUP_WIDTH), F32)],
        compiler_params=_cparams(("parallel",)),
        name="gm_sample",
    )(proj32_3d, wexp, bexp, g, be)


def _outproj_kernel(a0_ref, a1_ref, a2_ref, a3_ref, w_ref, x_ref, g_ref, b_ref, h32_ref, h16_ref):
    acc = _dot(a0_ref[...], w_ref[0:GROUP_WIDTH, :])
    for n, a_ref in enumerate((a1_ref, a2_ref, a3_ref), start=1):
        acc = acc + _dot(a_ref[...], w_ref[n * GROUP_WIDTH:(n + 1) * GROUP_WIDTH, :])
    h = _layer_norm(ALPHA * x_ref[...] + acc, g_ref[...], b_ref[...])
    h32_ref[...] = h
    h16_ref[...] = h.astype(BF16)


def _outproj(mix, w16, x32, g, b, tm):
    M = x32.shape[0]
    row = lambda i: (i, 0)
    vec = pl.BlockSpec((1, D_MODEL), lambda i: (0, 0))
    return pl.pallas_call(
        _outproj_kernel,
        grid=(M // tm,),
        in_specs=[pl.BlockSpec((tm, GROUP_WIDTH), row)] * 4
        + [pl.BlockSpec((D_MODEL, D_MODEL), lambda i: (0, 0)), pl.BlockSpec((tm, D_MODEL), row), vec, vec],
        out_specs=[pl.BlockSpec((tm, D_MODEL), row)] * 2,
        out_shape=[jax.ShapeDtypeStruct((M, D_MODEL), F32), jax.ShapeDtypeStruct((M, D_MODEL), BF16)],
        compiler_params=_cparams(("parallel",)),
        name="outproj_ln1",
    )(*mix, w16, x32, g, b)


FFN_TF = 512
FFN_CARRY = SUBLANES
FFN_SUB = 256
FFN_DOWN_TM = 256


def _ffn_up_kernel(h_ref, wa_ref, wb_ref, cwa_ref, cwb_ref, cba_ref, cbb_ref, *refs, shift, tiles_per_seq, tm):
    if shift == 1:
        g_ref, ta_ref, tb_ref, ca_ref, cb_ref, sh_ref = refs
    else:
        sa_ref, sb_ref, g_ref, ta_ref, tb_ref = refs
    m = pl.program_id(0)
    f = pl.program_id(1)
    h = h_ref[...]

    def conv3(cw, cb, p2, p1, x):
        return cw[0:1, :] * p2 + cw[1:2, :] * p1 + cw[2:3, :] * x + cb

    def gate(a, b):
        return (a * jax.nn.sigmoid(a) * b).astype(BF16)

    if shift == 1:
        @pl.when(m % tiles_per_seq == 0)
        def _():
            ca_ref[f] = jnp.zeros((FFN_CARRY, FFN_TF), F32)
            cb_ref[f] = jnp.zeros((FFN_CARRY, FFN_TF), F32)

    for c in range(FFN_TF // FFN_SUB):
        cs = slice(c * FFN_SUB, (c + 1) * FFN_SUB)
        body = []
        for part in range(2):
            up = _dot(h, (wa_ref, wb_ref)[part][:, cs])
            cw = (cwa_ref, cwb_ref)[part][:, cs]
            cb = (cba_ref, cbb_ref)[part][:, cs]
            if shift == 1:
                carry_ref = (ca_ref, cb_ref)[part]
                prev = carry_ref[f][:, cs]
                slabs = []
                for s in range(FFN_SUB // LANES):
                    ls = slice(s * LANES, (s + 1) * LANES)
                    slab = part * (FFN_SUB // LANES) + s
                    _put_rows(sh_ref, slab, 0, prev[:, ls])
                    _put_rows(sh_ref, slab, FFN_CARRY, up[:, ls])
                    slabs.append(conv3(cw[:, ls], cb[:, ls], _get_rows(sh_ref, slab, FFN_CARRY - 2, tm),
                                       _get_rows(sh_ref, slab, FFN_CARRY - 1, tm), up[:, ls]))
                body.append(jnp.concatenate(slabs, axis=1))
                tail_rows = up[tm - FFN_CARRY:, :]
                carry_ref[f, :, cs] = tail_rows
                (ta_ref, tb_ref)[part][0, :, cs] = tail_rows
            else:
                st = (sa_ref, sb_ref)[part][:, cs]
                p1 = jnp.concatenate([st[shift:], up[:tm - shift]], axis=0)
                p2 = jnp.concatenate([st, up[:tm - 2 * shift]], axis=0)
                (ta_ref, tb_ref)[part][:, cs] = up[tm - 2 * shift:, :]
                body.append(conv3(cw, cb, p2, p1, up))
        g_ref[:, cs] = gate(body[0], body[1])


def _ffn_up(h16, wup16, cw, cb, state_tm, *, shift, seq_len):
    M = h16.shape[0]
    nf = D_FF // FFN_TF
    wa = pl.BlockSpec((D_MODEL, FFN_TF), lambda m, f: (0, f))
    wb = pl.BlockSpec((D_MODEL, FFN_TF), lambda m, f: (0, f + nf))
    cwa = pl.BlockSpec((FFN_CONV_WIDTH, FFN_TF), lambda m, f: (0, f))
    cwb = pl.BlockSpec((FFN_CONV_WIDTH, FFN_TF), lambda m, f: (0, f + nf))
    cba = pl.BlockSpec((1, FFN_TF), lambda m, f: (0, f))
    cbb = pl.BlockSpec((1, FFN_TF), lambda m, f: (0, f + nf))
    if shift == 1:
        tm = min(512, seq_len)
        nm = M // tm
        kern = functools.partial(_ffn_up_kernel, shift=1, tiles_per_seq=seq_len // tm, tm=tm)
        tail = pl.BlockSpec((1, FFN_CARRY, FFN_TF), lambda m, f: (m, 0, f))
        return pl.pallas_call(
            kern, grid=(nm, nf),
            in_specs=[pl.BlockSpec((tm, D_MODEL), lambda m, f: (m, 0)), wa, wb, cwa, cwb, cba, cbb],
            out_specs=[pl.BlockSpec((tm, FFN_TF), lambda m, f: (m, f)), tail, tail],
            out_shape=[jax.ShapeDtypeStruct((M, D_FF), BF16),
                       jax.ShapeDtypeStruct((nm, FFN_CARRY, D_FF), F32),
                       jax.ShapeDtypeStruct((nm, FFN_CARRY, D_FF), F32)],
            scratch_shapes=[pltpu.VMEM((nf, FFN_CARRY, FFN_TF), F32), pltpu.VMEM((nf, FFN_CARRY, FFN_TF), F32),
                            pltpu.VMEM((2 * FFN_SUB // LANES, 2 * (FFN_CARRY + tm), LANES), F32)],
            compiler_params=_cparams(("arbitrary", "arbitrary")),
            name="ffn_up_prompt",
        )(h16, wup16, wup16, cw, cw, cb, cb)
    tm = M
    kern = functools.partial(_ffn_up_kernel, shift=shift, tiles_per_seq=1, tm=tm)
    sa = pl.BlockSpec((2 * shift, FFN_TF), lambda m, f: (0, f))
    sb = pl.BlockSpec((2 * shift, FFN_TF), lambda m, f: (0, f + nf))
    tail = pl.BlockSpec((2 * shift, FFN_TF), lambda m, f: (0, f))
    return pl.pallas_call(
        kern, grid=(1, nf),
        in_specs=[pl.BlockSpec((tm, D_MODEL), lambda m, f: (0, 0)), wa, wb, cwa, cwb, cba, cbb, sa, sb],
        out_specs=[pl.BlockSpec((tm, FFN_TF), lambda m, f: (0, f)), tail, tail],
        out_shape=[jax.ShapeDtypeStruct((M, D_FF), BF16),
                   jax.ShapeDtypeStruct((2 * shift, D_FF), F32),
                   jax.ShapeDtypeStruct((2 * shift, D_FF), F32)],
        compiler_params=_cparams(("arbitrary", "arbitrary")),
        name="ffn_up_sample",
    )(h16, wup16, wup16, cw, cw, cb, cb, state_tm, state_tm)


def _ffn_down_kernel(g_ref, w_ref, h_ref, ln_g_ref, ln_b_ref, y32_ref, y16_ref):
    y = _layer_norm(ALPHA * h_ref[...] + _dot(g_ref[...], w_ref[...]), ln_g_ref[...], ln_b_ref[...])
    y32_ref[...] = y
    y16_ref[...] = y.astype(BF16)


def _ffn_down(g16, wdown16, h32, ln_g, ln_b, tm):
    M = h32.shape[0]
    vec = pl.BlockSpec((1, D_MODEL), lambda m: (0, 0))
    row = pl.BlockSpec((tm, D_MODEL), lambda m: (m, 0))
    return pl.pallas_call(
        _ffn_down_kernel,
        grid=(M // tm,),
        in_specs=[pl.BlockSpec((tm, D_FF), lambda m: (m, 0)),
                  pl.BlockSpec((D_FF, D_MODEL), lambda m: (0, 0), pipeline_mode=pl.Buffered(1)), row, vec, vec],
        out_specs=[row, row],
        out_shape=[jax.ShapeDtypeStruct((M, D_MODEL), F32), jax.ShapeDtypeStruct((M, D_MODEL), BF16)],
        compiler_params=_cparams(("parallel",)),
        name="ffn_down_ln2",
    )(g16, wdown16, h32, ln_g, ln_b)


def _rope_tables(pos):
    half = HEAD_DIM // 2
    inv = ROPE_THETA ** (-jnp.arange(half, dtype=F32) / half)
    ang = pos.astype(F32)[:, None] * inv[None, :]
    cos, sin = jnp.cos(ang), jnp.sin(ang)
    return jnp.concatenate([cos, cos], -1), jnp.concatenate([-sin, sin], -1)


N_QKV = GROUP_WIDTH + 6 * HEAD_DIM
N_GATE = 3 * HEADS
N_REST = 2 * GROUP_WIDTH + 2 * GROUP_WIDTH + 3 * GROUP_WIDTH


W_IN_TILE = 256


def _w_in_rows_kernel(w_ref, o_ref, *, layer, depth):
    i = pl.program_id(0)
    n_kt = w_ref.shape[1] // depth
    row = lax.broadcasted_iota(jnp.int32, (W_IN_TILE, LANES), 0)
    keep = (i < pl.num_programs(0) - 1) | (row < N_GATE)
    for kt in range(n_kt):
        x = w_ref[:, kt * depth + layer, :]
        o_ref[:, kt * LANES:(kt + 1) * LANES] = jnp.where(keep, x, 0.0).astype(BF16)


def _w_in_rows(w_in, layer):
    depth, K, N = w_in.shape
    n_kt = K // LANES
    view = w_in.reshape(depth, n_kt, LANES, N).transpose(3, 1, 0, 2).reshape(N, n_kt * depth, LANES)
    n_rest, n_qkv = N_REST // W_IN_TILE, N_QKV // W_IN_TILE

    def src_row(i):
        return jnp.where(i < n_rest, N_QKV + N_GATE + i * W_IN_TILE,
                         jnp.where(i < n_rest + n_qkv, (i - n_rest) * W_IN_TILE, N_QKV))

    return pl.pallas_call(
        functools.partial(_w_in_rows_kernel, layer=layer, depth=depth),
        grid=(N_PROJ // W_IN_TILE,),
        in_specs=[pl.BlockSpec((pl.Element(W_IN_TILE), pl.Element(n_kt * depth), pl.Element(LANES)),
                               lambda i: (src_row(i), 0, 0))],
        out_specs=pl.BlockSpec((W_IN_TILE, K), lambda i: (i, 0)),
        out_shape=jax.ShapeDtypeStruct((N_PROJ, K), BF16),
        compiler_params=_cparams(("parallel",)),
        name="w_in_rows",
    )(view)


def _row2(v):
    return v.reshape(1, -1)


def _layer_prompt(x32, x16, lw, B, T):
    M = B * T
    proj32, proj16, sb_state = _proj(x16, lw["w_in"], min(PROJ_TM, M))
    pos = jnp.tile(jnp.arange(T), B)
    cos2, sin2 = _rope_tables(pos)
    tm_r = min(256, T)
    pe_t = jnp.tile(lw["nsa_pe"], (1, tm_r // NSA_BLOCK, 1))
    q16, _, kv16, xkv, rows4, rows2 = _rope(proj32, cos2, sin2, pe_t, tm_r)

    n_blk = T // NSA_BLOCK
    cmp = _compress(xkv.reshape(2, B * n_blk, NSA_BLOCK * HEAD_DIM), lw["nsa_cw1"], lw["nsa_cw2"],
                    min(128, B * n_blk))
    kcmp = cmp[0].reshape(B, n_blk, HEAD_DIM)
    vcmp = cmp[1].reshape(B, n_blk, HEAD_DIM)
    o_nsa = _nsa_prompt(q16, kv16, kcmp, vcmp, proj32, B, T)

    o_conv, conv_tail = _conv_prompt(proj32, lw["conv_w"], lw["conv_b"], lw["conv_ln_g"], lw["conv_ln_b"], B, T)
    o_gm = _gm_prompt(proj32, lw["gm_ws"], lw["gm_bias_t"], lw["gm_ln_g"], lw["gm_ln_b"], B, T)
    o_sb = _sb_prompt(proj16, B, T)

    h32, h16 = _outproj((o_nsa, o_conv, o_gm, o_sb), lw["w_out"], x32, lw["ln1_g"], lw["ln1_b"], min(512, M))
    g16, tail_a, tail_b = _ffn_up(h16, lw["ffn_up"], lw["ffn_conv_w"], lw["ffn_conv_b"], None, shift=1, seq_len=T)
    y32, y16 = _ffn_down(g16, lw["ffn_down"], h32, lw["ln2_g"], lw["ln2_b"], min(FFN_DOWN_TM, M))

    nsa_rows = rows4.reshape(B, T, 4, HEAD_DIM)
    wlen = min(NSA_WINDOW, T)
    win_rows = rows2.reshape(B, T, 2, HEAD_DIM)[:, T - wlen:]
    sb_rows = sb_state.reshape(B, T, 2, HEADS, HEAD_DIM)
    conv_rows = conv_tail[:, CONV_HIST - (CONV_WIDTH - 1):]
    tiles_per_seq = tail_a.shape[0] // B
    ffn_tail = jnp.concatenate([tail_a, tail_b], -1)[tiles_per_seq - 1::tiles_per_seq]
    ffn_rows = ffn_tail[:, FFN_CARRY - (FFN_CONV_WIDTH - 1):]
    return y32, y16, (nsa_rows, sb_rows, win_rows, conv_rows, ffn_rows)


def _layer_sample(x32, x16, lw, layer, B, T, cache_nsa, cache_sb, win_all, conv_state, ffn_state, page_table):
    M = B * T
    n_pages = page_table.shape[1]
    P = n_pages * PAGE_SIZE
    proj32, proj16, sb_state = _proj(x16, lw["w_in"], min(PROJ_TM, M))
    pos = jnp.repeat(P + jnp.arange(T), B)
    cos2, sin2 = _rope_tables(pos)
    tm_r = min(256, M)
    pe_t = jnp.tile(lw["nsa_pe"], (1, tm_r // NSA_BLOCK, 1))
    q16, kv32, kv16, _, _, _ = _rope(proj32, cos2, sin2, pe_t, tm_r)

    to_b = lambda a: a.reshape(T, B, -1).swapaxes(0, 1)
    kv32_b = to_b(kv32)

    xkv, ksv = _gather_nsa(cache_nsa, layer, page_table, lw["nsa_pe"])
    n_past = P // NSA_BLOCK
    cmp_past = _compress_rows(xkv, lw["nsa_cw1"], lw["nsa_cw2"], min(256, B * n_past)).reshape(2, B, n_past, HEAD_DIM)
    n_blk = -(-(P + T) // NSA_BLOCK)
    n_new = n_blk - n_past
    new_rows = jnp.pad(kv32_b[:, :, :2 * HEAD_DIM], ((0, 0), (0, n_new * NSA_BLOCK - T), (0, 0)))
    new_rows = new_rows.reshape(B, n_new, NSA_BLOCK, 2, HEAD_DIM)
    x_new = jnp.stack([new_rows[:, :, :, 0] + lw["nsa_pe"][0], new_rows[:, :, :, 1] + lw["nsa_pe"][1]])
    x_new = x_new.reshape(2, B * n_new, NSA_BLOCK * HEAD_DIM).astype(BF16)
    cmp_new = _compress(x_new, lw["nsa_cw1"], lw["nsa_cw2"], B * n_new).reshape(2, B, n_new, HEAD_DIM)
    n_pad = -(-n_blk // SUBLANES) * SUBLANES
    cmp = jnp.concatenate([cmp_past, cmp_new, jnp.zeros((2, B, n_pad - n_blk, HEAD_DIM), F32)], axis=2)

    g32_b = to_b(proj32[:, C_G:C_G + LANES])
    o_nsa = _nsa_sample(to_b(q16).astype(F32), to_b(kv16).astype(F32), win_all, layer, cmp[0], cmp[1], g32_b, ksv)
    o_sb = _sb_sample(to_b(proj16[:, C_SQ:C_SQ + 3 * GROUP_WIDTH]).astype(F32), cache_sb, layer, page_table)
    to_t = lambda a: a.swapaxes(0, 1).reshape(M, -1).astype(BF16)
    o_nsa, o_sb = to_t(o_nsa), to_t(o_sb)

    proj32_3d = proj32.reshape(T, B, N_PROJ)
    o_conv, glu = _conv_sample(conv_state.swapaxes(0, 1), proj32_3d, lw["conv_w"], lw["conv_b"],
                               lw["conv_ln_g"], lw["conv_ln_b"])
    o_gm, vn = _gm_sample(proj32_3d, lw["gm_wexp"], lw["gm_bexp"], lw["gm_ln_g"], lw["gm_ln_b"])

    mix = (o_nsa, o_conv.reshape(M, GROUP_WIDTH), o_gm.reshape(M, GROUP_WIDTH), o_sb)
    h32, h16 = _outproj(mix, lw["w_out"], x32, lw["ln1_g"], lw["ln1_b"], min(512, M))
    ffn_state_tm = ffn_state.swapaxes(0, 1).reshape(2 * B, 2 * D_FF)
    g16, tail_a, tail_b = _ffn_up(h16, lw["ffn_up"], lw["ffn_conv_w"], lw["ffn_conv_b"], ffn_state_tm,
                                  shift=B, seq_len=T)
    y32, y16 = _ffn_down(g16, lw["ffn_down"], h32, lw["ln2_g"], lw["ln2_b"], min(FFN_DOWN_TM, M))

    nsa_rows = kv32_b[:, :, :4 * HEAD_DIM].reshape(B, T, 4, HEAD_DIM)
    win_rows = kv32_b[:, :, 4 * HEAD_DIM:].reshape(B, T, 2, HEAD_DIM)
    sb_rows = sb_state.reshape(T, B, 2, HEADS, HEAD_DIM).swapaxes(0, 1)
    ccat = jnp.concatenate([conv_state, glu.swapaxes(0, 1)], 1)
    conv_rows = ccat[:, ccat.shape[1] - (CONV_WIDTH - 1):]
    ffn_rows = jnp.concatenate([tail_a, tail_b], -1).reshape(2, B, 2 * D_FF).swapaxes(0, 1)
    gm_v = vn.swapaxes(0, 1)
    return y32, y16, (nsa_rows, sb_rows, win_rows, conv_rows, ffn_rows, gm_v)


def _win_join_kernel(st_ref, new_ref, o_ref, *, drop):
    kept = st_ref.shape[2] - drop
    o_ref[0, :, 0:kept, :] = st_ref[0, :, drop:, :]
    o_ref[0, :, kept:, :] = new_ref[0]


def _win_join(win_all, new_all, keep):
    depth, B, w2, d = win_all.shape
    t2 = new_all.shape[2]
    bb = min(8, B)
    blk = lambda rows: pl.BlockSpec((1, bb, rows, d), lambda l, i: (l, i, 0, 0))
    return pl.pallas_call(
        functools.partial(_win_join_kernel, drop=w2 + t2 - 2 * keep),
        grid=(depth, B // bb),
        in_specs=[blk(w2), blk(t2)],
        out_specs=blk(2 * keep),
        out_shape=jax.ShapeDtypeStruct((depth, B, 2 * keep, d), F32),
        compiler_params=_cparams(("parallel", "parallel")),
        name="win_join",
    )(win_all, new_all)


def _layer_weights(l, T_s, w_in, nsa_pe, nsa_cw1, nsa_cw2, conv_w, conv_b, conv_ln_g, conv_ln_b,
                   gm_ln_g, gm_ln_b, gm_ws, gm_bias, w_out, ln1_g, ln1_b,
                   ffn_up, ffn_conv_w, ffn_conv_b, ffn_down, ln2_g, ln2_b):
    cs = min(GM_CHUNK, T_s)
    ws_s = jnp.tril(gm_ws[l][:, :cs, :cs])
    wexp = jnp.repeat(ws_s.transpose(1, 2, 0), HEAD_DIM, axis=2)
    bexp = jnp.repeat(gm_bias[l][:, :cs].T, HEAD_DIM, axis=1)
    return {
        "w_in": _w_in_rows(w_in, l),
        "nsa_pe": nsa_pe[l],
        "nsa_cw1": nsa_cw1[l].astype(BF16), "nsa_cw2": nsa_cw2[l].astype(BF16),
        "conv_w": conv_w[l], "conv_b": _row2(conv_b[l]),
        "conv_ln_g": _row2(conv_ln_g[l]), "conv_ln_b": _row2(conv_ln_b[l]),
        "gm_ln_g": _row2(gm_ln_g[l]), "gm_ln_b": _row2(gm_ln_b[l]),
        "gm_ws": gm_ws[l], "gm_bias_t": gm_bias[l].T, "gm_wexp": wexp, "gm_bexp": bexp,
        "w_out": w_out[l].astype(BF16), "ln1_g": _row2(ln1_g[l]), "ln1_b": _row2(ln1_b[l]),
        "ffn_up": ffn_up[l].astype(BF16), "ffn_conv_w": ffn_conv_w[l], "ffn_conv_b": _row2(ffn_conv_b[l]),
        "ffn_down": ffn_down[l].astype(BF16), "ln2_g": _row2(ln2_g[l]), "ln2_b": _row2(ln2_b[l]),
    }


def kernel(x_prompt, x_sample, cache_nsa_kv, cache_sb_kv, state_nsa_win, state_conv, state_ffn, page_table,
           w_in, nsa_pe, nsa_cw1, nsa_cw2, conv_w, conv_b, conv_ln_g, conv_ln_b, gm_ln_g, gm_ln_b, gm_ws, gm_bias,
           w_out, ln1_g, ln1_b, ffn_up, ffn_conv_w, ffn_conv_b, ffn_down, ln2_g, ln2_b):
    Bp, Tp, _ = x_prompt.shape
    Bs, Ts, _ = x_sample.shape
    depth = w_in.shape[0]
    cache_nsa = cache_nsa_kv.reshape(cache_nsa_kv.shape[:2] + (-1, HEAD_DIM))
    cache_sb = cache_sb_kv.reshape(cache_sb_kv.shape[:2] + (-1, HEAD_DIM))
    win_all = state_nsa_win.reshape(state_nsa_win.shape[:2] + (-1, HEAD_DIM))

    xp32 = x_prompt.reshape(Bp * Tp, D_MODEL)
    xs32 = x_sample.swapaxes(0, 1).reshape(Ts * Bs, D_MODEL)
    xp16, xs16 = xp32.astype(BF16), xs32.astype(BF16)
    outs_p, outs_s = [], []
    for l in range(depth):
        lw = _layer_weights(l, Ts, w_in, nsa_pe, nsa_cw1, nsa_cw2, conv_w, conv_b, conv_ln_g, conv_ln_b,
                            gm_ln_g, gm_ln_b, gm_ws, gm_bias, w_out, ln1_g, ln1_b,
                            ffn_up, ffn_conv_w, ffn_conv_b, ffn_down, ln2_g, ln2_b)
        xp32, xp16, sp = _layer_prompt(xp32, xp16, lw, Bp, Tp)
        xs32, xs16, ss = _layer_sample(xs32, xs16, lw, l, Bs, Ts, cache_nsa, cache_sb, win_all,
                                       state_conv[l], state_ffn[l], page_table)
        outs_p.append(sp)
        outs_s.append(ss)
    y_p = xp32.reshape(Bp, Tp, D_MODEL)
    y_s = xs32.reshape(Ts, Bs, D_MODEL).swapaxes(0, 1)
    st = lambda outs, i: jnp.stack([o[i] for o in outs])
    keep = min(NSA_WINDOW, page_table.shape[1] * PAGE_SIZE + Ts)
    win_s = _win_join(win_all, st(outs_s, 2).reshape(depth, Bs, 2 * Ts, HEAD_DIM), keep)
    win_s = win_s.reshape(depth, Bs, keep, 2, HEAD_DIM)
    return (y_p, y_s, st(outs_p, 0), st(outs_s, 0), st(outs_p, 1), st(outs_s, 1), st(outs_p, 2), win_s,
            st(outs_p, 3), st(outs_s, 3), st(outs_p, 4), st(outs_s, 4), st(outs_s, 5))
```
